```python
import math
import jax
import jax.numpy as jnp
from jax import lax
import numpy as np

D_MODEL = 1024
BATCH = 8
SEQ = 2048
DEPTH = 2

CTX_LEN = 256
GRID_W = 64
HEAD_DIM = 64
ROPE_THETA = 10000.0
Q_BLOCK = 128
A_Q_HEADS = 8
A_KV_HEADS = 2
A_GROUP = A_Q_HEADS // A_KV_HEADS
A_Q_W = A_Q_HEADS * HEAD_DIM
A_KV_W = A_KV_HEADS * HEAD_DIM
F_GROUPS = 4
F_GROUP_W = 128
B_W = F_GROUPS * F_GROUP_W
C_HEADS = 4
C_V_DIM = 2 * HEAD_DIM
C_QK_W = C_HEADS * 2 * HEAD_DIM
C_V_W = C_HEADS * C_V_DIM
N_BRANCH = 3
GATE_W = N_BRANCH * D_MODEL
IN_W = A_Q_W + 2 * A_KV_W + B_W + 2 * C_QK_W + C_V_W + GATE_W
N_EXPERTS = 256
TOP_K = 8
N_GROUPS = 8
TOPK_GROUPS = 4
EXPERT_FF = 256
SHARED_FF = 256
ROUTED_SCALE = 2.5
MOE_BLOCK = 128
DEEPNORM_ALPHA = (2 * DEPTH) ** 0.25
DEEPNORM_BETA = (8 * DEPTH) ** -0.25
LN_EPS = 1e-5
RMS_EPS = 1e-6

kernel_name = 'hybrid_gqa_fourier_diffattn_moe_dit'


def _layer_norm(x, g=None, b=None):
    xf = x.astype(jnp.float32)
    mu = jnp.mean(xf, -1, keepdims=True)
    var = jnp.mean(jnp.square(xf - mu), -1, keepdims=True)
    y = (xf - mu) * lax.rsqrt(var + LN_EPS)
    if g is not None:
        y = y * g.astype(jnp.float32) + b.astype(jnp.float32)
    return y.astype(x.dtype)


def _rms_norm(x, g):
    xf = x.astype(jnp.float32)
    y = xf * lax.rsqrt(jnp.mean(xf * xf, -1, keepdims=True) + RMS_EPS)
    return (y * g.astype(jnp.float32)).astype(x.dtype)


def _modulate(h, shift, scale):
    return _layer_norm(h) * (1 + scale) + shift


def _axial_rope_tables(n_tok):
    rows_n = n_tok // GRID_W
    row = jnp.repeat(jnp.arange(rows_n, dtype=jnp.float32), GRID_W)
    col = jnp.tile(jnp.arange(GRID_W, dtype=jnp.float32), rows_n)
    axis_dim = HEAD_DIM // 2
    inv = jnp.power(ROPE_THETA, -jnp.arange(0, axis_dim, 2, dtype=jnp.float32) / axis_dim)
    ar = row[:, None] * inv[None]
    ac = col[:, None] * inv[None]
    ang = jnp.concatenate([ar, ar, ac, ac], -1)
    return jnp.cos(ang), jnp.sin(ang)


def _apply_rope(x, cos, sin):
    shape = (1, x.shape[1]) + (1,) * (x.ndim - 3) + (x.shape[-1],)
    cs = cos.reshape(shape).astype(x.dtype)
    sn = sin.reshape(shape).astype(x.dtype)
    a, b, c, d = jnp.split(x, 4, axis=-1)
    rot = jnp.concatenate([-b, a, -d, c], -1)
    return x * cs + rot * sn


def _blocks(t):
    b, s = t.shape[:2]
    t = t.reshape((b, s // Q_BLOCK, Q_BLOCK) + t.shape[2:])
    return jnp.moveaxis(t, 1, 0)


def _unblocks(t):
    t = jnp.moveaxis(t, 0, 1)
    return t.reshape((t.shape[0], t.shape[1] * t.shape[2]) + t.shape[3:])


def _gqa_attend(q, k, v):
    scale = HEAD_DIM ** -0.5

    def one(qb):
        s = jnp.einsum('bqhgd,bkhd->bhgqk', qb, k).astype(jnp.float32) * scale
        p = jax.nn.softmax(s, axis=-1).astype(v.dtype)
        return jnp.einsum('bhgqk,bkhd->bqhgd', p, v)

    return _unblocks(lax.map(one, _blocks(q)))


def _diff_attend(q1, q2, k1, k2, v, lam):
    scale = HEAD_DIM ** -0.5

    def one(qs):
        a, b = qs
        p1 = jax.nn.softmax(jnp.einsum('bqhd,bkhd->bhqk', a, k1).astype(jnp.float32) * scale, axis=-1)
        p2 = jax.nn.softmax(jnp.einsum('bqhd,bkhd->bhqk', b, k2).astype(jnp.float32) * scale, axis=-1)
        p = (p1 - lam * p2).astype(v.dtype)
        return jnp.einsum('bhqk,bkhe->bqhe', p, v)

    return _unblocks(lax.map(one, (_blocks(q1), _blocks(q2))))


def _fourier_mix(f):
    y = jnp.fft.fft2(f.astype(jnp.float32), axes=(1, 3), norm='ortho')
    return jnp.real(y).astype(f.dtype)


def _split_in(p):
    sizes = [A_Q_W, A_KV_W, A_KV_W, B_W, C_QK_W, C_QK_W, C_V_W, GATE_W]
    idx = [int(v) for v in np.cumsum(sizes)[:-1]]
    return jnp.split(p, idx, axis=-1)


def _mixer_inputs(u, w_in, qn, kn, rope):
    b, s, _ = u.shape
    qa, ka, va, fb, qc, kc, vc, gates = _split_in(u @ w_in)
    qa = _rms_norm(qa.reshape(b, s, A_KV_HEADS, A_GROUP, HEAD_DIM), qn)
    ka = _rms_norm(ka.reshape(b, s, A_KV_HEADS, HEAD_DIM), kn)
    va = va.reshape(b, s, A_KV_HEADS, HEAD_DIM)
    qc = qc.reshape(b, s, C_HEADS, 2, HEAD_DIM)
    kc = kc.reshape(b, s, C_HEADS, 2, HEAD_DIM)
    vc = vc.reshape(b, s, C_HEADS, C_V_DIM)
    if rope is not None:
        cos, sin = rope
        qa = _apply_rope(qa, cos, sin)
        ka = _apply_rope(ka, cos, sin)
        qc = _apply_rope(qc, cos, sin)
        kc = _apply_rope(kc, cos, sin)
    return {'qa': qa, 'ka': ka, 'va': va, 'fb': fb, 'qc': qc, 'kc': kc, 'vc': vc, 'gates': gates}


def _mixer_output(pq, k_a, v_a, k_c, v_c, lam, lam_init, subln_g, w_br_a, w_br_b, w_br_c, w_out):
    b, s = pq['fb'].shape[:2]
    o_a = _gqa_attend(pq['qa'], k_a, v_a).reshape(b, s, A_Q_W)
    o_b = _fourier_mix(pq['fb'].reshape(b, s, F_GROUPS, F_GROUP_W)).reshape(b, s, B_W)
    o_c = _diff_attend(pq['qc'][..., 0, :], pq['qc'][..., 1, :], k_c[..., 0, :], k_c[..., 1, :], v_c, lam)
    o_c = (_rms_norm(o_c, subln_g) * (1.0 - lam_init)).reshape(b, s, C_V_W)
    g = jax.nn.sigmoid(pq['gates'].reshape(b, s, N_BRANCH, D_MODEL))
    y = g[:, :, 0] * (o_a @ w_br_a) + g[:, :, 1] * (o_b @ w_br_b) + g[:, :, 2] * (o_c @ w_br_c)
    return y @ w_out


def _swiglu(h, wg, wu, wd):
    return (jax.nn.silu(h @ wg) * (h @ wu)) @ wd


def _route(h, w_r, b_r):
    t = h.shape[0]
    scores = jax.nn.sigmoid((h @ w_r).astype(jnp.float32))
    biased = scores + b_r.astype(jnp.float32)
    grp = biased.reshape(t, N_GROUPS, N_EXPERTS // N_GROUPS)
    grp_score = jnp.sum(lax.top_k(grp, 2)[0], -1)
    _, gidx = lax.top_k(grp_score, TOPK_GROUPS)
    gmask = jnp.sum(jax.nn.one_hot(gidx, N_GROUPS, dtype=jnp.float32), 1) > 0
    emask = jnp.repeat(gmask, N_EXPERTS // N_GROUPS, axis=1)
    _, idx = lax.top_k(jnp.where(emask, biased, -jnp.inf), TOP_K)
    w = jnp.take_along_axis(scores, idx, -1)
    w = w / jnp.sum(w, -1, keepdims=True) * ROUTED_SCALE
    return idx, w


def _routed_experts(h, idx, w, wg, wu, wd):
    t, d = h.shape
    n_assign = t * TOP_K
    n_slots = (n_assign + N_EXPERTS * (MOE_BLOCK - 1) + MOE_BLOCK - 1) // MOE_BLOCK * MOE_BLOCK
    nb = n_slots // MOE_BLOCK
    flat_e = idx.reshape(-1)
    flat_t = jnp.repeat(jnp.arange(t, dtype=jnp.int32), TOP_K)
    flat_w = w.reshape(-1)
    order = jnp.argsort(flat_e)
    e_sorted = flat_e[order]
    counts = jnp.bincount(flat_e, length=N_EXPERTS)
    padded = (counts + MOE_BLOCK - 1) // MOE_BLOCK * MOE_BLOCK
    pad_end = jnp.cumsum(padded)
    pad_start = pad_end - padded
    start = jnp.cumsum(counts) - counts
    dest = pad_start[e_sorted] + jnp.arange(n_assign, dtype=jnp.int32) - start[e_sorted]
    slot_tok = jnp.full((n_slots,), t, jnp.int32).at[dest].set(flat_t[order])
    slot_w = jnp.zeros((n_slots,), flat_w.dtype).at[dest].set(flat_w[order])
    blk_start = jnp.arange(nb, dtype=jnp.int32) * MOE_BLOCK
    blk_e = jnp.minimum(jnp.searchsorted(pad_end, blk_start, side='right'), N_EXPERTS - 1)
    h_pad = jnp.concatenate([h, jnp.zeros((1, d), h.dtype)], 0)

    def one(args):
        tok, e = args
        return _swiglu(h_pad[tok], wg[e], wu[e], wd[e])

    y = lax.map(one, (slot_tok.reshape(nb, MOE_BLOCK), blk_e))
    y = y.reshape(n_slots, d) * slot_w[:, None].astype(h.dtype)
    return jnp.zeros((t + 1, d), h.dtype).at[slot_tok].add(y)[:t]


def _moe(h, w_r, b_r, sg, su, sd, eg, eu, ed):
    idx, w = _route(h, w_r, b_r)
    return _swiglu(h, sg, su, sd) + _routed_experts(h, idx, w, eg, eu, ed)


def setup_inputs(seed: int = 0) -> dict:
    key = jax.random.key(seed)
    ks = jax.random.split(key, 32)
    D = D_MODEL

    def nrm(k, shape, s):
        return jax.random.normal(k, shape, jnp.float32) * s

    return {
        'x': nrm(ks[0], (BATCH, SEQ, D), 1.0),
        'c': nrm(ks[1], (BATCH, D), 1.0),
        'ctx': nrm(ks[2], (BATCH, CTX_LEN, D), 1.0),
        'c_ctx': nrm(ks[3], (D,), 1.0),
        'w_mod': nrm(ks[4], (DEPTH, D, 6 * D), 0.5 * D ** -0.5),
        'b_mod': nrm(ks[5], (DEPTH, 6 * D), 0.02),
        'w_in': nrm(ks[6], (DEPTH, D, IN_W), D ** -0.5),
        'qn_a': 1.0 + nrm(ks[7], (DEPTH, HEAD_DIM), 0.02),
        'kn_a': 1.0 + nrm(ks[8], (DEPTH, HEAD_DIM), 0.02),
        'lam_q1': nrm(ks[9], (DEPTH, HEAD_DIM), 0.1),
        'lam_k1': nrm(ks[10], (DEPTH, HEAD_DIM), 0.1),
        'lam_q2': nrm(ks[11], (DEPTH, HEAD_DIM), 0.1),
        'lam_k2': nrm(ks[12], (DEPTH, HEAD_DIM), 0.1),
        'subln_c': 1.0 + nrm(ks[13], (DEPTH, C_V_DIM), 0.02),
        'w_br_a': nrm(ks[14], (DEPTH, A_Q_W, D), A_Q_W ** -0.5),
        'w_br_b': nrm(ks[15], (DEPTH, B_W, D), B_W ** -0.5),
        'w_br_c': nrm(ks[16], (DEPTH, C_V_W, D), C_V_W ** -0.5),
        'w_out': nrm(ks[17], (DEPTH, D, D), DEEPNORM_BETA * D ** -0.5),
        'ln1_g': 1.0 + nrm(ks[18], (DEPTH, D), 0.02),
        'ln1_b': nrm(ks[19], (DEPTH, D), 0.02),
        'w_router': nrm(ks[20], (DEPTH, D, N_EXPERTS), D ** -0.5),
        'b_router': nrm(ks[21], (DEPTH, N_EXPERTS), 0.01),
        'w_sh_gate': nrm(ks[22], (DEPTH, D, SHARED_FF), D ** -0.5),
        'w_sh_up': nrm(ks[23], (DEPTH, D, SHARED_FF), D ** -0.5),
        'w_sh_down': nrm(ks[24], (DEPTH, SHARED_FF, D), DEEPNORM_BETA * SHARED_FF ** -0.5),
        'w_e_gate': nrm(ks[25], (DEPTH, N_EXPERTS, D, EXPERT_FF), D ** -0.5),
        'w_e_up': nrm(ks[26], (DEPTH, N_EXPERTS, D, EXPERT_FF), D ** -0.5),
        'w_e_down': nrm(ks[27], (DEPTH, N_EXPERTS, EXPERT_FF, D), DEEPNORM_BETA * EXPERT_FF ** -0.5),
        'ln2_g': 1.0 + nrm(ks[28], (DEPTH, D), 0.02),
        'ln2_b': nrm(ks[29], (DEPTH, D), 0.02),
    }


def reference(x, c, ctx, c_ctx, w_mod, b_mod, w_in, qn_a, kn_a, lam_q1, lam_k1, lam_q2, lam_k2,
              subln_c, w_br_a, w_br_b, w_br_c, w_out, ln1_g, ln1_b, w_router, b_router,
              w_sh_gate, w_sh_up, w_sh_down, w_e_gate, w_e_up, w_e_down, ln2_g, ln2_b):
    b, s, d = x.shape
    lc = ctx.shape[1]
    rope = _axial_rope_tables(s)
    sc = jax.nn.silu(c)
    scc = jax.nn.silu(c_ctx)
    h_lat, h_ctx = x, ctx
    for l in range(DEPTH):
        last = l == DEPTH - 1
        m_lat = (sc @ w_mod[l] + b_mod[l]).reshape(b, 1, 6, d)
        m_ctx = (scc @ w_mod[l] + b_mod[l]).reshape(1, 1, 6, d)
        lam = (jnp.exp(jnp.sum(lam_q1[l].astype(jnp.float32) * lam_k1[l].astype(jnp.float32)))
               - jnp.exp(jnp.sum(lam_q2[l].astype(jnp.float32) * lam_k2[l].astype(jnp.float32))))
        lam_init = 0.8 - 0.6 * math.exp(-0.3 * l)
        lam = lam + lam_init

        u_lat = _modulate(h_lat, m_lat[:, :, 0], m_lat[:, :, 1])
        u_ctx = _modulate(h_ctx, m_ctx[:, :, 0], m_ctx[:, :, 1])
        p_lat = _mixer_inputs(u_lat, w_in[l], qn_a[l], kn_a[l], rope)
        p_ctx = _mixer_inputs(u_ctx, w_in[l], qn_a[l], kn_a[l], None)
        ka_all = jnp.concatenate([p_ctx['ka'], p_lat['ka']], 1)
        va_all = jnp.concatenate([p_ctx['va'], p_lat['va']], 1)
        kc_all = jnp.concatenate([p_ctx['kc'], p_lat['kc']], 1)
        vc_all = jnp.concatenate([p_ctx['vc'], p_lat['vc']], 1)
        mix_lat = _mixer_output(p_lat, ka_all, va_all, kc_all, vc_all, lam, lam_init, subln_c[l],
                                w_br_a[l], w_br_b[l], w_br_c[l], w_out[l])
        h_lat = _layer_norm(DEEPNORM_ALPHA * h_lat + m_lat[:, :, 2] * mix_lat, ln1_g[l], ln1_b[l])
        if not last:
            mix_ctx = _mixer_output(p_ctx, p_ctx['ka'], p_ctx['va'], p_ctx['kc'], p_ctx['vc'], lam, lam_init,
                                    subln_c[l], w_br_a[l], w_br_b[l], w_br_c[l], w_out[l])
            h_ctx = _layer_norm(DEEPNORM_ALPHA * h_ctx + m_ctx[:, :, 2] * mix_ctx, ln1_g[l], ln1_b[l])

        u_lat = _modulate(h_lat, m_lat[:, :, 3], m_lat[:, :, 4])
        moe_args = (w_router[l], b_router[l], w_sh_gate[l], w_sh_up[l], w_sh_down[l],
                    w_e_gate[l], w_e_up[l], w_e_down[l])
        if not last:
            u_ctx = _modulate(h_ctx, m_ctx[:, :, 3], m_ctx[:, :, 4])
            tok = jnp.concatenate([u_ctx.reshape(-1, d), u_lat.reshape(-1, d)], 0)
            f = _moe(tok, *moe_args)
            f_ctx = f[: b * lc].reshape(b, lc, d)
            f_lat = f[b * lc:].reshape(b, s, d)
            h_ctx = _layer_norm(DEEPNORM_ALPHA * h_ctx + m_ctx[:, :, 5] * f_ctx, ln2_g[l], ln2_b[l])
        else:
            f_lat = _moe(u_lat.reshape(-1, d), *moe_args).reshape(b, s, d)
        h_lat = _layer_norm(DEEPNORM_ALPHA * h_lat + m_lat[:, :, 5] * f_lat, ln2_g[l], ln2_b[l])
    return h_lat
```

```python
import functools
import math

import numpy as np
import jax
import jax.numpy as jnp
from jax import lax
from jax.experimental import pallas as pl
from jax.experimental.pallas import tpu as pltpu

F32 = jnp.float32
BF16 = jnp.bfloat16
U32 = jnp.uint32

D_MODEL = 1024
CTX_LEN = 256
GRID_W = 64
HEAD_DIM = 64
ROPE_THETA = 10000.0
A_Q_HEADS = 8
A_KV_HEADS = 2
A_Q_W = A_Q_HEADS * HEAD_DIM
A_KV_W = A_KV_HEADS * HEAD_DIM
F_GROUPS = 4
F_GROUP_W = 128
B_W = F_GROUPS * F_GROUP_W
C_HEADS = 4
C_V_DIM = 2 * HEAD_DIM
C_QK_W = C_HEADS * 2 * HEAD_DIM
C_V_W = C_HEADS * C_V_DIM
N_BRANCH = 3
GATE_W = N_BRANCH * D_MODEL
N_EXPERTS = 256
TOP_K = 8
N_GROUPS = 8
TOPK_GROUPS = 4
EXPERT_FF = 256
SHARED_FF = 256
ROUTED_SCALE = 2.5
LN_EPS = 1e-5
RMS_EPS = 1e-6

LANES = 128
SLAB = D_MODEL // LANES
ROW_TILE = 256
MOE_BLOCK = 128
PAD_ROWS = 2 * MOE_BLOCK // TOP_K
COMBINE_TILE = 128
VMEM_LIMIT = 56 * 1024 * 1024

_SQRT_HALF_SCALE = HEAD_DIM ** -0.5


def _cparams(sem):
    return pltpu.CompilerParams(dimension_semantics=sem, vmem_limit_bytes=VMEM_LIMIT)


def _dot(a, b):
    return jnp.dot(a, b, preferred_element_type=F32)


def _dot_nt(a, b):
    return lax.dot_general(a, b, (((1,), (1,)), ((), ())), preferred_element_type=F32)


def _layer_norm_rows(x):
    mu = jnp.mean(x, axis=-1, keepdims=True)
    xc = x - mu
    var = jnp.mean(xc * xc, axis=-1, keepdims=True)
    return xc * lax.rsqrt(var + LN_EPS)


def _dot_split(x, e):
    hi = x.astype(BF16)
    lo = (x - hi.astype(F32)).astype(BF16)
    return _dot(hi, e) + _dot(lo, e)


def _mod_kernel(c_ref, w_ref, b_ref, o_ref):
    c = c_ref[...]
    sc = c * jax.nn.sigmoid(c)
    o_ref[...] = _dot(sc.astype(BF16), w_ref[...].astype(BF16)) + b_ref[...]


def _mod_call(cc, w_mod_l, b_mod_l):
    r = cc.shape[0]
    n = w_mod_l.shape[1]
    tn = D_MODEL
    return pl.pallas_call(
        _mod_kernel,
        out_shape=jax.ShapeDtypeStruct((r, n), F32),
        grid=(n // tn,),
        in_specs=[
            pl.BlockSpec((r, D_MODEL), lambda j: (0, 0)),
            pl.BlockSpec((D_MODEL, tn), lambda j: (0, j)),
            pl.BlockSpec((1, tn), lambda j: (0, j)),
        ],
        out_specs=pl.BlockSpec((r, tn), lambda j: (0, j)),
        compiler_params=_cparams(("arbitrary",)),
        name="mod_vectors",
    )(cc, w_mod_l, b_mod_l.reshape(1, n))


def _rope_cols(x, cos, sin_up, sin_dn):
    cols = []
    for c in range(x.shape[1] // LANES):
        xc = x[:, c * LANES:(c + 1) * LANES]
        up = pltpu.roll(xc, LANES - 16, axis=1)
        dn = pltpu.roll(xc, 16, axis=1)
        cols.append(xc * cos + up * sin_up + dn * sin_dn)
    return jnp.concatenate(cols, axis=1) if len(cols) > 1 else cols[0]


def _inproj_kernel(h_ref, mod_ref, cos_ref, su_ref, sd_ref,
                   wq_ref, wk_ref, wv_ref, wf_ref, wqc_ref, wkc_ref, wvc_ref,
                   qn_ref, kn_ref, eq_ref, ek_ref,
                   qa_o, ka_o, va_o, fb_o, qc_o, kc_o, vc_o):
    h = h_ref[...]
    shift = mod_ref[0, 0:1, :]
    scale = mod_ref[0, 1:2, :]
    u = (_layer_norm_rows(h) * (1.0 + scale) + shift).astype(BF16)
    cos = cos_ref[...]
    s_up = su_ref[...]
    s_dn = sd_ref[...]

    q = _dot(u, wq_ref[...])
    ms = _dot_split(q * q, eq_ref[...])
    q = q * lax.rsqrt(ms + RMS_EPS) * qn_ref[...]
    qa_o[...] = (_rope_cols(q, cos, s_up, s_dn) * _SQRT_HALF_SCALE).astype(BF16)

    k = _dot(u, wk_ref[...])
    ms = _dot_split(k * k, ek_ref[...])
    k = k * lax.rsqrt(ms + RMS_EPS) * kn_ref[...]
    ka_o[...] = _rope_cols(k, cos, s_up, s_dn).astype(BF16)

    va_o[...] = _dot(u, wv_ref[...]).astype(BF16)
    fb_o[...] = _dot(u, wf_ref[...]).astype(BF16)
    qc = _dot(u, wqc_ref[...])
    qc_o[...] = (_rope_cols(qc, cos, s_up, s_dn) * _SQRT_HALF_SCALE).astype(BF16)
    kc = _dot(u, wkc_ref[...])
    kc_o[...] = _rope_cols(kc, cos, s_up, s_dn).astype(BF16)
    vc_o[...] = _dot(u, wvc_ref[...]).astype(BF16)


def _inproj_call(h_all, mod, tabs, wts, nb, ntok):
    rows = h_all.shape[0]
    tpb = ntok // ROW_TILE
    cos, s_up, s_dn = tabs

    def full(a):
        return pl.BlockSpec(a.shape, lambda i: (0,) * a.ndim)

    def rowspec(w):
        return pl.BlockSpec((ROW_TILE, w), lambda i: (i, 0))

    tabspec = pl.BlockSpec((ROW_TILE, LANES), lambda i: (i % tpb, 0))
    modspec = pl.BlockSpec((1, 8, D_MODEL),
                           lambda i: (jnp.where(i % tpb == 0, nb, i // tpb), 0, 0))
    widths = (A_Q_W, 2 * A_KV_W, 2 * A_KV_W, B_W, C_QK_W, C_QK_W, C_V_W)
    return pl.pallas_call(
        _inproj_kernel,
        out_shape=[jax.ShapeDtypeStruct((rows, w), BF16) for w in widths],
        grid=(rows // ROW_TILE,),
        in_specs=[rowspec(D_MODEL), modspec, tabspec, tabspec, tabspec]
                 + [full(w) for w in wts],
        out_specs=[rowspec(w) for w in widths],
        compiler_params=_cparams(("parallel",)),
        name="in_projection",
    )(h_all, mod, cos, s_up, s_dn, *wts)


def _softmax_parts(s):
    m = jnp.max(s, axis=-1, keepdims=True)
    e = jnp.exp(s - m)
    return e, jnp.sum(e, axis=-1, keepdims=True)


def _gqa_tile(q_ref, k_ref, v_ref, o_ref, nk):
    lane = lax.broadcasted_iota(jnp.int32, (1, LANES), 1)
    low = lane < HEAD_DIM
    for c in range(A_Q_W // LANES):
        kvh = (2 * c) // (A_Q_HEADS // A_KV_HEADS)
        qc = q_ref[0, :, c * LANES:(c + 1) * LANES]
        kk = k_ref[0, 0:nk, kvh * LANES:(kvh + 1) * LANES]
        vv = v_ref[0, 0:nk, kvh * LANES:(kvh + 1) * LANES]
        halves = []
        for keep in (low, jnp.logical_not(low)):
            qm = jnp.where(keep, qc, jnp.zeros_like(qc))
            e, l = _softmax_parts(_dot_nt(qm, kk))
            halves.append(_dot(e.astype(BF16), vv) * (1.0 / l))
        o_ref[0, :, c * LANES:(c + 1) * LANES] = jnp.where(low, halves[0], halves[1]).astype(BF16)


def _attn_a_kernel(q_ref, k_ref, v_ref, o_ref, *, ntok, first_tile):
    j = pl.program_id(1) + first_tile
    if first_tile == 0:
        @pl.when(j == 0)
        def _():
            _gqa_tile(q_ref, k_ref, v_ref, o_ref, CTX_LEN)

        @pl.when(j != 0)
        def _():
            _gqa_tile(q_ref, k_ref, v_ref, o_ref, ntok)
    else:
        _gqa_tile(q_ref, k_ref, v_ref, o_ref, ntok)


def _diff_tile(lam_ref, q_ref, k_ref, v_ref, g_ref, o_ref, nk, out_scale):
    lane = lax.broadcasted_iota(jnp.int32, (1, LANES), 1)
    low = lane < HEAD_DIM
    lam = lam_ref[0, 0]
    for hd in range(C_HEADS):
        sl = slice(hd * LANES, (hd + 1) * LANES)
        qc = q_ref[0, :, sl]
        kk = k_ref[0, 0:nk, sl]
        vv = v_ref[0, 0:nk, sl]
        q1 = jnp.where(low, qc, jnp.zeros_like(qc))
        q2 = jnp.where(low, jnp.zeros_like(qc), qc)
        e1, l1 = _softmax_parts(_dot_nt(q1, kk))
        e2, l2 = _softmax_parts(_dot_nt(q2, kk))
        p = e1 * (1.0 / l1) - e2 * (lam / l2)
        o = _dot(p.astype(BF16), vv)
        ms = jnp.mean(o * o, axis=-1, keepdims=True)
        o = o * lax.rsqrt(ms + RMS_EPS) * g_ref[...] * out_scale
        o_ref[0, :, sl] = o.astype(BF16)


def _attn_c_kernel(lam_ref, q_ref, k_ref, v_ref, g_ref, o_ref, *, ntok, first_tile, out_scale):
    j = pl.program_id(1) + first_tile
    if first_tile == 0:
        @pl.when(j == 0)
        def _():
            _diff_tile(lam_ref, q_ref, k_ref, v_ref, g_ref, o_ref, CTX_LEN, out_scale)

        @pl.when(j != 0)
        def _():
            _diff_tile(lam_ref, q_ref, k_ref, v_ref, g_ref, o_ref, ntok, out_scale)
    else:
        _diff_tile(lam_ref, q_ref, k_ref, v_ref, g_ref, o_ref, ntok, out_scale)


def _attn_specs(nb, ntok, first_tile, wq, wkv):
    tpb = ntok // ROW_TILE
    grid = (nb, tpb - first_tile)
    qspec = pl.BlockSpec((1, ROW_TILE, wq), lambda b, j: (b, j + first_tile, 0))
    kvspec = pl.BlockSpec((1, ntok, wkv), lambda b, j: (b, 0, 0))
    ospec = pl.BlockSpec((1, ROW_TILE, wq), lambda b, j: (b, j, 0))
    out_rows = ntok - first_tile * ROW_TILE
    return grid, qspec, kvspec, ospec, out_rows


def _attn_a_call(qa, ka, va, nb, ntok, first_tile):
    grid, qspec, kvspec, ospec, out_rows = _attn_specs(nb, ntok, first_tile, A_Q_W, 2 * A_KV_W)
    return pl.pallas_call(
        functools.partial(_attn_a_kernel, ntok=ntok, first_tile=first_tile),
        out_shape=jax.ShapeDtypeStruct((nb, out_rows, A_Q_W), BF16),
        grid=grid,
        in_specs=[qspec, kvspec, kvspec],
        out_specs=ospec,
        compiler_params=_cparams(("parallel", "arbitrary")),
        name="gqa_attention",
    )(qa, ka, va)


def _attn_c_call(lam, qc, kc, vc, subln, nb, ntok, first_tile, out_scale):
    grid, qspec, kvspec, ospec, out_rows = _attn_specs(nb, ntok, first_tile, C_QK_W, C_QK_W)
    return pl.pallas_call(
        functools.partial(_attn_c_kernel, ntok=ntok, first_tile=first_tile, out_scale=out_scale),
        out_shape=jax.ShapeDtypeStruct((nb, out_rows, C_V_W), BF16),
        grid=grid,
        in_specs=[pl.BlockSpec(memory_space=pltpu.SMEM), qspec, kvspec, kvspec,
                  pl.BlockSpec((1, C_V_DIM), lambda b, j: (0, 0))],
        out_specs=ospec,
        compiler_params=_cparams(("parallel", "arbitrary")),
        name="diff_attention",
    )(lam, qc, kc, vc, subln)


def _fourier_kernel(f_ref, chan_ref, pos_ref, posc_ref, o_ref, g_ref, *, ntok, first_tile):
    j = pl.program_id(1) + first_tile
    nlat = ntok - CTX_LEN

    def channel_stage(rows):
        g = _dot(rows, chan_ref[...])
        return jnp.concatenate([g[:, :B_W], g[:, B_W:]], axis=0).astype(BF16)

    if first_tile == 0:
        @pl.when(j == 0)
        def _():
            gc = channel_stage(f_ref[0, 0:CTX_LEN, :])
            y = _dot(posc_ref[...], gc) * (1.0 / math.sqrt(CTX_LEN * F_GROUP_W))
            o_ref[0] = y.astype(BF16)

    @pl.when(j == 1)
    def _():
        g_ref[...] = channel_stage(f_ref[0, CTX_LEN:ntok, :])

    @pl.when(j >= 1)
    def _():
        y = _dot(pos_ref[...], g_ref[...]) * (1.0 / math.sqrt(nlat * F_GROUP_W))
        o_ref[0] = y.astype(BF16)


def _fourier_call(fb, chan, pos, posc, nb, ntok, first_tile):
    tpb = ntok // ROW_TILE
    nlat = ntok - CTX_LEN
    return pl.pallas_call(
        functools.partial(_fourier_kernel, ntok=ntok, first_tile=first_tile),
        out_shape=jax.ShapeDtypeStruct((nb, ntok - first_tile * ROW_TILE, B_W), BF16),
        grid=(nb, tpb - first_tile),
        in_specs=[
            pl.BlockSpec((1, ntok, B_W), lambda b, j: (b, 0, 0)),
            pl.BlockSpec(chan.shape, lambda b, j: (0, 0)),
            pl.BlockSpec((ROW_TILE, 2 * nlat),
                         lambda b, j: (jnp.maximum(j + first_tile - 1, 0), 0)),
            pl.BlockSpec(posc.shape, lambda b, j: (0, 0)),
        ],
        out_specs=pl.BlockSpec((1, ROW_TILE, B_W), lambda b, j: (b, j, 0)),
        scratch_shapes=[pltpu.VMEM((2 * nlat, B_W), BF16)],
        compiler_params=_cparams(("parallel", "arbitrary")),
        name="fourier_mix",
    )(fb, chan, pos, posc)


def _dft_tables(n):
    n1 = 32
    n0 = n // n1
    k = np.arange(n, dtype=np.int64)
    a = 2.0 * np.pi * ((k[:, None] * np.arange(n1)[None, :] * n0) % n) / n
    b = 2.0 * np.pi * ((k[:, None] * np.arange(n0)[None, :]) % n) / n
    ca, sa = jnp.asarray(np.cos(a), F32)[:, :, None], jnp.asarray(np.sin(a), F32)[:, :, None]
    cb, sb = jnp.asarray(np.cos(b), F32)[:, None, :], jnp.asarray(np.sin(b), F32)[:, None, :]
    cos = (ca * cb - sa * sb).reshape(n, n)
    sin = (sa * cb + ca * sb).reshape(n, n)
    return jnp.concatenate([cos, -sin], axis=1).astype(BF16)


def _channel_table():
    c = np.arange(F_GROUP_W)
    ang = 2.0 * np.pi * ((c[:, None] * c[None, :]) % F_GROUP_W) / F_GROUP_W
    eye = np.eye(F_GROUPS)
    cos = np.kron(eye, np.cos(ang))
    sin = np.kron(eye, np.sin(ang))
    return jnp.asarray(np.concatenate([cos, sin], axis=1), F32).astype(BF16)


def _pack_bf16_pairs(u):
    words = []
    for jb in range(u.shape[1] // (2 * LANES)):
        lo = u[:, 2 * jb * LANES:(2 * jb + 1) * LANES].astype(BF16).astype(F32)
        hi = u[:, (2 * jb + 1) * LANES:(2 * jb + 2) * LANES].astype(BF16).astype(F32)
        lo_b = pltpu.bitcast(lo, U32) >> 16
        hi_b = pltpu.bitcast(hi, U32) & jnp.uint32(0xFFFF0000)
        words.append(lo_b | hi_b)
    return jnp.concatenate(words, axis=1)


def _mixout_kernel(h_ref, mod_ref, oa_ref, ob_ref, oc_ref,
                   wg_ref, wa_ref, wb_ref, wc_ref, wo_ref, g1_ref, b1_ref, wr_ref,
                   h1_o, xp_o, lg_o, *, alpha):
    h = h_ref[...]
    m = mod_ref[0]
    u = (_layer_norm_rows(h) * (1.0 + m[1:2, :]) + m[0:1, :]).astype(BF16)
    y = None
    for n, (o_ref, w_ref) in enumerate(((oa_ref, wa_ref), (ob_ref, wb_ref), (oc_ref, wc_ref))):
        gate = jax.nn.sigmoid(_dot(u, wg_ref[:, n * D_MODEL:(n + 1) * D_MODEL]))
        t = gate * _dot(o_ref[...], w_ref[...])
        y = t if y is None else y + t
    z = _dot(y.astype(BF16), wo_ref[...])
    h1 = _layer_norm_rows(alpha * h + m[2:3, :] * z) * g1_ref[...] + b1_ref[...]
    h1_o[...] = h1
    u2 = _layer_norm_rows(h1) * (1.0 + m[4:5, :]) + m[3:4, :]
    xp_o[...] = _pack_bf16_pairs(u2)
    lg_o[...] = _dot(u2.astype(BF16), wr_ref[...])


def _tile_maps(nb, ntok, lat_only):
    tpb = ntok // ROW_TILE
    if lat_only:
        lpb = tpb - 1
        n_tiles = nb * lpb
        src = lambda i: (i // lpb) * tpb + 1 + i % lpb
        modi = lambda i: i // lpb
    else:
        n_tiles = nb * tpb
        src = lambda i: i
        modi = lambda i: jnp.where(i % tpb == 0, nb, i // tpb)
    return n_tiles, src, modi


def _mixout_call(h_all, mod, oa, ob, oc, wts, nb, ntok, lat_only, alpha):
    n_tiles, src, modi = _tile_maps(nb, ntok, lat_only)
    rows_out = n_tiles * ROW_TILE

    def full(a):
        return pl.BlockSpec(a.shape, lambda i: (0,) * a.ndim)

    def inrow(w):
        return pl.BlockSpec((ROW_TILE, w), lambda i: (src(i), 0))

    def outrow(w):
        return pl.BlockSpec((ROW_TILE, w), lambda i: (i, 0))

    modspec = pl.BlockSpec((1, 8, D_MODEL), lambda i: (modi(i), 0, 0))
    return pl.pallas_call(
        functools.partial(_mixout_kernel, alpha=alpha),
        out_shape=[jax.ShapeDtypeStruct((rows_out, D_MODEL), F32),
                   jax.ShapeDtypeStruct((rows_out, D_MODEL // 2), U32),
                   jax.ShapeDtypeStruct((rows_out, N_EXPERTS), F32)],
        grid=(n_tiles,),
        in_specs=[inrow(D_MODEL), modspec, outrow(A_Q_W), outrow(B_W), outrow(C_V_W)]
                 + [full(w) for w in wts],
        out_specs=[outrow(D_MODEL), outrow(D_MODEL // 2), outrow(N_EXPERTS)],
        compiler_params=_cparams(("parallel",)),
        name="mixer_output",
    )(h_all, mod, oa, ob, oc, *wts)


def _moe_kernel(blk_e_ref, nused_ref,
                tok_ref, dst_ref, w_ref, xp_ref, wg_ref, wu_ref, wd_ref,
                out_hbm,
                gath, wgu_s, wd_s, ybuf, zbuf, sem, zsem, *, t_pad, n_tok):
    i = pl.program_id(0)
    nsteps = pl.num_programs(0)
    nused = nused_ref[0]
    slot = lax.rem(i, 2)

    def row_copy(s, m, dst_row):
        return pltpu.make_async_copy(ybuf.at[s, pl.ds(SLAB * m, SLAB), :],
                                     out_hbm.at[pl.ds(pl.multiple_of(dst_row * SLAB, SLAB), SLAB), :],
                                     sem.at[s])

    def wait_slot(s):
        for m in range(MOE_BLOCK):
            row_copy(s, m, 0).wait()

    @pl.when(i == 0)
    def _():
        zbuf[...] = jnp.zeros_like(zbuf)
        for kk in range(TOP_K):
            cp = pltpu.make_async_copy(
                zbuf, out_hbm.at[pl.ds((kk * t_pad + n_tok) * SLAB, PAD_ROWS * SLAB), :], zsem)
            cp.start()
            cp.wait()

    @pl.when(i < nused)
    def _():
        prev = blk_e_ref[jnp.maximum(i - 1, 0)]

        @pl.when(jnp.logical_or(i == 0, blk_e_ref[i] != prev))
        def _():
            wgu_s[:, 0:EXPERT_FF] = wg_ref[0].astype(BF16)
            wgu_s[:, EXPERT_FF:2 * EXPERT_FF] = wu_ref[0].astype(BF16)
            wd_s[...] = wd_ref[0].astype(BF16)

        @pl.when(i >= 2)
        def _():
            wait_slot(slot)

        for m in range(MOE_BLOCK):
            t4 = pl.multiple_of(tok_ref[0, 0, m] * 4, 4)
            gath[4 * m:4 * m + 4, :] = xp_ref[pl.ds(t4, 4), :]

        acc = None
        for jb in range(4):
            words = gath[pl.ds(jb, MOE_BLOCK, stride=4), :]
            lo = pltpu.bitcast(words << 16, F32)
            hi = pltpu.bitcast(words & jnp.uint32(0xFFFF0000), F32)
            a = jnp.concatenate([lo, hi], axis=1).astype(BF16)
            t = _dot(a, wgu_s[jb * 2 * LANES:(jb + 1) * 2 * LANES, :])
            acc = t if acc is None else acc + t
        g = acc[:, 0:EXPERT_FF]
        act = (g * jax.nn.sigmoid(g)) * acc[:, EXPERT_FF:2 * EXPERT_FF]
        y = _dot(act.astype(BF16), wd_s[...])

        eye = (lax.broadcasted_iota(jnp.int32, (MOE_BLOCK, MOE_BLOCK), 0)
               == lax.broadcasted_iota(jnp.int32, (MOE_BLOCK, MOE_BLOCK), 1))
        wcol = jnp.sum(jnp.where(eye, jnp.broadcast_to(w_ref[0], (MOE_BLOCK, MOE_BLOCK)), 0.0),
                       axis=1, keepdims=True)
        y = y * wcol
        for jb in range(SLAB):
            ybuf[slot, pl.ds(jb, MOE_BLOCK, stride=SLAB), :] = y[:, jb * LANES:(jb + 1) * LANES]
        for m in range(MOE_BLOCK):
            row_copy(slot, m, dst_ref[0, 0, m]).start()

    @pl.when(i == nsteps - 1)
    def _():
        @pl.when(nused >= 1)
        def _():
            wait_slot(lax.rem(nused - 1, 2))

        @pl.when(nused >= 2)
        def _():
            wait_slot(lax.rem(nused - 2, 2))


def _moe_call(blk_e, nused, slot_tok, slot_dst, slot_w, xp4, wg, wu, wd, n_tok):
    nblk = blk_e.shape[0]
    t_pad = n_tok + PAD_ROWS
    blkspec = lambda: pl.BlockSpec((1, 1, MOE_BLOCK), lambda i, be, nu: (i, 0, 0),
                                   memory_space=pltpu.SMEM)
    grid_spec = pltpu.PrefetchScalarGridSpec(
        num_scalar_prefetch=2,
        grid=(nblk,),
        in_specs=[
            blkspec(), blkspec(),
            pl.BlockSpec((1, 1, MOE_BLOCK), lambda i, be, nu: (i, 0, 0)),
            pl.BlockSpec(memory_space=pltpu.VMEM),
            pl.BlockSpec((1, D_MODEL, EXPERT_FF), lambda i, be, nu: (be[i], 0, 0)),
            pl.BlockSpec((1, D_MODEL, EXPERT_FF), lambda i, be, nu: (be[i], 0, 0)),
            pl.BlockSpec((1, EXPERT_FF, D_MODEL), lambda i, be, nu: (be[i], 0, 0)),
        ],
        out_specs=pl.BlockSpec(memory_space=pl.ANY),
        scratch_shapes=[
            pltpu.VMEM((4 * MOE_BLOCK, LANES), U32),
            pltpu.VMEM((D_MODEL, 2 * EXPERT_FF), BF16),
            pltpu.VMEM((EXPERT_FF, D_MODEL), BF16),
            pltpu.VMEM((2, MOE_BLOCK * SLAB, LANES), F32),
            pltpu.VMEM((PAD_ROWS * SLAB, LANES), F32),
            pltpu.SemaphoreType.DMA((2,)),
            pltpu.SemaphoreType.DMA(()),
        ],
    )
    return pl.pallas_call(
        functools.partial(_moe_kernel, t_pad=t_pad, n_tok=n_tok),
        out_shape=jax.ShapeDtypeStruct((TOP_K * t_pad * SLAB, LANES), F32),
        grid_spec=grid_spec,
        compiler_params=_cparams(("arbitrary",)),
        name="routed_experts",
    )(blk_e, nused, slot_tok.reshape(nblk, 1, MOE_BLOCK), slot_dst.reshape(nblk, 1, MOE_BLOCK),
      slot_w.reshape(nblk, 1, MOE_BLOCK), xp4, wg, wu, wd)


def _combine_kernel(h1_ref, mod_ref, y8_ref, sg_ref, su_ref, sd_ref, g2_ref, b2_ref, o_ref, *, alpha):
    h1 = h1_ref[...]
    m = mod_ref[0]
    u2 = (_layer_norm_rows(h1) * (1.0 + m[4:5, :]) + m[3:4, :]).astype(BF16)
    g = _dot(u2, sg_ref[...])
    act = (g * jax.nn.sigmoid(g)) * _dot(u2, su_ref[...])
    f = _dot(act.astype(BF16), sd_ref[...])
    tm = h1.shape[0]
    cols = []
    for jb in range(SLAB):
        r = y8_ref[0, pl.ds(jb, tm, stride=SLAB), :]
        for kk in range(1, TOP_K):
            r = r + y8_ref[kk, pl.ds(jb, tm, stride=SLAB), :]
        cols.append(r)
    f = f + jnp.concatenate(cols, axis=1)
    o_ref[...] = _layer_norm_rows(alpha * h1 + m[5:6, :] * f) * g2_ref[...] + b2_ref[...]


def _combine_call(h1, mod, y8, wts, rows_per_group, group_is_ctx_first, nb, alpha):
    rows = h1.shape[0]
    tm = COMBINE_TILE
    tpg = rows_per_group // tm
    if group_is_ctx_first:
        first = CTX_LEN // tm
        modi = lambda i: jnp.where(i % tpg < first, nb, i // tpg)
    else:
        modi = lambda i: i // tpg

    def full(a):
        return pl.BlockSpec(a.shape, lambda i: (0,) * a.ndim)

    return pl.pallas_call(
        functools.partial(_combine_kernel, alpha=alpha),
        out_shape=jax.ShapeDtypeStruct((rows, D_MODEL), F32),
        grid=(rows // tm,),
        in_specs=[pl.BlockSpec((tm, D_MODEL), lambda i: (i, 0)),
                  pl.BlockSpec((1, 8, D_MODEL), lambda i: (modi(i), 0, 0)),
                  pl.BlockSpec((TOP_K, tm * SLAB, LANES), lambda i: (0, i, 0))]
                 + [full(w) for w in wts],
        out_specs=pl.BlockSpec((tm, D_MODEL), lambda i: (i, 0)),
        compiler_params=_cparams(("parallel",)),
        name="moe_combine",
    )(h1, mod, y8, *wts)


def _route(logits, b_r):
    t = logits.shape[0]
    scores = jax.nn.sigmoid(logits)
    biased = scores + b_r.astype(F32)
    grp = biased.reshape(t, N_GROUPS, N_EXPERTS // N_GROUPS)
    grp_score = jnp.sum(lax.top_k(grp, 2)[0], -1)
    _, gidx = lax.top_k(grp_score, TOPK_GROUPS)
    gmask = jnp.sum(jax.nn.one_hot(gidx, N_GROUPS, dtype=F32), 1) > 0
    emask = jnp.repeat(gmask, N_EXPERTS // N_GROUPS, axis=1)
    _, idx = lax.top_k(jnp.where(emask, biased, -jnp.inf), TOP_K)
    w = jnp.take_along_axis(scores, idx, -1)
    w = w / jnp.sum(w, -1, keepdims=True) * ROUTED_SCALE
    return idx, w


def _dispatch(idx, w, n_tok):
    t_pad = n_tok + PAD_ROWS
    n_assign = n_tok * TOP_K
    n_slots = (n_assign + N_EXPERTS * (MOE_BLOCK - 1) + MOE_BLOCK - 1) // MOE_BLOCK * MOE_BLOCK
    nblk = n_slots // MOE_BLOCK
    flat_e = idx.reshape(-1).astype(jnp.int32)
    flat_t = jnp.repeat(jnp.arange(n_tok, dtype=jnp.int32), TOP_K)
    flat_k = jnp.tile(jnp.arange(TOP_K, dtype=jnp.int32), n_tok)
    order = jnp.argsort(flat_e)
    e_sorted = flat_e[order]
    counts = jnp.bincount(flat_e, length=N_EXPERTS).astype(jnp.int32)
    padded = (counts + MOE_BLOCK - 1) // MOE_BLOCK * MOE_BLOCK
    pad_end = jnp.cumsum(padded)
    pad_start = pad_end - padded
    start = jnp.cumsum(counts) - counts
    dest = pad_start[e_sorted] + jnp.arange(n_assign, dtype=jnp.int32) - start[e_sorted]
    slot_tok = jnp.zeros((n_slots,), jnp.int32).at[dest].set(flat_t[order])
    slot_w = jnp.zeros((n_slots,), F32).at[dest].set(w.reshape(-1)[order])
    pad_id = jnp.arange(n_slots, dtype=jnp.int32) % (2 * MOE_BLOCK)
    pad_dst = (pad_id % TOP_K) * t_pad + n_tok + pad_id // TOP_K
    slot_dst = pad_dst.at[dest].set(flat_k[order] * t_pad + flat_t[order])
    blk_start = jnp.arange(nblk, dtype=jnp.int32) * MOE_BLOCK
    blk_e = jnp.minimum(jnp.searchsorted(pad_end, blk_start, side='right'),
                        N_EXPERTS - 1).astype(jnp.int32)
    nused = (pad_end[-1] // MOE_BLOCK).astype(jnp.int32).reshape(1)
    return blk_e, nused, slot_tok, slot_dst, slot_w


def _rope_tables(s, ntok):
    rows_n = s // GRID_W
    row = jnp.repeat(jnp.arange(rows_n, dtype=F32), GRID_W)
    col = jnp.tile(jnp.arange(GRID_W, dtype=F32), rows_n)
    axis_dim = HEAD_DIM // 2
    inv = jnp.power(ROPE_THETA, -jnp.arange(0, axis_dim, 2, dtype=F32) / axis_dim)
    ar = row[:, None] * inv[None]
    ac = col[:, None] * inv[None]
    ang = jnp.concatenate([ar, ar, ac, ac], -1)
    cos, sin = jnp.cos(ang), jnp.sin(ang)
    quarter = (jnp.arange(HEAD_DIM) // 16) % 2
    s_up = jnp.where(quarter == 0, -sin, 0.0)
    s_dn = jnp.where(quarter == 1, sin, 0.0)
    nctx = ntok - s

    def expand(t, ctx_val):
        t = jnp.concatenate([jnp.full((nctx, HEAD_DIM), ctx_val, F32), t], axis=0)
        return jnp.tile(t, (1, LANES // HEAD_DIM))

    return expand(cos, 1.0), expand(s_up, 0.0), expand(s_dn, 0.0)


def _head_mean_matrix(width):
    hid = np.arange(width) // HEAD_DIM
    return jnp.asarray((hid[:, None] == hid[None, :]).astype(np.float32) / HEAD_DIM).astype(BF16)


def _dup_heads(a, n_heads):
    parts = []
    for hd in range(n_heads):
        p = a[..., hd * HEAD_DIM:(hd + 1) * HEAD_DIM]
        parts += [p, p]
    return jnp.concatenate(parts, axis=-1)


def kernel(x, c, ctx, c_ctx, w_mod, b_mod, w_in, qn_a, kn_a, lam_q1, lam_k1, lam_q2, lam_k2, subln_c, w_br_a, w_br_b, w_br_c, w_out, ln1_g, ln1_b, w_router, b_router, w_sh_gate, w_sh_up, w_sh_down, w_e_gate, w_e_up, w_e_down, ln2_g, ln2_b):
    nb, s, d = x.shape
    lc = ctx.shape[1]
    depth = w_mod.shape[0]
    assert d == D_MODEL and lc == CTX_LEN and s % ROW_TILE == 0 and s % GRID_W == 0
    ntok = lc + s
    alpha = (2 * depth) ** 0.25

    tabs = _rope_tables(s, ntok)
    pos_lat = _dft_tables(s)
    pos_ctx = _dft_tables(lc)
    chan = _channel_table()
    e_q = _head_mean_matrix(A_Q_W)
    e_k = _head_mean_matrix(2 * A_KV_W)

    cc = jnp.concatenate([c, c_ctx[None, :]], axis=0)
    cc = jnp.pad(cc, ((0, (-(nb + 1)) % 8), (0, 0)))
    h_all = jnp.concatenate([ctx, x], axis=1).reshape(nb * ntok, d)

    offs = np.cumsum([0, A_Q_W, A_KV_W, A_KV_W, B_W, C_QK_W, C_QK_W, C_V_W, GATE_W])
    out = None
    for l in range(depth):
        last = l == depth - 1
        mod = _mod_call(cc, w_mod[l], b_mod[l])[:nb + 1].reshape(nb + 1, 6, d)
        mod = jnp.pad(mod, ((0, 0), (0, 2), (0, 0)))
        lam_init = 0.8 - 0.6 * math.exp(-0.3 * l)
        lam = (jnp.exp(jnp.sum(lam_q1[l].astype(F32) * lam_k1[l].astype(F32)))
               - jnp.exp(jnp.sum(lam_q2[l].astype(F32) * lam_k2[l].astype(F32)))) + lam_init
        lam = lam.reshape(1, 1).astype(F32)

        wl = w_in[l]
        seg = [wl[:, offs[i]:offs[i + 1]] for i in range(8)]
        in_wts = (seg[0].astype(BF16), _dup_heads(seg[1], A_KV_HEADS).astype(BF16),
                  _dup_heads(seg[2], A_KV_HEADS).astype(BF16), seg[3].astype(BF16),
                  seg[4].astype(BF16), seg[5].astype(BF16), seg[6].astype(BF16),
                  jnp.tile(qn_a[l].astype(F32), A_Q_HEADS).reshape(1, A_Q_W),
                  jnp.tile(kn_a[l].astype(F32), 2 * A_KV_HEADS).reshape(1, 2 * A_KV_W),
                  e_q, e_k)
        qa, ka, va, fb, qc, kc, vc = _inproj_call(h_all, mod, tabs, in_wts, nb, ntok)

        first_tile = 1 if last else 0
        r3 = lambda a: a.reshape(nb, ntok, a.shape[-1])
        oa = _attn_a_call(r3(qa), r3(ka), r3(va), nb, ntok, first_tile)
        oc = _attn_c_call(lam, r3(qc), r3(kc), r3(vc), subln_c[l].astype(F32).reshape(1, C_V_DIM),
                          nb, ntok, first_tile, 1.0 - lam_init)
        ob = _fourier_call(r3(fb), chan, pos_lat, pos_ctx, nb, ntok, first_tile)

        mix_wts = (seg[7].astype(BF16), w_br_a[l].astype(BF16), w_br_b[l].astype(BF16),
                   w_br_c[l].astype(BF16), w_out[l].astype(BF16),
                   ln1_g[l].astype(F32).reshape(1, d), ln1_b[l].astype(F32).reshape(1, d),
                   w_router[l].astype(BF16))
        flat = lambda a: a.reshape(-1, a.shape[-1])
        h1, xp, logits = _mixout_call(h_all, mod, flat(oa), flat(ob), flat(oc), mix_wts,
                                      nb, ntok, last, alpha)

        n_tok = h1.shape[0]
        idx, w = _route(logits, b_router[l])
        blk_e, nused, slot_tok, slot_dst, slot_w = _dispatch(idx, w, n_tok)
        y8 = _moe_call(blk_e, nused, slot_tok, slot_dst, slot_w,
                       xp.reshape(4 * n_tok, LANES), w_e_gate[l], w_e_up[l], w_e_down[l], n_tok)
        y8 = y8.reshape(TOP_K, (n_tok + PAD_ROWS) * SLAB, LANES)

        comb_wts = (w_sh_gate[l].astype(BF16), w_sh_up[l].astype(BF16), w_sh_down[l].astype(BF16),
                    ln2_g[l].astype(F32).reshape(1, d), ln2_b[l].astype(F32).reshape(1, d))
        h2 = _combine_call(h1, mod, y8, comb_wts, s if last else ntok, not last, nb, alpha)
        if last:
            out = h2.reshape(nb, s, d)
        else:
            h_all = h2
    return out
```

```python
import functools
import math

import numpy as np
import jax
import jax.numpy as jnp
from jax import lax
from jax.experimental import pallas as pl
from jax.experimental.pallas import tpu as pltpu

F32 = jnp.float32
BF16 = jnp.bfloat16
U32 = jnp.uint32

D_MODEL = 1024
CTX_LEN = 256
GRID_W = 64
HEAD_DIM = 64
ROPE_THETA = 10000.0
A_Q_HEADS = 8
A_KV_HEADS = 2
A_Q_W = A_Q_HEADS * HEAD_DIM
A_KV_W = A_KV_HEADS * HEAD_DIM
F_GROUPS = 4
F_GROUP_W = 128
B_W = F_GROUPS * F_GROUP_W
C_HEADS = 4
C_V_DIM = 2 * HEAD_DIM
C_QK_W = C_HEADS * 2 * HEAD_DIM
C_V_W = C_HEADS * C_V_DIM
N_BRANCH = 3
GATE_W = N_BRANCH * D_MODEL
N_EXPERTS = 256
TOP_K = 8
N_GROUPS = 8
TOPK_GROUPS = 4
EXPERT_FF = 256
SHARED_FF = 256
ROUTED_SCALE = 2.5
LN_EPS = 1e-5
RMS_EPS = 1e-6
K_SHIFT = TOP_K.bit_length() - 1
assert 1 << K_SHIFT == TOP_K

LANES = 128
SLAB = D_MODEL // LANES
ROW_TILE = 256
MOE_BLOCK = 128
PAD_ROWS = 2 * MOE_BLOCK // TOP_K
COMBINE_TILE = 128
VMEM_LIMIT = 56 * 1024 * 1024

_SQRT_HALF_SCALE = HEAD_DIM ** -0.5


def _cparams(sem):
    return pltpu.CompilerParams(dimension_semantics=sem, vmem_limit_bytes=VMEM_LIMIT)


def _dot(a, b):
    return jnp.dot(a, b, preferred_element_type=F32)


def _dot_nt(a, b):
    return lax.dot_general(a, b, (((1,), (1,)), ((), ())), preferred_element_type=F32)


def _layer_norm_rows(x):
    mu = jnp.mean(x, axis=-1, keepdims=True)
    xc = x - mu
    var = jnp.mean(xc * xc, axis=-1, keepdims=True)
    return xc * lax.rsqrt(var + LN_EPS)


def _dot_split(x, e):
    hi = x.astype(BF16)
    lo = (x - hi.astype(F32)).astype(BF16)
    return _dot(hi, e) + _dot(lo, e)


def _mod_kernel(c_ref, w_ref, b_ref, o_ref):
    c = c_ref[...]
    sc = c * jax.nn.sigmoid(c)
    o_ref[...] = _dot(sc.astype(BF16), w_ref[...].astype(BF16)) + b_ref[...]


def _mod_call(cc, w_mod_l, b_mod_l):
    r = cc.shape[0]
    n = w_mod_l.shape[1]
    tn = D_MODEL
    return pl.pallas_call(
        _mod_kernel,
        out_shape=jax.ShapeDtypeStruct((r, n), F32),
        grid=(n // tn,),
        in_specs=[
            pl.BlockSpec((r, D_MODEL), lambda j: (0, 0)),
            pl.BlockSpec((D_MODEL, tn), lambda j: (0, j)),
            pl.BlockSpec((1, tn), lambda j: (0, j)),
        ],
        out_specs=pl.BlockSpec((r, tn), lambda j: (0, j)),
        compiler_params=_cparams(("arbitrary",)),
        name="mod_vectors",
    )(cc, w_mod_l, b_mod_l.reshape(1, n))


def _rope_cols(x, cos, sin_up, sin_dn):
    cols = []
    for c in range(x.shape[1] // LANES):
        xc = x[:, c * LANES:(c + 1) * LANES]
        up = pltpu.roll(xc, LANES - 16, axis=1)
        dn = pltpu.roll(xc, 16, axis=1)
        cols.append(xc * cos + up * sin_up + dn * sin_dn)
    return jnp.concatenate(cols, axis=1) if len(cols) > 1 else cols[0]


def _inproj_kernel(h_ref, mod_ref, cos_ref, su_ref, sd_ref,
                   wq_ref, wk_ref, wv_ref, wf_ref, wqc_ref, wkc_ref, wvc_ref,
                   qn_ref, kn_ref, eq_ref, ek_ref,
                   qa_o, ka_o, va_o, fb_o, qc_o, kc_o, vc_o):
    h = h_ref[...]
    shift = mod_ref[0, 0:1, :]
    scale = mod_ref[0, 1:2, :]
    u = (_layer_norm_rows(h) * (1.0 + scale) + shift).astype(BF16)
    cos = cos_ref[...]
    s_up = su_ref[...]
    s_dn = sd_ref[...]

    q = _dot(u, wq_ref[...])
    ms = _dot_split(q * q, eq_ref[...])
    q = q * lax.rsqrt(ms + RMS_EPS) * qn_ref[...]
    qa_o[...] = (_rope_cols(q, cos, s_up, s_dn) * _SQRT_HALF_SCALE).astype(BF16)

    k = _dot(u, wk_ref[...])
    ms = _dot_split(k * k, ek_ref[...])
    k = k * lax.rsqrt(ms + RMS_EPS) * kn_ref[...]
    ka_o[...] = _rope_cols(k, cos, s_up, s_dn).astype(BF16)

    va_o[...] = _dot(u, wv_ref[...]).astype(BF16)
    fb_o[...] = _dot(u, wf_ref[...]).astype(BF16)
    qc = _dot(u, wqc_ref[...])
    qc_o[...] = (_rope_cols(qc, cos, s_up, s_dn) * _SQRT_HALF_SCALE).astype(BF16)
    kc = _dot(u, wkc_ref[...])
    kc_o[...] = _rope_cols(kc, cos, s_up, s_dn).astype(BF16)
    vc_o[...] = _dot(u, wvc_ref[...]).astype(BF16)


def _inproj_call(h_all, mod, tabs, wts, nb, ntok):
    rows = h_all.shape[0]
    tpb = ntok // ROW_TILE
    cos, s_up, s_dn = tabs

    def full(a):
        return pl.BlockSpec(a.shape, lambda i: (0,) * a.ndim)

    def rowspec(w):
        return pl.BlockSpec((ROW_TILE, w), lambda i: (i, 0))

    tabspec = pl.BlockSpec((ROW_TILE, LANES), lambda i: (i % tpb, 0))
    modspec = pl.BlockSpec((1, 8, D_MODEL),
                           lambda i: (jnp.where(i % tpb == 0, nb, i // tpb), 0, 0))
    widths = (A_Q_W, 2 * A_KV_W, 2 * A_KV_W, B_W, C_QK_W, C_QK_W, C_V_W)
    return pl.pallas_call(
        _inproj_kernel,
        out_shape=[jax.ShapeDtypeStruct((rows, w), BF16) for w in widths],
        grid=(rows // ROW_TILE,),
        in_specs=[rowspec(D_MODEL), modspec, tabspec, tabspec, tabspec]
                 + [full(w) for w in wts],
        out_specs=[rowspec(w) for w in widths],
        compiler_params=_cparams(("parallel",)),
        name="in_projection",
    )(h_all, mod, cos, s_up, s_dn, *wts)


def _softmax_parts(s):
    m = jnp.max(s, axis=-1, keepdims=True)
    e = jnp.exp(s - m)
    return e, jnp.sum(e, axis=-1, keepdims=True)


def _gqa_tile(q_ref, k_ref, v_ref, o_ref, nk):
    lane = lax.broadcasted_iota(jnp.int32, (1, LANES), 1)
    low = lane < HEAD_DIM
    for c in range(A_Q_W // LANES):
        kvh = (2 * c) // (A_Q_HEADS // A_KV_HEADS)
        qc = q_ref[0, :, c * LANES:(c + 1) * LANES]
        kk = k_ref[0, 0:nk, kvh * LANES:(kvh + 1) * LANES]
        vv = v_ref[0, 0:nk, kvh * LANES:(kvh + 1) * LANES]
        halves = []
        for keep in (low, jnp.logical_not(low)):
            qm = jnp.where(keep, qc, jnp.zeros_like(qc))
            e, l = _softmax_parts(_dot_nt(qm, kk))
            halves.append(_dot(e.astype(BF16), vv) * (1.0 / l))
        o_ref[0, :, c * LANES:(c + 1) * LANES] = jnp.where(low, halves[0], halves[1]).astype(BF16)


def _attn_a_kernel(q_ref, k_ref, v_ref, o_ref, *, ntok, first_tile):
    j = pl.program_id(1) + first_tile
    if first_tile == 0:
        @pl.when(j == 0)
        def _():
            _gqa_tile(q_ref, k_ref, v_ref, o_ref, CTX_LEN)

        @pl.when(j != 0)
        def _():
            _gqa_tile(q_ref, k_ref, v_ref, o_ref, ntok)
    else:
        _gqa_tile(q_ref, k_ref, v_ref, o_ref, ntok)


def _diff_tile(lam_ref, q_ref, k_ref, v_ref, g_ref, o_ref, nk, out_scale):
    lane = lax.broadcasted_iota(jnp.int32, (1, LANES), 1)
    low = lane < HEAD_DIM
    lam = lam_ref[0, 0]
    for hd in range(C_HEADS):
        sl = slice(hd * LANES, (hd + 1) * LANES)
        qc = q_ref[0, :, sl]
        kk = k_ref[0, 0:nk, sl]
        vv = v_ref[0, 0:nk, sl]
        q1 = jnp.where(low, qc, jnp.zeros_like(qc))
        q2 = jnp.where(low, jnp.zeros_like(qc), qc)
        e1, l1 = _softmax_parts(_dot_nt(q1, kk))
        e2, l2 = _softmax_parts(_dot_nt(q2, kk))
        p = e1 * (1.0 / l1) - e2 * (lam / l2)
        o = _dot(p.astype(BF16), vv)
        ms = jnp.mean(o * o, axis=-1, keepdims=True)
        o = o * lax.rsqrt(ms + RMS_EPS) * g_ref[...] * out_scale
        o_ref[0, :, sl] = o.astype(BF16)


def _attn_c_kernel(lam_ref, q_ref, k_ref, v_ref, g_ref, o_ref, *, ntok, first_tile, out_scale):
    j = pl.program_id(1) + first_tile
    if first_tile == 0:
        @pl.when(j == 0)
        def _():
            _diff_tile(lam_ref, q_ref, k_ref, v_ref, g_ref, o_ref, CTX_LEN, out_scale)

        @pl.when(j != 0)
        def _():
            _diff_tile(lam_ref, q_ref, k_ref, v_ref, g_ref, o_ref, ntok, out_scale)
    else:
        _diff_tile(lam_ref, q_ref, k_ref, v_ref, g_ref, o_ref, ntok, out_scale)


def _attn_specs(nb, ntok, first_tile, wq, wkv):
    tpb = ntok // ROW_TILE
    grid = (nb, tpb - first_tile)
    qspec = pl.BlockSpec((1, ROW_TILE, wq), lambda b, j: (b, j + first_tile, 0))
    kvspec = pl.BlockSpec((1, ntok, wkv), lambda b, j: (b, 0, 0))
    ospec = pl.BlockSpec((1, ROW_TILE, wq), lambda b, j: (b, j, 0))
    out_rows = ntok - first_tile * ROW_TILE
    return grid, qspec, kvspec, ospec, out_rows


def _attn_a_call(qa, ka, va, nb, ntok, first_tile):
    grid, qspec, kvspec, ospec, out_rows = _attn_specs(nb, ntok, first_tile, A_Q_W, 2 * A_KV_W)
    return pl.pallas_call(
        functools.partial(_attn_a_kernel, ntok=ntok, first_tile=first_tile),
        out_shape=jax.ShapeDtypeStruct((nb, out_rows, A_Q_W), BF16),
        grid=grid,
        in_specs=[qspec, kvspec, kvspec],
        out_specs=ospec,
        compiler_params=_cparams(("parallel", "arbitrary")),
        name="gqa_attention",
    )(qa, ka, va)


def _attn_c_call(lam, qc, kc, vc, subln, nb, ntok, first_tile, out_scale):
    grid, qspec, kvspec, ospec, out_rows = _attn_specs(nb, ntok, first_tile, C_QK_W, C_QK_W)
    return pl.pallas_call(
        functools.partial(_attn_c_kernel, ntok=ntok, first_tile=first_tile, out_scale=out_scale),
        out_shape=jax.ShapeDtypeStruct((nb, out_rows, C_V_W), BF16),
        grid=grid,
        in_specs=[pl.BlockSpec(memory_space=pltpu.SMEM), qspec, kvspec, kvspec,
                  pl.BlockSpec((1, C_V_DIM), lambda b, j: (0, 0))],
        out_specs=ospec,
        compiler_params=_cparams(("parallel", "arbitrary")),
        name="diff_attention",
    )(lam, qc, kc, vc, subln)


def _fourier_kernel(f_ref, chan_ref, pos_ref, posc_ref, o_ref, g_ref, *, ntok, first_tile):
    j = pl.program_id(1) + first_tile
    nlat = ntok - CTX_LEN

    def channel_stage(rows):
        g = _dot(rows, chan_ref[...])
        return jnp.concatenate([g[:, :B_W], g[:, B_W:]], axis=0).astype(BF16)

    if first_tile == 0:
        @pl.when(j == 0)
        def _():
            gc = channel_stage(f_ref[0, 0:CTX_LEN, :])
            y = _dot(posc_ref[...], gc) * (1.0 / math.sqrt(CTX_LEN * F_GROUP_W))
            o_ref[0] = y.astype(BF16)

    @pl.when(j == 1)
    def _():
        g_ref[...] = channel_stage(f_ref[0, CTX_LEN:ntok, :])

    @pl.when(j >= 1)
    def _():
        y = _dot(pos_ref[...], g_ref[...]) * (1.0 / math.sqrt(nlat * F_GROUP_W))
        o_ref[0] = y.astype(BF16)


def _fourier_call(fb, chan, pos, posc, nb, ntok, first_tile):
    tpb = ntok // ROW_TILE
    nlat = ntok - CTX_LEN
    return pl.pallas_call(
        functools.partial(_fourier_kernel, ntok=ntok, first_tile=first_tile),
        out_shape=jax.ShapeDtypeStruct((nb, ntok - first_tile * ROW_TILE, B_W), BF16),
        grid=(nb, tpb - first_tile),
        in_specs=[
            pl.BlockSpec((1, ntok, B_W), lambda b, j: (b, 0, 0)),
            pl.BlockSpec(chan.shape, lambda b, j: (0, 0)),
            pl.BlockSpec((ROW_TILE, 2 * nlat),
                         lambda b, j: (jnp.maximum(j + first_tile - 1, 0), 0)),
            pl.BlockSpec(posc.shape, lambda b, j: (0, 0)),
        ],
        out_specs=pl.BlockSpec((1, ROW_TILE, B_W), lambda b, j: (b, j, 0)),
        scratch_shapes=[pltpu.VMEM((2 * nlat, B_W), BF16)],
        compiler_params=_cparams(("parallel", "arbitrary")),
        name="fourier_mix",
    )(fb, chan, pos, posc)


def _dft_tables(n):
    n1 = 32
    n0 = n // n1
    k = np.arange(n, dtype=np.int64)
    a = 2.0 * np.pi * ((k[:, None] * np.arange(n1)[None, :] * n0) % n) / n
    b = 2.0 * np.pi * ((k[:, None] * np.arange(n0)[None, :]) % n) / n
    ca, sa = jnp.asarray(np.cos(a), F32)[:, :, None], jnp.asarray(np.sin(a), F32)[:, :, None]
    cb, sb = jnp.asarray(np.cos(b), F32)[:, None, :], jnp.asarray(np.sin(b), F32)[:, None, :]
    cos = (ca * cb - sa * sb).reshape(n, n)
    sin = (sa * cb + ca * sb).reshape(n, n)
    return jnp.concatenate([cos, -sin], axis=1).astype(BF16)


def _channel_table():
    c = np.arange(F_GROUP_W)
    ang = 2.0 * np.pi * ((c[:, None] * c[None, :]) % F_GROUP_W) / F_GROUP_W
    eye = np.eye(F_GROUPS)
    cos = np.kron(eye, np.cos(ang))
    sin = np.kron(eye, np.sin(ang))
    return jnp.asarray(np.concatenate([cos, sin], axis=1), F32).astype(BF16)


def _pack_bf16_pairs(u):
    words = []
    for jb in range(u.shape[1] // (2 * LANES)):
        lo = u[:, 2 * jb * LANES:(2 * jb + 1) * LANES].astype(BF16).astype(F32)
        hi = u[:, (2 * jb + 1) * LANES:(2 * jb + 2) * LANES].astype(BF16).astype(F32)
        lo_b = pltpu.bitcast(lo, U32) >> 16
        hi_b = pltpu.bitcast(hi, U32) & jnp.uint32(0xFFFF0000)
        words.append(lo_b | hi_b)
    return jnp.concatenate(words, axis=1)


def _mixout_kernel(h_ref, mod_ref, oa_ref, ob_ref, oc_ref,
                   wg_ref, wa_ref, wb_ref, wc_ref, wo_ref, g1_ref, b1_ref, wr_ref,
                   h1_o, xp_o, lg_o, *, alpha):
    h = h_ref[...]
    m = mod_ref[0]
    u = (_layer_norm_rows(h) * (1.0 + m[1:2, :]) + m[0:1, :]).astype(BF16)
    y = None
    for n, (o_ref, w_ref) in enumerate(((oa_ref, wa_ref), (ob_ref, wb_ref), (oc_ref, wc_ref))):
        gate = jax.nn.sigmoid(_dot(u, wg_ref[:, n * D_MODEL:(n + 1) * D_MODEL]))
        t = gate * _dot(o_ref[...], w_ref[...])
        y = t if y is None else y + t
    z = _dot(y.astype(BF16), wo_ref[...])
    h1 = _layer_norm_rows(alpha * h + m[2:3, :] * z) * g1_ref[...] + b1_ref[...]
    h1_o[...] = h1
    u2 = _layer_norm_rows(h1) * (1.0 + m[4:5, :]) + m[3:4, :]
    xp_o[...] = _pack_bf16_pairs(u2)
    lg_o[...] = _dot(u2.astype(BF16), wr_ref[...])


def _tile_maps(nb, ntok, lat_only):
    tpb = ntok // ROW_TILE
    if lat_only:
        lpb = tpb - 1
        n_tiles = nb * lpb
        src = lambda i: (i // lpb) * tpb + 1 + i % lpb
        modi = lambda i: i // lpb
    else:
        n_tiles = nb * tpb
        src = lambda i: i
        modi = lambda i: jnp.where(i % tpb == 0, nb, i // tpb)
    return n_tiles, src, modi


def _mixout_call(h_all, mod, oa, ob, oc, wts, nb, ntok, lat_only, alpha):
    n_tiles, src, modi = _tile_maps(nb, ntok, lat_only)
    rows_out = n_tiles * ROW_TILE

    def full(a):
        return pl.BlockSpec(a.shape, lambda i: (0,) * a.ndim)

    def inrow(w):
        return pl.BlockSpec((ROW_TILE, w), lambda i: (src(i), 0))

    def outrow(w):
        return pl.BlockSpec((ROW_TILE, w), lambda i: (i, 0))

    modspec = pl.BlockSpec((1, 8, D_MODEL), lambda i: (modi(i), 0, 0))
    return pl.pallas_call(
        functools.partial(_mixout_kernel, alpha=alpha),
        out_shape=[jax.ShapeDtypeStruct((rows_out, D_MODEL), F32),
                   jax.ShapeDtypeStruct((rows_out, D_MODEL // 2), U32),
                   jax.ShapeDtypeStruct((rows_out, N_EXPERTS), F32)],
        grid=(n_tiles,),
        in_specs=[inrow(D_MODEL), modspec, outrow(A_Q_W), outrow(B_W), outrow(C_V_W)]
                 + [full(w) for w in wts],
        out_specs=[outrow(D_MODEL), outrow(D_MODEL // 2), outrow(N_EXPERTS)],
        compiler_params=_cparams(("parallel",)),
        name="mixer_output",
    )(h_all, mod, oa, ob, oc, *wts)


def _moe_kernel(blk_e_ref, nused_ref,
                a_ref, xp_ref, wg_ref, wu_ref, wd_ref,
                out_hbm,
                gath, wgu_s, wd_s, ybuf, zbuf, sem, zsem, *, t_pad, n_tok):
    i = pl.program_id(0)
    nsteps = pl.num_programs(0)
    nused = nused_ref[0]
    slot = lax.rem(i, 2)

    def row_copy(s, m, dst_row):
        return pltpu.make_async_copy(ybuf.at[s, pl.ds(SLAB * m, SLAB), :],
                                     out_hbm.at[pl.ds(pl.multiple_of(dst_row * SLAB, SLAB), SLAB), :],
                                     sem.at[s])

    def wait_slot(s):
        for m in range(MOE_BLOCK):
            row_copy(s, m, 0).wait()

    @pl.when(i == 0)
    def _():
        zbuf[...] = jnp.zeros_like(zbuf)
        for kk in range(TOP_K):
            cp = pltpu.make_async_copy(
                zbuf, out_hbm.at[pl.ds((kk * t_pad + n_tok) * SLAB, PAD_ROWS * SLAB), :], zsem)
            cp.start()
            cp.wait()

    @pl.when(i < nused)
    def _():
        prev = blk_e_ref[jnp.maximum(i - 1, 0)]

        @pl.when(jnp.logical_or(i == 0, blk_e_ref[i] != prev))
        def _():
            wgu_s[:, 0:EXPERT_FF] = wg_ref[0].astype(BF16)
            wgu_s[:, EXPERT_FF:2 * EXPERT_FF] = wu_ref[0].astype(BF16)
            wd_s[...] = wd_ref[0].astype(BF16)

        @pl.when(i >= 2)
        def _():
            wait_slot(slot)

        for m in range(MOE_BLOCK):
            tok = jnp.minimum(lax.shift_right_logical(a_ref[0, 0, m], K_SHIFT), n_tok - 1)
            t4 = pl.multiple_of(tok * 4, 4)
            gath[4 * m:4 * m + 4, :] = xp_ref[pl.ds(t4, 4), :]

        acc = None
        for jb in range(4):
            words = gath[pl.ds(jb, MOE_BLOCK, stride=4), :]
            lo = pltpu.bitcast(words << 16, F32)
            hi = pltpu.bitcast(words & jnp.uint32(0xFFFF0000), F32)
            a = jnp.concatenate([lo, hi], axis=1).astype(BF16)
            t = _dot(a, wgu_s[jb * 2 * LANES:(jb + 1) * 2 * LANES, :])
            acc = t if acc is None else acc + t
        g = acc[:, 0:EXPERT_FF]
        act = (g * jax.nn.sigmoid(g)) * acc[:, EXPERT_FF:2 * EXPERT_FF]
        y = _dot(act.astype(BF16), wd_s[...])

        for jb in range(SLAB):
            ybuf[slot, pl.ds(jb, MOE_BLOCK, stride=SLAB), :] = y[:, jb * LANES:(jb + 1) * LANES]
        for m in range(MOE_BLOCK):
            a = a_ref[0, 0, m]
            row_copy(slot, m, (a & (TOP_K - 1)) * t_pad + lax.shift_right_logical(a, K_SHIFT)).start()

    @pl.when(i == nsteps - 1)
    def _():
        @pl.when(nused >= 1)
        def _():
            wait_slot(lax.rem(nused - 1, 2))

        @pl.when(nused >= 2)
        def _():
            wait_slot(lax.rem(nused - 2, 2))


def _moe_call(blk_e, nused, slot_a, xp4, wg, wu, wd, n_tok):
    nblk = blk_e.shape[0]
    t_pad = n_tok + PAD_ROWS
    grid_spec = pltpu.PrefetchScalarGridSpec(
        num_scalar_prefetch=2,
        grid=(nblk,),
        in_specs=[
            pl.BlockSpec((1, 1, MOE_BLOCK), lambda i, be, nu: (i, 0, 0), memory_space=pltpu.SMEM),
            pl.BlockSpec(memory_space=pltpu.VMEM),
            pl.BlockSpec((1, D_MODEL, EXPERT_FF), lambda i, be, nu: (be[i], 0, 0)),
            pl.BlockSpec((1, D_MODEL, EXPERT_FF), lambda i, be, nu: (be[i], 0, 0)),
            pl.BlockSpec((1, EXPERT_FF, D_MODEL), lambda i, be, nu: (be[i], 0, 0)),
        ],
        out_specs=pl.BlockSpec(memory_space=pl.ANY),
        scratch_shapes=[
            pltpu.VMEM((4 * MOE_BLOCK, LANES), U32),
            pltpu.VMEM((D_MODEL, 2 * EXPERT_FF), BF16),
            pltpu.VMEM((EXPERT_FF, D_MODEL), BF16),
            pltpu.VMEM((2, MOE_BLOCK * SLAB, LANES), F32),
            pltpu.VMEM((PAD_ROWS * SLAB, LANES), F32),
            pltpu.SemaphoreType.DMA((2,)),
            pltpu.SemaphoreType.DMA(()),
        ],
    )
    return pl.pallas_call(
        functools.partial(_moe_kernel, t_pad=t_pad, n_tok=n_tok),
        out_shape=jax.ShapeDtypeStruct((TOP_K * t_pad * SLAB, LANES), F32),
        grid_spec=grid_spec,
        compiler_params=_cparams(("arbitrary",)),
        name="routed_experts",
    )(blk_e, nused, slot_a.reshape(nblk, 1, MOE_BLOCK), xp4, wg, wu, wd)


def _combine_kernel(h1_ref, mod_ref, y8_ref, rw_ref, sg_ref, su_ref, sd_ref, g2_ref, b2_ref, o_ref,
                    *, alpha):
    h1 = h1_ref[...]
    m = mod_ref[0]
    u2 = (_layer_norm_rows(h1) * (1.0 + m[4:5, :]) + m[3:4, :]).astype(BF16)
    g = _dot(u2, sg_ref[...])
    act = (g * jax.nn.sigmoid(g)) * _dot(u2, su_ref[...])
    f = _dot(act.astype(BF16), sd_ref[...])
    tm = h1.shape[0]
    rw = rw_ref[...]
    wk = [jnp.broadcast_to(rw[:, kk:kk + 1], (tm, LANES)) for kk in range(TOP_K)]
    cols = []
    for jb in range(SLAB):
        r = wk[0] * y8_ref[0, pl.ds(jb, tm, stride=SLAB), :]
        for kk in range(1, TOP_K):
            r = r + wk[kk] * y8_ref[kk, pl.ds(jb, tm, stride=SLAB), :]
        cols.append(r)
    f = f + jnp.concatenate(cols, axis=1)
    o_ref[...] = _layer_norm_rows(alpha * h1 + m[5:6, :] * f) * g2_ref[...] + b2_ref[...]


def _combine_call(h1, mod, y8, rw, wts, rows_per_group, group_is_ctx_first, nb, alpha):
    rows = h1.shape[0]
    tm = COMBINE_TILE
    tpg = rows_per_group // tm
    if group_is_ctx_first:
        first = CTX_LEN // tm
        modi = lambda i: jnp.where(i % tpg < first, nb, i // tpg)
    else:
        modi = lambda i: i // tpg

    def full(a):
        return pl.BlockSpec(a.shape, lambda i: (0,) * a.ndim)

    return pl.pallas_call(
        functools.partial(_combine_kernel, alpha=alpha),
        out_shape=jax.ShapeDtypeStruct((rows, D_MODEL), F32),
        grid=(rows // tm,),
        in_specs=[pl.BlockSpec((tm, D_MODEL), lambda i: (i, 0)),
                  pl.BlockSpec((1, 8, D_MODEL), lambda i: (modi(i), 0, 0)),
                  pl.BlockSpec((TOP_K, tm * SLAB, LANES), lambda i: (0, i, 0)),
                  pl.BlockSpec((tm, TOP_K), lambda i: (i, 0))]
                 + [full(w) for w in wts],
        out_specs=pl.BlockSpec((tm, D_MODEL), lambda i: (i, 0)),
        compiler_params=_cparams(("parallel",)),
        name="moe_combine",
    )(h1, mod, y8, rw, *wts)


_BIG_LANE = 1 << 30


def _route_kernel(lg_ref, b_ref, idx_o, w_o, rank_o, cnt_o, carry):
    i = pl.program_id(0)

    @pl.when(i == 0)
    def _():
        carry[...] = jnp.zeros_like(carry)

    tm = lg_ref.shape[0]
    scores = jax.nn.sigmoid(lg_ref[...])
    biased = scores + b_ref[...]
    lane = lax.broadcasted_iota(jnp.int32, (tm, N_EXPERTS), 1)
    gsize = N_EXPERTS // N_GROUPS

    def group_allreduce(x, op):
        k = 1
        while k < gsize:
            up = pltpu.roll(x, N_EXPERTS - k, axis=1)
            dn = pltpu.roll(x, k, axis=1)
            x = op(x, jnp.where((lane & k) == 0, up, dn))
            k *= 2
        return x

    m1 = group_allreduce(biased, jnp.maximum)
    first = group_allreduce(jnp.where(biased == m1, lane, _BIG_LANE), jnp.minimum)
    m2 = group_allreduce(jnp.where(lane == first, -jnp.inf, biased), jnp.maximum)
    gscore = m1 + m2

    gid = lane // gsize
    beaten = jnp.zeros((tm, N_EXPERTS), jnp.int32)
    for g in range(N_GROUPS):
        sg = gscore[:, g * gsize:g * gsize + 1]
        ahead = jnp.logical_or(sg > gscore, jnp.logical_and(sg == gscore, g < gid))
        beaten = beaten + ahead.astype(jnp.int32)
    masked = jnp.where(beaten < TOPK_GROUPS, biased, -jnp.inf)

    idxs, ws, hots = [], [], []
    for _ in range(TOP_K):
        m = jnp.max(masked, axis=1, keepdims=True)
        ix = jnp.min(jnp.where(masked == m, lane, _BIG_LANE), axis=1, keepdims=True)
        hot = lane == ix
        idxs.append(ix)
        ws.append(jnp.sum(jnp.where(hot, scores, 0.0), axis=1, keepdims=True))
        hots.append(hot)
        masked = jnp.where(hot, -jnp.inf, masked)
    wsum = ws[0]
    for r in range(1, TOP_K):
        wsum = wsum + ws[r]
    idx_o[...] = jnp.concatenate(idxs, axis=1)
    w_o[...] = jnp.concatenate([wr / wsum * ROUTED_SCALE for wr in ws], axis=1)

    sel = jnp.zeros((tm, N_EXPERTS), F32)
    for hot in hots:
        sel = sel + hot.astype(F32)
    earlier = (lax.broadcasted_iota(jnp.int32, (tm, tm), 1)
               < lax.broadcasted_iota(jnp.int32, (tm, tm), 0)).astype(BF16)
    prefix = _dot(earlier, sel.astype(BF16)) + carry[...]
    rank_o[...] = jnp.concatenate(
        [jnp.sum(jnp.where(hot, prefix, 0.0), axis=1, keepdims=True) for hot in hots],
        axis=1).astype(jnp.int32)
    carry[...] = carry[...] + jnp.sum(sel, axis=0, keepdims=True)
    cnt_o[...] = carry[...]


def _route_call(logits, b_r):
    t = logits.shape[0]
    tm = ROW_TILE
    return pl.pallas_call(
        _route_kernel,
        out_shape=[jax.ShapeDtypeStruct((t, TOP_K), jnp.int32),
                   jax.ShapeDtypeStruct((t, TOP_K), F32),
                   jax.ShapeDtypeStruct((t, TOP_K), jnp.int32),
                   jax.ShapeDtypeStruct((1, N_EXPERTS), F32)],
        grid=(t // tm,),
        in_specs=[pl.BlockSpec((tm, N_EXPERTS), lambda i: (i, 0)),
                  pl.BlockSpec((1, N_EXPERTS), lambda i: (0, 0))],
        out_specs=[pl.BlockSpec((tm, TOP_K), lambda i: (i, 0)),
                   pl.BlockSpec((tm, TOP_K), lambda i: (i, 0)),
                   pl.BlockSpec((tm, TOP_K), lambda i: (i, 0)),
                   pl.BlockSpec((1, N_EXPERTS), lambda i: (0, 0))],
        scratch_shapes=[pltpu.VMEM((1, N_EXPERTS), F32)],
        compiler_params=_cparams(("arbitrary",)),
        name="route_topk",
    )(logits, b_r.astype(F32).reshape(1, N_EXPERTS))


def _dest_kernel(idx_ref, rank_ref, start_ref, o_ref):
    tm = idx_ref.shape[0]
    lane = lax.broadcasted_iota(jnp.int32, (tm, N_EXPERTS), 1)
    idx = idx_ref[...]
    start = start_ref[...]
    cols = [jnp.sum(jnp.where(lane == idx[:, r:r + 1], start, 0), axis=1, keepdims=True)
            for r in range(TOP_K)]
    o_ref[...] = jnp.concatenate(cols, axis=1) + rank_ref[...]


def _dest_call(idx, rank, pad_start):
    t = idx.shape[0]
    tm = ROW_TILE
    blk = pl.BlockSpec((tm, TOP_K), lambda i: (i, 0))
    return pl.pallas_call(
        _dest_kernel,
        out_shape=jax.ShapeDtypeStruct((t, TOP_K), jnp.int32),
        grid=(t // tm,),
        in_specs=[blk, blk, pl.BlockSpec((1, N_EXPERTS), lambda i: (0, 0))],
        out_specs=blk,
        compiler_params=_cparams(("parallel",)),
        name="slot_of_assignment",
    )(idx, rank, pad_start.reshape(1, N_EXPERTS))


def _dispatch(idx, rank, counts, n_tok):
    n_assign = n_tok * TOP_K
    n_slots = (n_assign + N_EXPERTS * (MOE_BLOCK - 1) + MOE_BLOCK - 1) // MOE_BLOCK * MOE_BLOCK
    nblk = n_slots // MOE_BLOCK
    counts = counts.reshape(N_EXPERTS).astype(jnp.int32)
    padded = (counts + MOE_BLOCK - 1) // MOE_BLOCK * MOE_BLOCK
    pad_end = jnp.cumsum(padded)
    pad_start = pad_end - padded
    dest = _dest_call(idx, rank, pad_start)
    pad_a = n_assign + jnp.arange(n_slots, dtype=jnp.int32) % (2 * MOE_BLOCK)
    slot_a = pad_a.at[dest.reshape(-1)].set(jnp.arange(n_assign, dtype=jnp.int32),
                                            unique_indices=True)
    blk_start = jnp.arange(nblk, dtype=jnp.int32) * MOE_BLOCK
    blk_e = jnp.minimum(jnp.searchsorted(pad_end, blk_start, side='right'),
                        N_EXPERTS - 1).astype(jnp.int32)
    nused = (pad_end[-1] // MOE_BLOCK).astype(jnp.int32).reshape(1)
    return blk_e, nused, slot_a


def _rope_tables(s, ntok):
    rows_n = s // GRID_W
    row = jnp.repeat(jnp.arange(rows_n, dtype=F32), GRID_W)
    col = jnp.tile(jnp.arange(GRID_W, dtype=F32), rows_n)
    axis_dim = HEAD_DIM // 2
    inv = jnp.power(ROPE_THETA, -jnp.arange(0, axis_dim, 2, dtype=F32) / axis_dim)
    ar = row[:, None] * inv[None]
    ac = col[:, None] * inv[None]
    ang = jnp.concatenate([ar, ar, ac, ac], -1)
    cos, sin = jnp.cos(ang), jnp.sin(ang)
    quarter = (jnp.arange(HEAD_DIM) // 16) % 2
    s_up = jnp.where(quarter == 0, -sin, 0.0)
    s_dn = jnp.where(quarter == 1, sin, 0.0)
    nctx = ntok - s

    def expand(t, ctx_val):
        t = jnp.concatenate([jnp.full((nctx, HEAD_DIM), ctx_val, F32), t], axis=0)
        return jnp.tile(t, (1, LANES // HEAD_DIM))

    return expand(cos, 1.0), expand(s_up, 0.0), expand(s_dn, 0.0)


def _head_mean_matrix(width):
    hid = np.arange(width) // HEAD_DIM
    return jnp.asarray((hid[:, None] == hid[None, :]).astype(np.float32) / HEAD_DIM).astype(BF16)


def _dup_heads(a, n_heads):
    parts = []
    for hd in range(n_heads):
        p = a[..., hd * HEAD_DIM:(hd + 1) * HEAD_DIM]
        parts += [p, p]
    return jnp.concatenate(parts, axis=-1)


def kernel(x, c, ctx, c_ctx, w_mod, b_mod, w_in, qn_a, kn_a, lam_q1, lam_k1, lam_q2, lam_k2, subln_c, w_br_a, w_br_b, w_br_c, w_out, ln1_g, ln1_b, w_router, b_router, w_sh_gate, w_sh_up, w_sh_down, w_e_gate, w_e_up, w_e_down, ln2_g, ln2_b):
    nb, s, d = x.shape
    lc = ctx.shape[1]
    depth = w_mod.shape[0]
    assert d == D_MODEL and lc == CTX_LEN and s % ROW_TILE == 0 and s % GRID_W == 0
    ntok = lc + s
    alpha = (2 * depth) ** 0.25

    tabs = _rope_tables(s, ntok)
    pos_lat = _dft_tables(s)
    pos_ctx = _dft_tables(lc)
    chan = _channel_table()
    e_q = _head_mean_matrix(A_Q_W)
    e_k = _head_mean_matrix(2 * A_KV_W)

    cc = jnp.concatenate([c, c_ctx[None, :]], axis=0)
    cc = jnp.pad(cc, ((0, (-(nb + 1)) % 8), (0, 0)))
    h_all = jnp.concatenate([ctx, x], axis=1).reshape(nb * ntok, d)

    offs = np.cumsum([0, A_Q_W, A_KV_W, A_KV_W, B_W, C_QK_W, C_QK_W, C_V_W, GATE_W])
    out = None
    for l in range(depth):
        last = l == depth - 1
        mod = _mod_call(cc, w_mod[l], b_mod[l])[:nb + 1].reshape(nb + 1, 6, d)
        mod = jnp.pad(mod, ((0, 0), (0, 2), (0, 0)))
        lam_init = 0.8 - 0.6 * math.exp(-0.3 * l)
        lam = (jnp.exp(jnp.sum(lam_q1[l].astype(F32) * lam_k1[l].astype(F32)))
               - jnp.exp(jnp.sum(lam_q2[l].astype(F32) * lam_k2[l].astype(F32)))) + lam_init
        lam = lam.reshape(1, 1).astype(F32)

        wl = w_in[l]
        seg = [wl[:, offs[i]:offs[i + 1]] for i in range(8)]
        in_wts = (seg[0].astype(BF16), _dup_heads(seg[1], A_KV_HEADS).astype(BF16),
                  _dup_heads(seg[2], A_KV_HEADS).astype(BF16), seg[3].astype(BF16),
                  seg[4].astype(BF16), seg[5].astype(BF16), seg[6].astype(BF16),
                  jnp.tile(qn_a[l].astype(F32), A_Q_HEADS).reshape(1, A_Q_W),
                  jnp.tile(kn_a[l].astype(F32), 2 * A_KV_HEADS).reshape(1, 2 * A_KV_W),
                  e_q, e_k)
        qa, ka, va, fb, qc, kc, vc = _inproj_call(h_all, mod, tabs, in_wts, nb, ntok)

        first_tile = 1 if last else 0
        r3 = lambda a: a.reshape(nb, ntok, a.shape[-1])
        oa = _attn_a_call(r3(qa), r3(ka), r3(va), nb, ntok, first_tile)
        oc = _attn_c_call(lam, r3(qc), r3(kc), r3(vc), subln_c[l].astype(F32).reshape(1, C_V_DIM),
                          nb, ntok, first_tile, 1.0 - lam_init)
        ob = _fourier_call(r3(fb), chan, pos_lat, pos_ctx, nb, ntok, first_tile)

        mix_wts = (seg[7].astype(BF16), w_br_a[l].astype(BF16), w_br_b[l].astype(BF16),
                   w_br_c[l].astype(BF16), w_out[l].astype(BF16),
                   ln1_g[l].astype(F32).reshape(1, d), ln1_b[l].astype(F32).reshape(1, d),
                   w_router[l].astype(BF16))
        flat = lambda a: a.reshape(-1, a.shape[-1])
        h1, xp, logits = _mixout_call(h_all, mod, flat(oa), flat(ob), flat(oc), mix_wts,
                                      nb, ntok, last, alpha)

        n_tok = h1.shape[0]
        idx, rw, rank, counts = _route_call(logits, b_router[l])
        blk_e, nused, slot_a = _dispatch(idx, rank, counts, n_tok)
        y8 = _moe_call(blk_e, nused, slot_a,
                       xp.reshape(4 * n_tok, LANES), w_e_gate[l], w_e_up[l], w_e_down[l], n_tok)
        y8 = y8.reshape(TOP_K, (n_tok + PAD_ROWS) * SLAB, LANES)

        comb_wts = (w_sh_gate[l].astype(BF16), w_sh_up[l].astype(BF16), w_sh_down[l].astype(BF16),
                    ln2_g[l].astype(F32).reshape(1, d), ln2_b[l].astype(F32).reshape(1, d))
        h2 = _combine_call(h1, mod, y8, rw, comb_wts, s if last else ntok, not last, nb, alpha)
        if last:
            out = h2.reshape(nb, s, d)
        else:
            h_all = h2
    return out
```

```python
import functools
import math

import numpy as np
import jax
import jax.numpy as jnp
from jax import lax
from jax.experimental import pallas as pl
from jax.experimental.pallas import tpu as pltpu

F32 = jnp.float32
BF16 = jnp.bfloat16
U32 = jnp.uint32

D_MODEL = 1024
CTX_LEN = 256
GRID_W = 64
HEAD_DIM = 64
ROPE_THETA = 10000.0
A_Q_HEADS = 8
A_KV_HEADS = 2
A_Q_W = A_Q_HEADS * HEAD_DIM
A_KV_W = A_KV_HEADS * HEAD_DIM
F_GROUPS = 4
F_GROUP_W = 128
B_W = F_GROUPS * F_GROUP_W
C_HEADS = 4
C_V_DIM = 2 * HEAD_DIM
C_QK_W = C_HEADS * 2 * HEAD_DIM
C_V_W = C_HEADS * C_V_DIM
N_BRANCH = 3
GATE_W = N_BRANCH * D_MODEL
N_EXPERTS = 256
TOP_K = 8
N_GROUPS = 8
TOPK_GROUPS = 4
EXPERT_FF = 256
SHARED_FF = 256
ROUTED_SCALE = 2.5
LN_EPS = 1e-5
RMS_EPS = 1e-6
K_SHIFT = TOP_K.bit_length() - 1
assert 1 << K_SHIFT == TOP_K

LANES = 128
PSLAB = D_MODEL // (2 * LANES)
ROW_TILE = 256
MOE_BLOCK = 128
PAD_ROWS = 2 * MOE_BLOCK // TOP_K
COMBINE_TILE = 128
VMEM_LIMIT = 56 * 1024 * 1024

_SQRT_HALF_SCALE = HEAD_DIM ** -0.5


def _cparams(sem):
    return pltpu.CompilerParams(dimension_semantics=sem, vmem_limit_bytes=VMEM_LIMIT)


def _dot(a, b):
    return jnp.dot(a, b, preferred_element_type=F32)


def _dot_nt(a, b):
    return lax.dot_general(a, b, (((1,), (1,)), ((), ())), preferred_element_type=F32)


def _layer_norm_rows(x):
    mu = jnp.mean(x, axis=-1, keepdims=True)
    xc = x - mu
    var = jnp.mean(xc * xc, axis=-1, keepdims=True)
    return xc * lax.rsqrt(var + LN_EPS)


def _dot_split(x, e):
    hi = x.astype(BF16)
    lo = (x - hi.astype(F32)).astype(BF16)
    return _dot(hi, e) + _dot(lo, e)


def _mod_kernel(c_ref, w_ref, b_ref, o_ref):
    c = c_ref[...]
    sc = c * jax.nn.sigmoid(c)
    o_ref[...] = _dot(sc.astype(BF16), w_ref[...].astype(BF16)) + b_ref[...]


def _mod_call(cc, w_mod_l, b_mod_l):
    r = cc.shape[0]
    n = w_mod_l.shape[1]
    tn = D_MODEL
    return pl.pallas_call(
        _mod_kernel,
        out_shape=jax.ShapeDtypeStruct((r, n), F32),
        grid=(n // tn,),
        in_specs=[
            pl.BlockSpec((r, D_MODEL), lambda j: (0, 0)),
            pl.BlockSpec((D_MODEL, tn), lambda j: (0, j)),
            pl.BlockSpec((1, tn), lambda j: (0, j)),
        ],
        out_specs=pl.BlockSpec((r, tn), lambda j: (0, j)),
        compiler_params=_cparams(("arbitrary",)),
        name="mod_vectors",
    )(cc, w_mod_l, b_mod_l.reshape(1, n))


def _rope_cols(x, cos, sin_up, sin_dn):
    cols = []
    for c in range(x.shape[1] // LANES):
        xc = x[:, c * LANES:(c + 1) * LANES]
        up = pltpu.roll(xc, LANES - 16, axis=1)
        dn = pltpu.roll(xc, 16, axis=1)
        cols.append(xc * cos + up * sin_up + dn * sin_dn)
    return jnp.concatenate(cols, axis=1) if len(cols) > 1 else cols[0]


def _inproj_kernel(h_ref, mod_ref, cos_ref, su_ref, sd_ref,
                   wq_ref, wk_ref, wv_ref, wf_ref, wqc_ref, wkc_ref, wvc_ref,
                   qn_ref, kn_ref, eq_ref, ek_ref,
                   qa_o, ka_o, va_o, fb_o, qc_o, kc_o, vc_o):
    h = h_ref[...]
    shift = mod_ref[0, 0:1, :]
    scale = mod_ref[0, 1:2, :]
    u = (_layer_norm_rows(h) * (1.0 + scale) + shift).astype(BF16)
    cos = cos_ref[...]
    s_up = su_ref[...]
    s_dn = sd_ref[...]

    q = _dot(u, wq_ref[...])
    ms = _dot_split(q * q, eq_ref[...])
    q = q * lax.rsqrt(ms + RMS_EPS) * qn_ref[...]
    qa_o[...] = (_rope_cols(q, cos, s_up, s_dn) * _SQRT_HALF_SCALE).astype(BF16)

    k = _dot(u, wk_ref[...])
    ms = _dot_split(k * k, ek_ref[...])
    k = k * lax.rsqrt(ms + RMS_EPS) * kn_ref[...]
    ka_o[...] = _rope_cols(k, cos, s_up, s_dn).astype(BF16)

    va_o[...] = _dot(u, wv_ref[...]).astype(BF16)
    fb_o[...] = _dot(u, wf_ref[...]).astype(BF16)
    qc = _dot(u, wqc_ref[...])
    qc_o[...] = (_rope_cols(qc, cos, s_up, s_dn) * _SQRT_HALF_SCALE).astype(BF16)
    kc = _dot(u, wkc_ref[...])
    kc_o[...] = _rope_cols(kc, cos, s_up, s_dn).astype(BF16)
    vc_o[...] = _dot(u, wvc_ref[...]).astype(BF16)


def _inproj_call(h_all, mod, tabs, wts, nb, ntok):
    rows = h_all.shape[0]
    tpb = ntok // ROW_TILE
    cos, s_up, s_dn = tabs

    def full(a):
        return pl.BlockSpec(a.shape, lambda i: (0,) * a.ndim)

    def rowspec(w):
        return pl.BlockSpec((ROW_TILE, w), lambda i: (i, 0))

    tabspec = pl.BlockSpec((ROW_TILE, LANES), lambda i: (i % tpb, 0))
    modspec = pl.BlockSpec((1, 8, D_MODEL),
                           lambda i: (jnp.where(i % tpb == 0, nb, i // tpb), 0, 0))
    widths = (A_Q_W, 2 * A_KV_W, 2 * A_KV_W, B_W, C_QK_W, C_QK_W, C_V_W)
    return pl.pallas_call(
        _inproj_kernel,
        out_shape=[jax.ShapeDtypeStruct((rows, w), BF16) for w in widths],
        grid=(rows // ROW_TILE,),
        in_specs=[rowspec(D_MODEL), modspec, tabspec, tabspec, tabspec]
                 + [full(w) for w in wts],
        out_specs=[rowspec(w) for w in widths],
        compiler_params=_cparams(("parallel",)),
        name="in_projection",
    )(h_all, mod, cos, s_up, s_dn, *wts)


def _softmax_parts(s):
    m = jnp.max(s, axis=-1, keepdims=True)
    e = jnp.exp(s - m)
    return e, jnp.sum(e, axis=-1, keepdims=True)


def _gqa_tile(q_ref, k_ref, v_ref, o_ref, nk):
    lane = lax.broadcasted_iota(jnp.int32, (1, LANES), 1)
    low = lane < HEAD_DIM
    for c in range(A_Q_W // LANES):
        kvh = (2 * c) // (A_Q_HEADS // A_KV_HEADS)
        qc = q_ref[0, :, c * LANES:(c + 1) * LANES]
        kk = k_ref[0, 0:nk, kvh * LANES:(kvh + 1) * LANES]
        vv = v_ref[0, 0:nk, kvh * LANES:(kvh + 1) * LANES]
        halves = []
        for keep in (low, jnp.logical_not(low)):
            qm = jnp.where(keep, qc, jnp.zeros_like(qc))
            e, l = _softmax_parts(_dot_nt(qm, kk))
            halves.append(_dot(e.astype(BF16), vv) * (1.0 / l))
        o_ref[0, :, c * LANES:(c + 1) * LANES] = jnp.where(low, halves[0], halves[1]).astype(BF16)


def _attn_a_kernel(q_ref, k_ref, v_ref, o_ref, *, ntok, first_tile):
    j = pl.program_id(1) + first_tile
    if first_tile == 0:
        @pl.when(j == 0)
        def _():
            _gqa_tile(q_ref, k_ref, v_ref, o_ref, CTX_LEN)

        @pl.when(j != 0)
        def _():
            _gqa_tile(q_ref, k_ref, v_ref, o_ref, ntok)
    else:
        _gqa_tile(q_ref, k_ref, v_ref, o_ref, ntok)


def _diff_tile(lam_ref, q_ref, k_ref, v_ref, g_ref, o_ref, nk, out_scale):
    lane = lax.broadcasted_iota(jnp.int32, (1, LANES), 1)
    low = lane < HEAD_DIM
    lam = lam_ref[0, 0]
    for hd in range(C_HEADS):
        sl = slice(hd * LANES, (hd + 1) * LANES)
        qc = q_ref[0, :, sl]
        kk = k_ref[0, 0:nk, sl]
        vv = v_ref[0, 0:nk, sl]
        q1 = jnp.where(low, qc, jnp.zeros_like(qc))
        q2 = jnp.where(low, jnp.zeros_like(qc), qc)
        e1, l1 = _softmax_parts(_dot_nt(q1, kk))
        e2, l2 = _softmax_parts(_dot_nt(q2, kk))
        p = e1 * (1.0 / l1) - e2 * (lam / l2)
        o = _dot(p.astype(BF16), vv)
        ms = jnp.mean(o * o, axis=-1, keepdims=True)
        o = o * lax.rsqrt(ms + RMS_EPS) * g_ref[...] * out_scale
        o_ref[0, :, sl] = o.astype(BF16)


def _attn_c_kernel(lam_ref, q_ref, k_ref, v_ref, g_ref, o_ref, *, ntok, first_tile, out_scale):
    j = pl.program_id(1) + first_tile
    if first_tile == 0:
        @pl.when(j == 0)
        def _():
            _diff_tile(lam_ref, q_ref, k_ref, v_ref, g_ref, o_ref, CTX_LEN, out_scale)

        @pl.when(j != 0)
        def _():
            _diff_tile(lam_ref, q_ref, k_ref, v_ref, g_ref, o_ref, ntok, out_scale)
    else:
        _diff_tile(lam_ref, q_ref, k_ref, v_ref, g_ref, o_ref, ntok, out_scale)


def _attn_specs(nb, ntok, first_tile, wq, wkv):
    tpb = ntok // ROW_TILE
    grid = (nb, tpb - first_tile)
    qspec = pl.BlockSpec((1, ROW_TILE, wq), lambda b, j: (b, j + first_tile, 0))
    kvspec = pl.BlockSpec((1, ntok, wkv), lambda b, j: (b, 0, 0))
    ospec = pl.BlockSpec((1, ROW_TILE, wq), lambda b, j: (b, j, 0))
    out_rows = ntok - first_tile * ROW_TILE
    return grid, qspec, kvspec, ospec, out_rows


def _attn_a_call(qa, ka, va, nb, ntok, first_tile):
    grid, qspec, kvspec, ospec, out_rows = _attn_specs(nb, ntok, first_tile, A_Q_W, 2 * A_KV_W)
    return pl.pallas_call(
        functools.partial(_attn_a_kernel, ntok=ntok, first_tile=first_tile),
        out_shape=jax.ShapeDtypeStruct((nb, out_rows, A_Q_W), BF16),
        grid=grid,
        in_specs=[qspec, kvspec, kvspec],
        out_specs=ospec,
        compiler_params=_cparams(("parallel", "arbitrary")),
        name="gqa_attention",
    )(qa, ka, va)


def _attn_c_call(lam, qc, kc, vc, subln, nb, ntok, first_tile, out_scale):
    grid, qspec, kvspec, ospec, out_rows = _attn_specs(nb, ntok, first_tile, C_QK_W, C_QK_W)
    return pl.pallas_call(
        functools.partial(_attn_c_kernel, ntok=ntok, first_tile=first_tile, out_scale=out_scale),
        out_shape=jax.ShapeDtypeStruct((nb, out_rows, C_V_W), BF16),
        grid=grid,
        in_specs=[pl.BlockSpec(memory_space=pltpu.SMEM), qspec, kvspec, kvspec,
                  pl.BlockSpec((1, C_V_DIM), lambda b, j: (0, 0))],
        out_specs=ospec,
        compiler_params=_cparams(("parallel", "arbitrary")),
        name="diff_attention",
    )(lam, qc, kc, vc, subln)


def _fourier_kernel(f_ref, chan_ref, pos_ref, posc_ref, o_ref, g_ref, *, ntok, first_tile):
    j = pl.program_id(1) + first_tile
    nlat = ntok - CTX_LEN

    def channel_stage(rows):
        g = _dot(rows, chan_ref[...])
        return jnp.concatenate([g[:, :B_W], g[:, B_W:]], axis=0).astype(BF16)

    if first_tile == 0:
        @pl.when(j == 0)
        def _():
            gc = channel_stage(f_ref[0, 0:CTX_LEN, :])
            y = _dot(posc_ref[...], gc) * (1.0 / math.sqrt(CTX_LEN * F_GROUP_W))
            o_ref[0] = y.astype(BF16)

    @pl.when(j == 1)
    def _():
        g_ref[...] = channel_stage(f_ref[0, CTX_LEN:ntok, :])

    @pl.when(j >= 1)
    def _():
        y = _dot(pos_ref[...], g_ref[...]) * (1.0 / math.sqrt(nlat * F_GROUP_W))
        o_ref[0] = y.astype(BF16)


def _fourier_call(fb, chan, pos, posc, nb, ntok, first_tile):
    tpb = ntok // ROW_TILE
    nlat = ntok - CTX_LEN
    return pl.pallas_call(
        functools.partial(_fourier_kernel, ntok=ntok, first_tile=first_tile),
        out_shape=jax.ShapeDtypeStruct((nb, ntok - first_tile * ROW_TILE, B_W), BF16),
        grid=(nb, tpb - first_tile),
        in_specs=[
            pl.BlockSpec((1, ntok, B_W), lambda b, j: (b, 0, 0)),
            pl.BlockSpec(chan.shape, lambda b, j: (0, 0)),
            pl.BlockSpec((ROW_TILE, 2 * nlat),
                         lambda b, j: (jnp.maximum(j + first_tile - 1, 0), 0)),
            pl.BlockSpec(posc.shape, lambda b, j: (0, 0)),
        ],
        out_specs=pl.BlockSpec((1, ROW_TILE, B_W), lambda b, j: (b, j, 0)),
        scratch_shapes=[pltpu.VMEM((2 * nlat, B_W), BF16)],
        compiler_params=_cparams(("parallel", "arbitrary")),
        name="fourier_mix",
    )(fb, chan, pos, posc)


def _dft_tables(n):
    n1 = 32
    n0 = n // n1
    k = np.arange(n, dtype=np.int64)
    a = 2.0 * np.pi * ((k[:, None] * np.arange(n1)[None, :] * n0) % n) / n
    b = 2.0 * np.pi * ((k[:, None] * np.arange(n0)[None, :]) % n) / n
    ca, sa = jnp.asarray(np.cos(a), F32)[:, :, None], jnp.asarray(np.sin(a), F32)[:, :, None]
    cb, sb = jnp.asarray(np.cos(b), F32)[:, None, :], jnp.asarray(np.sin(b), F32)[:, None, :]
    cos = (ca * cb - sa * sb).reshape(n, n)
    sin = (sa * cb + ca * sb).reshape(n, n)
    return jnp.concatenate([cos, -sin], axis=1).astype(BF16)


def _channel_table():
    c = np.arange(F_GROUP_W)
    ang = 2.0 * np.pi * ((c[:, None] * c[None, :]) % F_GROUP_W) / F_GROUP_W
    eye = np.eye(F_GROUPS)
    cos = np.kron(eye, np.cos(ang))
    sin = np.kron(eye, np.sin(ang))
    return jnp.asarray(np.concatenate([cos, sin], axis=1), F32).astype(BF16)


def _store_packed_slabs(dst_ref, u):
    rows = u.shape[0]
    for jb in range(PSLAB):
        lo = u[:, 2 * jb * LANES:(2 * jb + 1) * LANES]
        hi = u[:, (2 * jb + 1) * LANES:(2 * jb + 2) * LANES]
        dst_ref[pl.ds(jb, rows, stride=PSLAB), :] = pltpu.pack_elementwise([lo, hi], packed_dtype=BF16)


def _load_packed_slabs(src_ref, rows):
    out = []
    for jb in range(PSLAB):
        words = src_ref[pl.ds(jb, rows, stride=PSLAB), :]
        out.append((pltpu.unpack_elementwise(words, index=0, packed_dtype=BF16, unpacked_dtype=F32),
                    pltpu.unpack_elementwise(words, index=1, packed_dtype=BF16, unpacked_dtype=F32)))
    return out


def _mixout_kernel(h_ref, mod_ref, oa_ref, ob_ref, oc_ref,
                   wg_ref, wa_ref, wb_ref, wc_ref, wo_ref, g1_ref, b1_ref, wr_ref,
                   h1_o, xp_o, lg_o, *, alpha):
    h = h_ref[...]
    m = mod_ref[0]
    u = (_layer_norm_rows(h) * (1.0 + m[1:2, :]) + m[0:1, :]).astype(BF16)
    y = None
    for n, (o_ref, w_ref) in enumerate(((oa_ref, wa_ref), (ob_ref, wb_ref), (oc_ref, wc_ref))):
        gate = jax.nn.sigmoid(_dot(u, wg_ref[:, n * D_MODEL:(n + 1) * D_MODEL]))
        t = gate * _dot(o_ref[...], w_ref[...])
        y = t if y is None else y + t
    z = _dot(y.astype(BF16), wo_ref[...])
    h1 = _layer_norm_rows(alpha * h + m[2:3, :] * z) * g1_ref[...] + b1_ref[...]
    h1_o[...] = h1
    u2 = _layer_norm_rows(h1) * (1.0 + m[4:5, :]) + m[3:4, :]
    _store_packed_slabs(xp_o, u2)
    lg_o[...] = _dot_nt(wr_ref[...], u2.astype(BF16))


def _tile_maps(nb, ntok, lat_only):
    tpb = ntok // ROW_TILE
    if lat_only:
        lpb = tpb - 1
        n_tiles = nb * lpb
        src = lambda i: (i // lpb) * tpb + 1 + i % lpb
        modi = lambda i: i // lpb
    else:
        n_tiles = nb * tpb
        src = lambda i: i
        modi = lambda i: jnp.where(i % tpb == 0, nb, i // tpb)
    return n_tiles, src, modi


def _mixout_call(h_all, mod, oa, ob, oc, wts, nb, ntok, lat_only, alpha):
    n_tiles, src, modi = _tile_maps(nb, ntok, lat_only)
    rows_out = n_tiles * ROW_TILE

    def full(a):
        return pl.BlockSpec(a.shape, lambda i: (0,) * a.ndim)

    def inrow(w):
        return pl.BlockSpec((ROW_TILE, w), lambda i: (src(i), 0))

    def outrow(w):
        return pl.BlockSpec((ROW_TILE, w), lambda i: (i, 0))

    modspec = pl.BlockSpec((1, 8, D_MODEL), lambda i: (modi(i), 0, 0))
    return pl.pallas_call(
        functools.partial(_mixout_kernel, alpha=alpha),
        out_shape=[jax.ShapeDtypeStruct((rows_out, D_MODEL), F32),
                   jax.ShapeDtypeStruct((rows_out * PSLAB, LANES), U32),
                   jax.ShapeDtypeStruct((N_EXPERTS, rows_out), F32)],
        grid=(n_tiles,),
        in_specs=[inrow(D_MODEL), modspec, outrow(A_Q_W), outrow(B_W), outrow(C_V_W)]
                 + [full(w) for w in wts],
        out_specs=[outrow(D_MODEL), pl.BlockSpec((ROW_TILE * PSLAB, LANES), lambda i: (i, 0)),
                   pl.BlockSpec((N_EXPERTS, ROW_TILE), lambda i: (0, i))],
        compiler_params=_cparams(("parallel",)),
        name="mixer_output",
    )(h_all, mod, oa, ob, oc, *wts)


def _moe_kernel(first_ref, count_ref, total_ref,
                tab_hbm, xp_ref, wg_ref, wu_ref, wd_ref,
                out_hbm,
                tab, gath, wgu_s, wd_s, ybuf, zbuf, sem, tsem, zsem, *, t_pad, n_tok):
    e = pl.program_id(0)
    total = total_ref[0]

    def row_copy(s, m, dst4):
        return pltpu.make_async_copy(ybuf.at[s, pl.ds(PSLAB * m, PSLAB), :],
                                     out_hbm.at[pl.ds(pl.multiple_of(dst4, PSLAB), PSLAB), :],
                                     sem.at[s])

    def wait_rows(s):
        for m in range(MOE_BLOCK):
            row_copy(s, m, 0).wait()

    def tab_copy(g, s):
        return pltpu.make_async_copy(tab_hbm.at[g], tab.at[s], tsem.at[s])

    @pl.when(e == 0)
    def _():
        zbuf[...] = jnp.zeros_like(zbuf)
        for kk in range(TOP_K):
            cp = pltpu.make_async_copy(
                zbuf, out_hbm.at[pl.ds((kk * t_pad + n_tok) * PSLAB, PAD_ROWS * PSLAB), :], zsem)
            cp.start()
            cp.wait()

        @pl.when(total > 0)
        def _():
            tab_copy(0, 0).start()

    wgu_s[:, 0:EXPERT_FF] = wg_ref[0, 0].astype(BF16)
    wgu_s[:, EXPERT_FF:2 * EXPERT_FF] = wu_ref[0, 0].astype(BF16)
    wd_s[...] = wd_ref[0, 0].astype(BF16)

    def block(j, carry):
        g = first_ref[e] + j
        s = lax.rem(g, 2)
        tab_copy(g, s).wait()

        @pl.when(g + 1 < total)
        def _():
            tab_copy(g + 1, 1 - s).start()

        @pl.when(g >= 2)
        def _():
            wait_rows(s)

        for m in range(MOE_BLOCK):
            t4 = pl.multiple_of(tab[s, m], PSLAB)
            gath[PSLAB * m:PSLAB * (m + 1), :] = xp_ref[pl.ds(t4, PSLAB), :]

        acc = None
        for jb, (lo, hi) in enumerate(_load_packed_slabs(gath, MOE_BLOCK)):
            a = jnp.concatenate([lo, hi], axis=1).astype(BF16)
            t = _dot(a, wgu_s[jb * 2 * LANES:(jb + 1) * 2 * LANES, :])
            acc = t if acc is None else acc + t
        gt = acc[:, 0:EXPERT_FF]
        act = (gt * jax.nn.sigmoid(gt)) * acc[:, EXPERT_FF:2 * EXPERT_FF]
        y = _dot(act.astype(BF16), wd_s[...])

        _store_packed_slabs(ybuf.at[s], y)
        for m in range(MOE_BLOCK):
            row_copy(s, m, tab[s, MOE_BLOCK + m]).start(priority=m % 2)
        return carry

    lax.fori_loop(0, count_ref[e], block, 0)

    @pl.when(e == pl.num_programs(0) - 1)
    def _():
        @pl.when(total >= 1)
        def _():
            wait_rows(lax.rem(total - 1, 2))

        @pl.when(total >= 2)
        def _():
            wait_rows(lax.rem(total - 2, 2))


def _moe_call(first_blk, n_blk, total, table, xp4, wg, wu, wd, layer, n_tok):
    t_pad = n_tok + PAD_ROWS
    wspec = lambda r, c: pl.BlockSpec((1, 1, r, c), lambda e, fb, nb, tt: (layer, e, 0, 0))
    grid_spec = pltpu.PrefetchScalarGridSpec(
        num_scalar_prefetch=3,
        grid=(N_EXPERTS,),
        in_specs=[
            pl.BlockSpec(memory_space=pl.ANY),
            pl.BlockSpec(memory_space=pltpu.VMEM),
            wspec(D_MODEL, EXPERT_FF), wspec(D_MODEL, EXPERT_FF), wspec(EXPERT_FF, D_MODEL),
        ],
        out_specs=pl.BlockSpec(memory_space=pl.ANY),
        scratch_shapes=[
            pltpu.SMEM((2, 2 * MOE_BLOCK), jnp.int32),
            pltpu.VMEM((MOE_BLOCK * PSLAB, LANES), U32),
            pltpu.VMEM((D_MODEL, 2 * EXPERT_FF), BF16),
            pltpu.VMEM((EXPERT_FF, D_MODEL), BF16),
            pltpu.VMEM((2, MOE_BLOCK * PSLAB, LANES), U32),
            pltpu.VMEM((PAD_ROWS * PSLAB, LANES), U32),
            pltpu.SemaphoreType.DMA((2,)),
            pltpu.SemaphoreType.DMA((2,)),
            pltpu.SemaphoreType.DMA(()),
        ],
    )
    return pl.pallas_call(
        functools.partial(_moe_kernel, t_pad=t_pad, n_tok=n_tok),
        out_shape=jax.ShapeDtypeStruct((TOP_K * t_pad * PSLAB, LANES), U32),
        grid_spec=grid_spec,
        compiler_params=_cparams(("arbitrary",)),
        name="routed_experts",
    )(first_blk, n_blk, total, table, xp4, wg, wu, wd)


def _combine_kernel(h1_ref, mod_ref, y8_ref, rw_ref, sg_ref, su_ref, sd_ref, g2_ref, b2_ref, o_ref,
                    *, alpha):
    h1 = h1_ref[...]
    m = mod_ref[0]
    u2 = (_layer_norm_rows(h1) * (1.0 + m[4:5, :]) + m[3:4, :]).astype(BF16)
    g = _dot(u2, sg_ref[...])
    act = (g * jax.nn.sigmoid(g)) * _dot(u2, su_ref[...])
    f = _dot(act.astype(BF16), sd_ref[...])
    tm = h1.shape[0]
    rw = rw_ref[...]
    wk = [jnp.broadcast_to(rw[:, kk:kk + 1], (tm, LANES)) for kk in range(TOP_K)]
    cols = [None] * (2 * PSLAB)
    for kk in range(TOP_K):
        for jb, pair in enumerate(_load_packed_slabs(y8_ref.at[kk], tm)):
            for half in range(2):
                t = wk[kk] * pair[half]
                c = 2 * jb + half
                cols[c] = t if cols[c] is None else cols[c] + t
    f = f + jnp.concatenate(cols, axis=1)
    o_ref[...] = _layer_norm_rows(alpha * h1 + m[5:6, :] * f) * g2_ref[...] + b2_ref[...]


def _combine_call(h1, mod, y8, rw, wts, rows_per_group, group_is_ctx_first, nb, alpha):
    rows = h1.shape[0]
    tm = COMBINE_TILE
    tpg = rows_per_group // tm
    if group_is_ctx_first:
        first = CTX_LEN // tm
        modi = lambda i: jnp.where(i % tpg < first, nb, i // tpg)
    else:
        modi = lambda i: i // tpg

    def full(a):
        return pl.BlockSpec(a.shape, lambda i: (0,) * a.ndim)

    return pl.pallas_call(
        functools.partial(_combine_kernel, alpha=alpha),
        out_shape=jax.ShapeDtypeStruct((rows, D_MODEL), F32),
        grid=(rows // tm,),
        in_specs=[pl.BlockSpec((tm, D_MODEL), lambda i: (i, 0)),
                  pl.BlockSpec((1, 8, D_MODEL), lambda i: (modi(i), 0, 0)),
                  pl.BlockSpec((TOP_K, tm * PSLAB, LANES), lambda i: (0, i, 0)),
                  pl.BlockSpec((tm, TOP_K), lambda i: (i, 0))]
                 + [full(w) for w in wts],
        out_specs=pl.BlockSpec((tm, D_MODEL), lambda i: (i, 0)),
        compiler_params=_cparams(("parallel",)),
        name="moe_combine",
    )(h1, mod, y8, rw, *wts)


_BIG_LANE = 1 << 30


def _route_kernel(lg_ref, b_ref, idx_o, w_o, rank_o, cnt_o, carry):
    i = pl.program_id(0)

    @pl.when(i == 0)
    def _():
        carry[...] = jnp.zeros_like(carry)

    tm = lg_ref.shape[1]
    gsize = N_EXPERTS // N_GROUPS
    scores = jax.nn.sigmoid(lg_ref[...])
    biased = scores + b_ref[...]
    eid = lax.broadcasted_iota(jnp.int32, (N_EXPERTS, tm), 0)

    b3 = biased.reshape(N_GROUPS, gsize, tm)
    in_g = lax.broadcasted_iota(jnp.int32, (N_GROUPS, gsize, tm), 1)
    m1 = jnp.max(b3, axis=1, keepdims=True)
    first = jnp.min(jnp.where(b3 == m1, in_g, _BIG_LANE), axis=1, keepdims=True)
    m2 = jnp.max(jnp.where(in_g == first, -jnp.inf, b3), axis=1, keepdims=True)
    gscore = (m1 + m2).reshape(N_GROUPS, tm)

    gid = lax.broadcasted_iota(jnp.int32, (N_GROUPS, tm), 0)
    beaten = jnp.zeros((N_GROUPS, tm), jnp.int32)
    for g in range(N_GROUPS):
        sg = gscore[g:g + 1, :]
        ahead = jnp.logical_or(sg > gscore, jnp.logical_and(sg == gscore, g < gid))
        beaten = beaten + ahead.astype(jnp.int32)
    keep = jnp.broadcast_to((beaten < TOPK_GROUPS).astype(jnp.int32).reshape(N_GROUPS, 1, tm),
                            (N_GROUPS, gsize, tm)).reshape(N_EXPERTS, tm)
    masked = jnp.where(keep > 0, biased, -jnp.inf)

    idxs, ws, hots = [], [], []
    for _ in range(TOP_K):
        m = jnp.max(masked, axis=0, keepdims=True)
        ix = jnp.min(jnp.where(masked == m, eid, _BIG_LANE), axis=0, keepdims=True)
        hot = eid == ix
        idxs.append(ix)
        ws.append(jnp.sum(jnp.where(hot, scores, 0.0), axis=0, keepdims=True))
        hots.append(hot)
        masked = jnp.where(hot, -jnp.inf, masked)
    wsum = ws[0]
    for r in range(1, TOP_K):
        wsum = wsum + ws[r]
    idx_o[...] = jnp.concatenate(idxs, axis=0)
    w_rows = jnp.concatenate([wr / wsum * ROUTED_SCALE for wr in ws], axis=0)

    eye = (lax.broadcasted_iota(jnp.int32, (tm, tm), 0)
           == lax.broadcasted_iota(jnp.int32, (tm, tm), 1)).astype(BF16)
    w_cols = jnp.zeros((tm, TOP_K), F32)
    rest = w_rows
    for _ in range(3):
        part = rest.astype(BF16)
        rest = rest - part.astype(F32)
        w_cols = w_cols + _dot_nt(eye, part)
    w_o[...] = w_cols

    sel = jnp.zeros((N_EXPERTS, tm), F32)
    for hot in hots:
        sel = sel + hot.astype(F32)
    sel = sel.astype(BF16)
    earlier = (lax.broadcasted_iota(jnp.int32, (tm, tm), 0)
               < lax.broadcasted_iota(jnp.int32, (tm, tm), 1)).astype(BF16)
    prefix = _dot(sel, earlier) + carry[:, 0:1]
    rank_o[...] = jnp.concatenate(
        [jnp.sum(jnp.where(hot, prefix, 0.0), axis=0, keepdims=True) for hot in hots],
        axis=0).astype(jnp.int32)
    carry[...] = carry[...] + _dot(sel, jnp.ones((tm, LANES), BF16))
    cnt_o[...] = carry[...]


def _route_call(logits_t, b_r):
    t = logits_t.shape[1]
    tm = ROW_TILE
    kt = pl.BlockSpec((TOP_K, tm), lambda i: (0, i))
    return pl.pallas_call(
        _route_kernel,
        out_shape=[jax.ShapeDtypeStruct((TOP_K, t), jnp.int32),
                   jax.ShapeDtypeStruct((t, TOP_K), F32),
                   jax.ShapeDtypeStruct((TOP_K, t), jnp.int32),
                   jax.ShapeDtypeStruct((N_EXPERTS, LANES), F32)],
        grid=(t // tm,),
        in_specs=[pl.BlockSpec((N_EXPERTS, tm), lambda i: (0, i)),
                  pl.BlockSpec((N_EXPERTS, 1), lambda i: (0, 0))],
        out_specs=[kt, pl.BlockSpec((tm, TOP_K), lambda i: (i, 0)), kt,
                   pl.BlockSpec((N_EXPERTS, LANES), lambda i: (0, 0))],
        scratch_shapes=[pltpu.VMEM((N_EXPERTS, LANES), F32)],
        compiler_params=_cparams(("arbitrary",)),
        name="route_topk",
    )(logits_t, b_r.astype(F32).reshape(N_EXPERTS, 1))


def _dest_kernel(idx_ref, rank_ref, start_ref, o_ref):
    tm = idx_ref.shape[1]
    eid = lax.broadcasted_iota(jnp.int32, (N_EXPERTS, tm), 0)
    idx = idx_ref[...]
    start = start_ref[...]
    rows = [jnp.sum(jnp.where(eid == idx[r:r + 1, :], start, 0), axis=0, keepdims=True)
            for r in range(TOP_K)]
    o_ref[...] = jnp.concatenate(rows, axis=0) + rank_ref[...]


def _dest_call(idx, rank, pad_start):
    t = idx.shape[1]
    tm = ROW_TILE
    blk = pl.BlockSpec((TOP_K, tm), lambda i: (0, i))
    return pl.pallas_call(
        _dest_kernel,
        out_shape=jax.ShapeDtypeStruct((TOP_K, t), jnp.int32),
        grid=(t // tm,),
        in_specs=[blk, blk, pl.BlockSpec((N_EXPERTS, 1), lambda i: (0, 0))],
        out_specs=blk,
        compiler_params=_cparams(("parallel",)),
        name="slot_of_assignment",
    )(idx, rank, pad_start.reshape(N_EXPERTS, 1))


def _dispatch(idx, rank, counts, n_tok):
    n_assign = n_tok * TOP_K
    n_slots = (n_assign + N_EXPERTS * (MOE_BLOCK - 1) + MOE_BLOCK - 1) // MOE_BLOCK * MOE_BLOCK
    nblk = n_slots // MOE_BLOCK
    counts = counts[:, 0].astype(jnp.int32)
    padded = (counts + MOE_BLOCK - 1) // MOE_BLOCK * MOE_BLOCK
    pad_end = jnp.cumsum(padded)
    pad_start = pad_end - padded
    dest = _dest_call(idx, rank, pad_start)
    assign = (jnp.arange(n_tok, dtype=jnp.int32)[None, :] * TOP_K
              + jnp.arange(TOP_K, dtype=jnp.int32)[:, None])
    pad_a = n_assign + jnp.arange(n_slots, dtype=jnp.int32) % (2 * MOE_BLOCK)
    slot_a = pad_a.at[dest.reshape(-1)].set(assign.reshape(-1), unique_indices=True)
    t_pad = n_tok + PAD_ROWS
    tok = lax.shift_right_logical(slot_a, K_SHIFT)
    tok4 = jnp.minimum(tok, n_tok - 1) * PSLAB
    dst4 = ((slot_a & (TOP_K - 1)) * t_pad + tok) * PSLAB
    table = jnp.concatenate([tok4.reshape(nblk, MOE_BLOCK), dst4.reshape(nblk, MOE_BLOCK)], axis=1)
    total = (pad_end[-1] // MOE_BLOCK).astype(jnp.int32).reshape(1)
    return pad_start // MOE_BLOCK, padded // MOE_BLOCK, total, table


def _rope_tables(s, ntok):
    rows_n = s // GRID_W
    row = jnp.repeat(jnp.arange(rows_n, dtype=F32), GRID_W)
    col = jnp.tile(jnp.arange(GRID_W, dtype=F32), rows_n)
    axis_dim = HEAD_DIM // 2
    inv = jnp.power(ROPE_THETA, -jnp.arange(0, axis_dim, 2, dtype=F32) / axis_dim)
    ar = row[:, None] * inv[None]
    ac = col[:, None] * inv[None]
    ang = jnp.concatenate([ar, ar, ac, ac], -1)
    cos, sin = jnp.cos(ang), jnp.sin(ang)
    quarter = (jnp.arange(HEAD_DIM) // 16) % 2
    s_up = jnp.where(quarter == 0, -sin, 0.0)
    s_dn = jnp.where(quarter == 1, sin, 0.0)
    nctx = ntok - s

    def expand(t, ctx_val):
        t = jnp.concatenate([jnp.full((nctx, HEAD_DIM), ctx_val, F32), t], axis=0)
        return jnp.tile(t, (1, LANES // HEAD_DIM))

    return expand(cos, 1.0), expand(s_up, 0.0), expand(s_dn, 0.0)


def _head_mean_matrix(width):
    hid = np.arange(width) // HEAD_DIM
    return jnp.asarray((hid[:, None] == hid[None, :]).astype(np.float32) / HEAD_DIM).astype(BF16)


def _dup_heads(a, n_heads):
    parts = []
    for hd in range(n_heads):
        p = a[..., hd * HEAD_DIM:(hd + 1) * HEAD_DIM]
        parts += [p, p]
    return jnp.concatenate(parts, axis=-1)


def kernel(x, c, ctx, c_ctx, w_mod, b_mod, w_in, qn_a, kn_a, lam_q1, lam_k1, lam_q2, lam_k2, subln_c, w_br_a, w_br_b, w_br_c, w_out, ln1_g, ln1_b, w_router, b_router, w_sh_gate, w_sh_up, w_sh_down, w_e_gate, w_e_up, w_e_down, ln2_g, ln2_b):
    nb, s, d = x.shape
    lc = ctx.shape[1]
    depth = w_mod.shape[0]
    assert d == D_MODEL and lc == CTX_LEN and s % ROW_TILE == 0 and s % GRID_W == 0
    ntok = lc + s
    alpha = (2 * depth) ** 0.25

    tabs = _rope_tables(s, ntok)
    pos_lat = _dft_tables(s)
    pos_ctx = _dft_tables(lc)
    chan = _channel_table()
    e_q = _head_mean_matrix(A_Q_W)
    e_k = _head_mean_matrix(2 * A_KV_W)

    cc = jnp.concatenate([c, c_ctx[None, :]], axis=0)
    cc = jnp.pad(cc, ((0, (-(nb + 1)) % 8), (0, 0)))
    h_all = jnp.concatenate([ctx, x], axis=1).reshape(nb * ntok, d)

    offs = np.cumsum([0, A_Q_W, A_KV_W, A_KV_W, B_W, C_QK_W, C_QK_W, C_V_W, GATE_W])
    out = None
    for l in range(depth):
        last = l == depth - 1
        mod = _mod_call(cc, w_mod[l], b_mod[l])[:nb + 1].reshape(nb + 1, 6, d)
        mod = jnp.pad(mod, ((0, 0), (0, 2), (0, 0)))
        lam_init = 0.8 - 0.6 * math.exp(-0.3 * l)
        lam = (jnp.exp(jnp.sum(lam_q1[l].astype(F32) * lam_k1[l].astype(F32)))
               - jnp.exp(jnp.sum(lam_q2[l].astype(F32) * lam_k2[l].astype(F32)))) + lam_init
        lam = lam.reshape(1, 1).astype(F32)

        wl = w_in[l]
        seg = [wl[:, offs[i]:offs[i + 1]] for i in range(8)]
        in_wts = (seg[0].astype(BF16), _dup_heads(seg[1], A_KV_HEADS).astype(BF16),
                  _dup_heads(seg[2], A_KV_HEADS).astype(BF16), seg[3].astype(BF16),
                  seg[4].astype(BF16), seg[5].astype(BF16), seg[6].astype(BF16),
                  jnp.tile(qn_a[l].astype(F32), A_Q_HEADS).reshape(1, A_Q_W),
                  jnp.tile(kn_a[l].astype(F32), 2 * A_KV_HEADS).reshape(1, 2 * A_KV_W),
                  e_q, e_k)
        qa, ka, va, fb, qc, kc, vc = _inproj_call(h_all, mod, tabs, in_wts, nb, ntok)

        first_tile = 1 if last else 0
        r3 = lambda a: a.reshape(nb, ntok, a.shape[-1])
        oa = _attn_a_call(r3(qa), r3(ka), r3(va), nb, ntok, first_tile)
        oc = _attn_c_call(lam, r3(qc), r3(kc), r3(vc), subln_c[l].astype(F32).reshape(1, C_V_DIM),
                          nb, ntok, first_tile, 1.0 - lam_init)
        ob = _fourier_call(r3(fb), chan, pos_lat, pos_ctx, nb, ntok, first_tile)

        mix_wts = (seg[7].astype(BF16), w_br_a[l].astype(BF16), w_br_b[l].astype(BF16),
                   w_br_c[l].astype(BF16), w_out[l].astype(BF16),
                   ln1_g[l].astype(F32).reshape(1, d), ln1_b[l].astype(F32).reshape(1, d),
                   w_router[l].T.astype(BF16))
        flat = lambda a: a.reshape(-1, a.shape[-1])
        h1, xp, logits = _mixout_call(h_all, mod, flat(oa), flat(ob), flat(oc), mix_wts,
                                      nb, ntok, last, alpha)

        n_tok = h1.shape[0]
        idx, rw, rank, counts = _route_call(logits, b_router[l])
        first_blk, n_blk, total, table = _dispatch(idx, rank, counts, n_tok)
        y8 = _moe_call(first_blk, n_blk, total, table, xp,
                       w_e_gate, w_e_up, w_e_down, l, n_tok)
        y8 = y8.reshape(TOP_K, (n_tok + PAD_ROWS) * PSLAB, LANES)

        comb_wts = (w_sh_gate[l].astype(BF16), w_sh_up[l].astype(BF16), w_sh_down[l].astype(BF16),
                    ln2_g[l].astype(F32).reshape(1, d), ln2_b[l].astype(F32).reshape(1, d))
        h2 = _combine_call(h1, mod, y8, rw, comb_wts, s if last else ntok, not last, nb, alpha)
        if last:
            out = h2.reshape(nb, s, d)
        else:
            h_all = h2
    return out
```

```python
import functools
import math

import numpy as np
import jax
import jax.numpy as jnp
from jax import lax
from jax.experimental import pallas as pl
from jax.experimental.pallas import tpu as pltpu

F32 = jnp.float32
BF16 = jnp.bfloat16
U32 = jnp.uint32

D_MODEL = 1024
CTX_LEN = 256
GRID_W = 64
HEAD_DIM = 64
ROPE_THETA = 10000.0
A_Q_HEADS = 8
A_KV_HEADS = 2
A_Q_W = A_Q_HEADS * HEAD_DIM
A_KV_W = A_KV_HEADS * HEAD_DIM
F_GROUPS = 4
F_GROUP_W = 128
B_W = F_GROUPS * F_GROUP_W
C_HEADS = 4
C_V_DIM = 2 * HEAD_DIM
C_QK_W = C_HEADS * 2 * HEAD_DIM
C_V_W = C_HEADS * C_V_DIM
N_BRANCH = 3
GATE_W = N_BRANCH * D_MODEL
N_EXPERTS = 256
TOP_K = 8
N_GROUPS = 8
TOPK_GROUPS = 4
EXPERT_FF = 256
SHARED_FF = 256
ROUTED_SCALE = 2.5
LN_EPS = 1e-5
RMS_EPS = 1e-6
K_SHIFT = TOP_K.bit_length() - 1
assert 1 << K_SHIFT == TOP_K

LANES = 128
PSLAB = D_MODEL // (2 * LANES)
ROW_TILE = 256
MOE_BLOCK = 128
BLOCKS_IN_FLIGHT = 3
PAD_ROWS = BLOCKS_IN_FLIGHT * MOE_BLOCK // TOP_K
COMBINE_TILE = 128
VMEM_LIMIT = 56 * 1024 * 1024

_Q_SCALE = HEAD_DIM ** -0.5 * math.log2(math.e)


def _cparams(sem):
    return pltpu.CompilerParams(dimension_semantics=sem, vmem_limit_bytes=VMEM_LIMIT)


def _dot(a, b):
    return jnp.dot(a, b, preferred_element_type=F32)


def _dot_nt(a, b):
    return lax.dot_general(a, b, (((1,), (1,)), ((), ())), preferred_element_type=F32)


def _layer_norm_rows(x):
    mu = jnp.mean(x, axis=-1, keepdims=True)
    xc = x - mu
    var = jnp.mean(xc * xc, axis=-1, keepdims=True)
    return xc * lax.rsqrt(var + LN_EPS)


def _dot_split(x, e):
    hi = x.astype(BF16)
    lo = (x - hi.astype(F32)).astype(BF16)
    return _dot(hi, e) + _dot(lo, e)


def _mod_kernel(c_ref, w_ref, b_ref, o_ref):
    c = c_ref[...]
    sc = c * jax.nn.sigmoid(c)
    o_ref[...] = _dot(sc.astype(BF16), w_ref[...].astype(BF16)) + b_ref[...]


def _mod_call(cc, w_mod_l, b_mod_l):
    r = cc.shape[0]
    n = w_mod_l.shape[1]
    tn = D_MODEL
    return pl.pallas_call(
        _mod_kernel,
        out_shape=jax.ShapeDtypeStruct((r, n), F32),
        grid=(n // tn,),
        in_specs=[
            pl.BlockSpec((r, D_MODEL), lambda j: (0, 0)),
            pl.BlockSpec((D_MODEL, tn), lambda j: (0, j)),
            pl.BlockSpec((1, tn), lambda j: (0, j)),
        ],
        out_specs=pl.BlockSpec((r, tn), lambda j: (0, j)),
        compiler_params=_cparams(("arbitrary",)),
        name="mod_vectors",
    )(cc, w_mod_l, b_mod_l.reshape(1, n))


def _rope_cols(x, cos, sin_up, sin_dn):
    cols = []
    for c in range(x.shape[1] // LANES):
        xc = x[:, c * LANES:(c + 1) * LANES]
        up = pltpu.roll(xc, LANES - 16, axis=1)
        dn = pltpu.roll(xc, 16, axis=1)
        cols.append(xc * cos + up * sin_up + dn * sin_dn)
    return jnp.concatenate(cols, axis=1) if len(cols) > 1 else cols[0]


def _inproj_kernel(h_ref, mod_ref, cos_ref, su_ref, sd_ref,
                   wq_ref, wk_ref, wv_ref, wf_ref, wqc_ref, wkc_ref, wvc_ref,
                   qn_ref, kn_ref, eq_ref, ek_ref,
                   qa_o, ka_o, va_o, fb_o, qc_o, kc_o, vc_o):
    h = h_ref[...]
    shift = mod_ref[0, 0:1, :]
    scale = mod_ref[0, 1:2, :]
    u = (_layer_norm_rows(h) * (1.0 + scale) + shift).astype(BF16)
    cos = cos_ref[...]
    s_up = su_ref[...]
    s_dn = sd_ref[...]

    q = _dot(u, wq_ref[...])
    ms = _dot_split(q * q, eq_ref[...])
    q = q * lax.rsqrt(ms + RMS_EPS) * qn_ref[...]
    qa_o[...] = (_rope_cols(q, cos, s_up, s_dn) * _Q_SCALE).astype(BF16)

    k = _dot(u, wk_ref[...])
    ms = _dot_split(k * k, ek_ref[...])
    k = k * lax.rsqrt(ms + RMS_EPS) * kn_ref[...]
    ka_o[0] = jnp.transpose(_rope_cols(k, cos, s_up, s_dn)).astype(BF16)

    va_o[...] = _dot(u, wv_ref[...]).astype(BF16)
    fb_o[...] = _dot(u, wf_ref[...]).astype(BF16)
    qc = _dot(u, wqc_ref[...])
    qc_o[...] = (_rope_cols(qc, cos, s_up, s_dn) * _Q_SCALE).astype(BF16)
    kc = _dot(u, wkc_ref[...])
    kc_o[0] = jnp.transpose(_rope_cols(kc, cos, s_up, s_dn)).astype(BF16)
    vc_o[...] = _dot(u, wvc_ref[...]).astype(BF16)


def _inproj_call(h_all, mod, tabs, wts, nb, ntok):
    rows = h_all.shape[0]
    tpb = ntok // ROW_TILE
    cos, s_up, s_dn = tabs

    def full(a):
        return pl.BlockSpec(a.shape, lambda i: (0,) * a.ndim)

    def rowspec(w):
        return pl.BlockSpec((ROW_TILE, w), lambda i: (i, 0))

    tabspec = pl.BlockSpec((ROW_TILE, LANES), lambda i: (i % tpb, 0))
    modspec = pl.BlockSpec((1, 8, D_MODEL),
                           lambda i: (jnp.where(i % tpb == 0, nb, i // tpb), 0, 0))
    widths = (A_Q_W, 2 * A_KV_W, 2 * A_KV_W, B_W, C_QK_W, C_QK_W, C_V_W)
    transposed = (1, 5)

    def oshape(n, w):
        return (nb, w, ntok) if n in transposed else (rows, w)

    def ospec(n, w):
        if n in transposed:
            return pl.BlockSpec((1, w, ROW_TILE), lambda i: (i // tpb, 0, i % tpb))
        return rowspec(w)

    return pl.pallas_call(
        _inproj_kernel,
        out_shape=[jax.ShapeDtypeStruct(oshape(n, w), BF16) for n, w in enumerate(widths)],
        grid=(rows // ROW_TILE,),
        in_specs=[rowspec(D_MODEL), modspec, tabspec, tabspec, tabspec]
                 + [full(w) for w in wts],
        out_specs=[ospec(n, w) for n, w in enumerate(widths)],
        compiler_params=_cparams(("parallel",)),
        name="in_projection",
    )(h_all, mod, cos, s_up, s_dn, *wts)


def _softmax_parts(s):
    m = jnp.max(s, axis=-1, keepdims=True)
    e = jnp.exp2(s - m)
    return e, jnp.sum(e, axis=-1, keepdims=True)


def _gqa_tile(q_ref, k_ref, v_ref, o_ref, nk):
    lane = lax.broadcasted_iota(jnp.int32, (1, LANES), 1)
    low = lane < HEAD_DIM
    for c in range(A_Q_W // LANES):
        kvh = (2 * c) // (A_Q_HEADS // A_KV_HEADS)
        qc = q_ref[0, :, c * LANES:(c + 1) * LANES]
        kk = k_ref[0, kvh * LANES:(kvh + 1) * LANES, 0:nk]
        vv = v_ref[0, 0:nk, kvh * LANES:(kvh + 1) * LANES]
        halves = []
        for keep in (low, jnp.logical_not(low)):
            qm = jnp.where(keep, qc, jnp.zeros_like(qc))
            e, l = _softmax_parts(_dot(qm, kk))
            halves.append(_dot(e.astype(BF16), vv) * (1.0 / l))
        o_ref[0, :, c * LANES:(c + 1) * LANES] = jnp.where(low, halves[0], halves[1]).astype(BF16)


def _attn_a_kernel(q_ref, k_ref, v_ref, o_ref, *, ntok, first_tile):
    j = pl.program_id(1) + first_tile
    if first_tile == 0:
        @pl.when(j == 0)
        def _():
            _gqa_tile(q_ref, k_ref, v_ref, o_ref, CTX_LEN)

        @pl.when(j != 0)
        def _():
            _gqa_tile(q_ref, k_ref, v_ref, o_ref, ntok)
    else:
        _gqa_tile(q_ref, k_ref, v_ref, o_ref, ntok)


def _diff_tile(lam_ref, q_ref, k_ref, v_ref, g_ref, o_ref, nk, out_scale):
    lane = lax.broadcasted_iota(jnp.int32, (1, LANES), 1)
    low = lane < HEAD_DIM
    lam = lam_ref[0, 0]
    for hd in range(C_HEADS):
        sl = slice(hd * LANES, (hd + 1) * LANES)
        qc = q_ref[0, :, sl]
        kk = k_ref[0, sl, 0:nk]
        vv = v_ref[0, 0:nk, sl]
        q1 = jnp.where(low, qc, jnp.zeros_like(qc))
        q2 = jnp.where(low, jnp.zeros_like(qc), qc)
        e1, l1 = _softmax_parts(_dot(q1, kk))
        e2, l2 = _softmax_parts(_dot(q2, kk))
        o = _dot(e1.astype(BF16), vv) * (1.0 / l1) - _dot(e2.astype(BF16), vv) * (lam / l2)
        ms = jnp.mean(o * o, axis=-1, keepdims=True)
        o = o * lax.rsqrt(ms + RMS_EPS) * g_ref[...] * out_scale
        o_ref[0, :, sl] = o.astype(BF16)


def _attn_c_kernel(lam_ref, q_ref, k_ref, v_ref, g_ref, o_ref, *, ntok, first_tile, out_scale):
    j = pl.program_id(1) + first_tile
    if first_tile == 0:
        @pl.when(j == 0)
        def _():
            _diff_tile(lam_ref, q_ref, k_ref, v_ref, g_ref, o_ref, CTX_LEN, out_scale)

        @pl.when(j != 0)
        def _():
            _diff_tile(lam_ref, q_ref, k_ref, v_ref, g_ref, o_ref, ntok, out_scale)
    else:
        _diff_tile(lam_ref, q_ref, k_ref, v_ref, g_ref, o_ref, ntok, out_scale)


def _attn_specs(nb, ntok, first_tile, wq, wkv):
    tpb = ntok // ROW_TILE
    grid = (nb, tpb - first_tile)
    qspec = pl.BlockSpec((1, ROW_TILE, wq), lambda b, j: (b, j + first_tile, 0))
    kspec = pl.BlockSpec((1, wkv, ntok), lambda b, j: (b, 0, 0))
    vspec = pl.BlockSpec((1, ntok, wkv), lambda b, j: (b, 0, 0))
    ospec = pl.BlockSpec((1, ROW_TILE, wq), lambda b, j: (b, j, 0))
    out_rows = ntok - first_tile * ROW_TILE
    return grid, qspec, kspec, vspec, ospec, out_rows


def _attn_a_call(qa, ka, va, nb, ntok, first_tile):
    grid, qspec, kspec, vspec, ospec, out_rows = _attn_specs(nb, ntok, first_tile, A_Q_W, 2 * A_KV_W)
    return pl.pallas_call(
        functools.partial(_attn_a_kernel, ntok=ntok, first_tile=first_tile),
        out_shape=jax.ShapeDtypeStruct((nb, out_rows, A_Q_W), BF16),
        grid=grid,
        in_specs=[qspec, kspec, vspec],
        out_specs=ospec,
        compiler_params=_cparams(("parallel", "arbitrary")),
        name="gqa_attention",
    )(qa, ka, va)


def _attn_c_call(lam, qc, kc, vc, subln, nb, ntok, first_tile, out_scale):
    grid, qspec, kspec, vspec, ospec, out_rows = _attn_specs(nb, ntok, first_tile, C_QK_W, C_QK_W)
    return pl.pallas_call(
        functools.partial(_attn_c_kernel, ntok=ntok, first_tile=first_tile, out_scale=out_scale),
        out_shape=jax.ShapeDtypeStruct((nb, out_rows, C_V_W), BF16),
        grid=grid,
        in_specs=[pl.BlockSpec(memory_space=pltpu.SMEM), qspec, kspec, vspec,
                  pl.BlockSpec((1, C_V_DIM), lambda b, j: (0, 0))],
        out_specs=ospec,
        compiler_params=_cparams(("parallel", "arbitrary")),
        name="diff_attention",
    )(lam, qc, kc, vc, subln)


def _fourier_kernel(f_ref, chan_ref, pos_ref, posc_ref, o_ref, g_ref, *, ntok, first_tile):
    j = pl.program_id(1) + first_tile
    nlat = ntok - CTX_LEN

    def channel_stage(rows):
        g = _dot(rows, chan_ref[...])
        return jnp.concatenate([g[:, :B_W], g[:, B_W:]], axis=0).astype(BF16)

    if first_tile == 0:
        @pl.when(j == 0)
        def _():
            gc = channel_stage(f_ref[0, 0:CTX_LEN, :])
            y = _dot(posc_ref[...], gc) * (1.0 / math.sqrt(CTX_LEN * F_GROUP_W))
            o_ref[0] = y.astype(BF16)

    @pl.when(j == 1)
    def _():
        g_ref[...] = channel_stage(f_ref[0, CTX_LEN:ntok, :])

    @pl.when(j >= 1)
    def _():
        y = _dot(pos_ref[...], g_ref[...]) * (1.0 / math.sqrt(nlat * F_GROUP_W))
        o_ref[0] = y.astype(BF16)


def _fourier_call(fb, chan, pos, posc, nb, ntok, first_tile):
    tpb = ntok // ROW_TILE
    nlat = ntok - CTX_LEN
    return pl.pallas_call(
        functools.partial(_fourier_kernel, ntok=ntok, first_tile=first_tile),
        out_shape=jax.ShapeDtypeStruct((nb, ntok - first_tile * ROW_TILE, B_W), BF16),
        grid=(nb, tpb - first_tile),
        in_specs=[
            pl.BlockSpec((1, ntok, B_W), lambda b, j: (b, 0, 0)),
            pl.BlockSpec(chan.shape, lambda b, j: (0, 0)),
            pl.BlockSpec((ROW_TILE, 2 * nlat),
                         lambda b, j: (jnp.maximum(j + first_tile - 1, 0), 0)),
            pl.BlockSpec(posc.shape, lambda b, j: (0, 0)),
        ],
        out_specs=pl.BlockSpec((1, ROW_TILE, B_W), lambda b, j: (b, j, 0)),
        scratch_shapes=[pltpu.VMEM((2 * nlat, B_W), BF16)],
        compiler_params=_cparams(("parallel", "arbitrary")),
        name="fourier_mix",
    )(fb, chan, pos, posc)


def _dft_tables(n):
    n1 = 32
    n0 = n // n1
    k = np.arange(n, dtype=np.int64)
    a = 2.0 * np.pi * ((k[:, None] * np.arange(n1)[None, :] * n0) % n) / n
    b = 2.0 * np.pi * ((k[:, None] * np.arange(n0)[None, :]) % n) / n
    ca, sa = jnp.asarray(np.cos(a), F32)[:, :, None], jnp.asarray(np.sin(a), F32)[:, :, None]
    cb, sb = jnp.asarray(np.cos(b), F32)[:, None, :], jnp.asarray(np.sin(b), F32)[:, None, :]
    cos = (ca * cb - sa * sb).reshape(n, n)
    sin = (sa * cb + ca * sb).reshape(n, n)
    return jnp.concatenate([cos, -sin], axis=1).astype(BF16)


def _channel_table():
    c = np.arange(F_GROUP_W)
    ang = 2.0 * np.pi * ((c[:, None] * c[None, :]) % F_GROUP_W) / F_GROUP_W
    eye = np.eye(F_GROUPS)
    cos = np.kron(eye, np.cos(ang))
    sin = np.kron(eye, np.sin(ang))
    return jnp.asarray(np.concatenate([cos, sin], axis=1), F32).astype(BF16)


def _store_packed_slabs(dst_ref, u):
    rows = u.shape[0]
    for jb in range(PSLAB):
        lo = u[:, 2 * jb * LANES:(2 * jb + 1) * LANES]
        hi = u[:, (2 * jb + 1) * LANES:(2 * jb + 2) * LANES]
        dst_ref[pl.ds(jb, rows, stride=PSLAB), :] = pltpu.pack_elementwise([lo, hi], packed_dtype=BF16)


def _load_packed_slabs(src_ref, rows):
    out = []
    for jb in range(PSLAB):
        words = src_ref[pl.ds(jb, rows, stride=PSLAB), :]
        out.append((pltpu.unpack_elementwise(words, index=0, packed_dtype=BF16, unpacked_dtype=F32),
                    pltpu.unpack_elementwise(words, index=1, packed_dtype=BF16, unpacked_dtype=F32)))
    return out


def _mixout_kernel(h_ref, mod_ref, oa_ref, ob_ref, oc_ref,
                   wg_ref, wa_ref, wb_ref, wc_ref, wo_ref, g1_ref, b1_ref, wr_ref,
                   h1_o, xp_o, lg_o, *, alpha):
    h = h_ref[...]
    m = mod_ref[0]
    u = (_layer_norm_rows(h) * (1.0 + m[1:2, :]) + m[0:1, :]).astype(BF16)
    y = None
    for n, (o_ref, w_ref) in enumerate(((oa_ref, wa_ref), (ob_ref, wb_ref), (oc_ref, wc_ref))):
        gate = jax.nn.sigmoid(_dot(u, wg_ref[:, n * D_MODEL:(n + 1) * D_MODEL]))
        t = gate * _dot(o_ref[...], w_ref[...])
        y = t if y is None else y + t
    z = _dot(y.astype(BF16), wo_ref[...])
    h1 = _layer_norm_rows(alpha * h + m[2:3, :] * z) * g1_ref[...] + b1_ref[...]
    h1_o[...] = h1
    u2 = _layer_norm_rows(h1) * (1.0 + m[4:5, :]) + m[3:4, :]
    _store_packed_slabs(xp_o, u2)
    lg_o[...] = _dot_nt(wr_ref[...], u2.astype(BF16))


def _tile_maps(nb, ntok, lat_only):
    tpb = ntok // ROW_TILE
    if lat_only:
        lpb = tpb - 1
        n_tiles = nb * lpb
        src = lambda i: (i // lpb) * tpb + 1 + i % lpb
        modi = lambda i: i // lpb
    else:
        n_tiles = nb * tpb
        src = lambda i: i
        modi = lambda i: jnp.where(i % tpb == 0, nb, i // tpb)
    return n_tiles, src, modi


def _mixout_call(h_all, mod, oa, ob, oc, wts, nb, ntok, lat_only, alpha):
    n_tiles, src, modi = _tile_maps(nb, ntok, lat_only)
    rows_out = n_tiles * ROW_TILE

    def full(a):
        return pl.BlockSpec(a.shape, lambda i: (0,) * a.ndim)

    def inrow(w):
        return pl.BlockSpec((ROW_TILE, w), lambda i: (src(i), 0))

    def outrow(w):
        return pl.BlockSpec((ROW_TILE, w), lambda i: (i, 0))

    modspec = pl.BlockSpec((1, 8, D_MODEL), lambda i: (modi(i), 0, 0))
    return pl.pallas_call(
        functools.partial(_mixout_kernel, alpha=alpha),
        out_shape=[jax.ShapeDtypeStruct((rows_out, D_MODEL), F32),
                   jax.ShapeDtypeStruct((rows_out * PSLAB, LANES), U32),
                   jax.ShapeDtypeStruct((N_EXPERTS, rows_out), F32)],
        grid=(n_tiles,),
        in_specs=[inrow(D_MODEL), modspec, outrow(A_Q_W), outrow(B_W), outrow(C_V_W)]
                 + [full(w) for w in wts],
        out_specs=[outrow(D_MODEL), pl.BlockSpec((ROW_TILE * PSLAB, LANES), lambda i: (i, 0)),
                   pl.BlockSpec((N_EXPERTS, ROW_TILE), lambda i: (0, i))],
        compiler_params=_cparams(("parallel",)),
        name="mixer_output",
    )(h_all, mod, oa, ob, oc, *wts)


def _moe_kernel(first_ref, count_ref, total_ref,
                tab_hbm, xp_ref, wg_ref, wu_ref, wd_ref,
                out_hbm,
                tab, gath, wgu_s, wd_s, ybuf, sem, tsem, *, t_pad, n_tok, lead_row):
    e = pl.program_id(0)
    total = total_ref[0]

    def row_copy(s, m, dst4):
        return pltpu.make_async_copy(ybuf.at[s, pl.ds(PSLAB * m, PSLAB), :],
                                     out_hbm.at[pl.ds(pl.multiple_of(dst4, PSLAB), PSLAB), :],
                                     sem.at[s])

    def wait_rows(s):
        for m in range(MOE_BLOCK):
            row_copy(s, m, 0).wait()

    def send_rows(ts, ys):
        for m in range(MOE_BLOCK):
            row_copy(ys, m, tab[ts, 0, MOE_BLOCK + m]).start(priority=m % 2)

    def tab_copy(row, s):
        return pltpu.make_async_copy(tab_hbm.at[row], tab.at[s], tsem.at[s])

    def gather_rows(ts, gs):
        for m in range(MOE_BLOCK):
            t4 = pl.multiple_of(tab[ts, 0, m], PSLAB)
            gath[gs, PSLAB * m:PSLAB * (m + 1), :] = xp_ref[pl.ds(t4, PSLAB), :]

    @pl.when(e == 0)
    def _():
        ybuf[...] = jnp.zeros_like(ybuf)
        tab_copy(lead_row, 3).start()
        tab_copy(0, 0).start()
        tab_copy(1, 1).start()
        for b in range(2):
            for m in range(MOE_BLOCK):
                q = b * MOE_BLOCK + m
                row_copy(b, m, ((q % TOP_K) * t_pad + n_tok + q // TOP_K) * PSLAB).start(priority=m % 2)
        tab_copy(lead_row, 3).wait()
        tab_copy(0, 0).wait()
        gather_rows(0, 0)

    wgu_s[:, 0:EXPERT_FF] = wg_ref[0, 0].astype(BF16)
    wgu_s[:, EXPERT_FF:2 * EXPERT_FF] = wu_ref[0, 0].astype(BF16)
    wd_s[...] = wd_ref[0, 0].astype(BF16)

    def block(j, carry):
        g = first_ref[e] + j
        tab_copy(g + 1, (g + 1) & 3).wait()
        tab_copy(g + 2, (g + 2) & 3).start()
        ys = lax.rem(g, 3)
        wait_rows(ys)

        send_rows((g + 3) & 3, lax.rem(g + 2, 3))
        gather_rows((g + 1) & 3, (g + 1) & 1)

        acc = None
        for jb, (lo, hi) in enumerate(_load_packed_slabs(gath.at[g & 1], MOE_BLOCK)):
            a = jnp.concatenate([lo, hi], axis=1).astype(BF16)
            t = _dot(a, wgu_s[jb * 2 * LANES:(jb + 1) * 2 * LANES, :])
            acc = t if acc is None else acc + t
        gt = acc[:, 0:EXPERT_FF]
        act = (gt * jax.nn.sigmoid(gt)) * acc[:, EXPERT_FF:2 * EXPERT_FF]
        y = _dot(act.astype(BF16), wd_s[...])
        _store_packed_slabs(ybuf.at[ys], y)
        return carry

    lax.fori_loop(0, count_ref[e], block, 0)

    @pl.when(e == pl.num_programs(0) - 1)
    def _():
        send_rows((total + 3) & 3, lax.rem(total + 2, 3))
        for s in range(3):
            wait_rows(s)
        tab_copy(0, (total + 1) & 3).wait()


def _moe_call(first_blk, n_blk, total, table, xp4, wg, wu, wd, layer, n_tok):
    t_pad = n_tok + PAD_ROWS
    lead_row = table.shape[0] - 1
    wspec = lambda r, c: pl.BlockSpec((1, 1, r, c), lambda e, fb, nb, tt: (layer, e, 0, 0))
    grid_spec = pltpu.PrefetchScalarGridSpec(
        num_scalar_prefetch=3,
        grid=(N_EXPERTS,),
        in_specs=[
            pl.BlockSpec(memory_space=pl.ANY),
            pl.BlockSpec(memory_space=pltpu.VMEM),
            wspec(D_MODEL, EXPERT_FF), wspec(D_MODEL, EXPERT_FF), wspec(EXPERT_FF, D_MODEL),
        ],
        out_specs=pl.BlockSpec(memory_space=pl.ANY),
        scratch_shapes=[
            pltpu.SMEM((4, 1, 2 * MOE_BLOCK), jnp.int32),
            pltpu.VMEM((2, MOE_BLOCK * PSLAB, LANES), U32),
            pltpu.VMEM((D_MODEL, 2 * EXPERT_FF), BF16),
            pltpu.VMEM((EXPERT_FF, D_MODEL), BF16),
            pltpu.VMEM((3, MOE_BLOCK * PSLAB, LANES), U32),
            pltpu.SemaphoreType.DMA((3,)),
            pltpu.SemaphoreType.DMA((4,)),
        ],
    )
    return pl.pallas_call(
        functools.partial(_moe_kernel, t_pad=t_pad, n_tok=n_tok, lead_row=lead_row),
        out_shape=jax.ShapeDtypeStruct((TOP_K * t_pad * PSLAB, LANES), U32),
        grid_spec=grid_spec,
        compiler_params=_cparams(("arbitrary",)),
        name="routed_experts",
    )(first_blk, n_blk, total, table, xp4, wg, wu, wd)


def _combine_kernel(h1_ref, mod_ref, y8_ref, rw_ref, sg_ref, su_ref, sd_ref, g2_ref, b2_ref, o_ref,
                    *, alpha):
    h1 = h1_ref[...]
    m = mod_ref[0]
    u2 = (_layer_norm_rows(h1) * (1.0 + m[4:5, :]) + m[3:4, :]).astype(BF16)
    g = _dot(u2, sg_ref[...])
    act = (g * jax.nn.sigmoid(g)) * _dot(u2, su_ref[...])
    f = _dot(act.astype(BF16), sd_ref[...])
    tm = h1.shape[0]
    rw = rw_ref[...]
    wk = [jnp.broadcast_to(rw[:, kk:kk + 1], (tm, LANES)) for kk in range(TOP_K)]
    cols = [None] * (2 * PSLAB)
    for kk in range(TOP_K):
        for jb, pair in enumerate(_load_packed_slabs(y8_ref.at[kk], tm)):
            for half in range(2):
                t = wk[kk] * pair[half]
                c = 2 * jb + half
                cols[c] = t if cols[c] is None else cols[c] + t
    f = f + jnp.concatenate(cols, axis=1)
    o_ref[...] = _layer_norm_rows(alpha * h1 + m[5:6, :] * f) * g2_ref[...] + b2_ref[...]


def _combine_call(h1, mod, y8, rw, wts, rows_per_group, group_is_ctx_first, nb, alpha):
    rows = h1.shape[0]
    tm = COMBINE_TILE
    tpg = rows_per_group // tm
    if group_is_ctx_first:
        first = CTX_LEN // tm
        modi = lambda i: jnp.where(i % tpg < first, nb, i // tpg)
    else:
        modi = lambda i: i // tpg

    def full(a):
        return pl.BlockSpec(a.shape, lambda i: (0,) * a.ndim)

    return pl.pallas_call(
        functools.partial(_combine_kernel, alpha=alpha),
        out_shape=jax.ShapeDtypeStruct((rows, D_MODEL), F32),
        grid=(rows // tm,),
        in_specs=[pl.BlockSpec((tm, D_MODEL), lambda i: (i, 0)),
                  pl.BlockSpec((1, 8, D_MODEL), lambda i: (modi(i), 0, 0)),
                  pl.BlockSpec((TOP_K, tm * PSLAB, LANES), lambda i: (0, i, 0)),
                  pl.BlockSpec((tm, TOP_K), lambda i: (i, 0))]
                 + [full(w) for w in wts],
        out_specs=pl.BlockSpec((tm, D_MODEL), lambda i: (i, 0)),
        compiler_params=_cparams(("parallel",)),
        name="moe_combine",
    )(h1, mod, y8, rw, *wts)


_BIG_LANE = 1 << 30


def _route_kernel(lg_ref, b_ref, idx_o, w_o, rank_o, cnt_o, carry):
    i = pl.program_id(0)

    @pl.when(i == 0)
    def _():
        carry[...] = jnp.zeros_like(carry)

    tm = lg_ref.shape[1]
    gsize = N_EXPERTS // N_GROUPS
    scores = jax.nn.sigmoid(lg_ref[...])
    biased = scores + b_ref[...]
    eid = lax.broadcasted_iota(jnp.int32, (N_EXPERTS, tm), 0)

    b3 = biased.reshape(N_GROUPS, gsize, tm)
    in_g = lax.broadcasted_iota(jnp.int32, (N_GROUPS, gsize, tm), 1)
    m1 = jnp.max(b3, axis=1, keepdims=True)
    first = jnp.min(jnp.where(b3 == m1, in_g, _BIG_LANE), axis=1, keepdims=True)
    m2 = jnp.max(jnp.where(in_g == first, -jnp.inf, b3), axis=1, keepdims=True)
    gscore = (m1 + m2).reshape(N_GROUPS, tm)

    gid = lax.broadcasted_iota(jnp.int32, (N_GROUPS, tm), 0)
    beaten = jnp.zeros((N_GROUPS, tm), jnp.int32)
    for g in range(N_GROUPS):
        sg = gscore[g:g + 1, :]
        ahead = jnp.logical_or(sg > gscore, jnp.logical_and(sg == gscore, g < gid))
        beaten = beaten + ahead.astype(jnp.int32)
    keep = jnp.broadcast_to((beaten < TOPK_GROUPS).astype(jnp.int32).reshape(N_GROUPS, 1, tm),
                            (N_GROUPS, gsize, tm)).reshape(N_EXPERTS, tm)
    masked = jnp.where(keep > 0, biased, -jnp.inf)

    idxs, ws, hots = [], [], []
    for _ in range(TOP_K):
        m = jnp.max(masked, axis=0, keepdims=True)
        ix = jnp.min(jnp.where(masked == m, eid, _BIG_LANE), axis=0, keepdims=True)
        hot = eid == ix
        idxs.append(ix)
        ws.append(jnp.sum(jnp.where(hot, scores, 0.0), axis=0, keepdims=True))
        hots.append(hot)
        masked = jnp.where(hot, -jnp.inf, masked)
    wsum = ws[0]
    for r in range(1, TOP_K):
        wsum = wsum + ws[r]
    idx_o[...] = jnp.concatenate(idxs, axis=0)
    w_rows = jnp.concatenate([wr / wsum * ROUTED_SCALE for wr in ws], axis=0)

    eye = (lax.broadcasted_iota(jnp.int32, (tm, tm), 0)
           == lax.broadcasted_iota(jnp.int32, (tm, tm), 1)).astype(BF16)
    w_cols = jnp.zeros((tm, TOP_K), F32)
    rest = w_rows
    for _ in range(3):
        part = rest.astype(BF16)
        rest = rest - part.astype(F32)
        w_cols = w_cols + _dot_nt(eye, part)
    w_o[...] = w_cols

    sel = jnp.zeros((N_EXPERTS, tm), F32)
    for hot in hots:
        sel = sel + hot.astype(F32)
    sel = sel.astype(BF16)
    earlier = (lax.broadcasted_iota(jnp.int32, (tm, tm), 0)
               < lax.broadcasted_iota(jnp.int32, (tm, tm), 1)).astype(BF16)
    prefix = _dot(sel, earlier) + carry[:, 0:1]
    rank_o[...] = jnp.concatenate(
        [jnp.sum(jnp.where(hot, prefix, 0.0), axis=0, keepdims=True) for hot in hots],
        axis=0).astype(jnp.int32)
    carry[...] = carry[...] + _dot(sel, jnp.ones((tm, LANES), BF16))
    cnt_o[...] = carry[...]


def _route_call(logits_t, b_r):
    t = logits_t.shape[1]
    tm = ROW_TILE
    kt = pl.BlockSpec((TOP_K, tm), lambda i: (0, i))
    return pl.pallas_call(
        _route_kernel,
        out_shape=[jax.ShapeDtypeStruct((TOP_K, t), jnp.int32),
                   jax.ShapeDtypeStruct((t, TOP_K), F32),
                   jax.ShapeDtypeStruct((TOP_K, t), jnp.int32),
                   jax.ShapeDtypeStruct((N_EXPERTS, LANES), F32)],
        grid=(t // tm,),
        in_specs=[pl.BlockSpec((N_EXPERTS, tm), lambda i: (0, i)),
                  pl.BlockSpec((N_EXPERTS, 1), lambda i: (0, 0))],
        out_specs=[kt, pl.BlockSpec((tm, TOP_K), lambda i: (i, 0)), kt,
                   pl.BlockSpec((N_EXPERTS, LANES), lambda i: (0, 0))],
        scratch_shapes=[pltpu.VMEM((N_EXPERTS, LANES), F32)],
        compiler_params=_cparams(("arbitrary",)),
        name="route_topk",
    )(logits_t, b_r.astype(F32).reshape(N_EXPERTS, 1))


def _dest_kernel(idx_ref, rank_ref, start_ref, o_ref):
    tm = idx_ref.shape[1]
    eid = lax.broadcasted_iota(jnp.int32, (N_EXPERTS, tm), 0)
    idx = idx_ref[...]
    start = start_ref[...]
    rows = [jnp.sum(jnp.where(eid == idx[r:r + 1, :], start, 0), axis=0, keepdims=True)
            for r in range(TOP_K)]
    o_ref[...] = jnp.concatenate(rows, axis=0) + rank_ref[...]


def _dest_call(idx, rank, pad_start):
    t = idx.shape[1]
    tm = ROW_TILE
    blk = pl.BlockSpec((TOP_K, tm), lambda i: (0, i))
    return pl.pallas_call(
        _dest_kernel,
        out_shape=jax.ShapeDtypeStruct((TOP_K, t), jnp.int32),
        grid=(t // tm,),
        in_specs=[blk, blk, pl.BlockSpec((N_EXPERTS, 1), lambda i: (0, 0))],
        out_specs=blk,
        compiler_params=_cparams(("parallel",)),
        name="slot_of_assignment",
    )(idx, rank, pad_start.reshape(N_EXPERTS, 1))


def _dispatch(idx, rank, counts, n_tok):
    n_assign = n_tok * TOP_K
    used_max = (n_assign + N_EXPERTS * (MOE_BLOCK - 1) + MOE_BLOCK - 1) // MOE_BLOCK
    nblk = -(-(used_max + 3) // BLOCKS_IN_FLIGHT) * BLOCKS_IN_FLIGHT
    n_slots = nblk * MOE_BLOCK
    counts = counts[:, 0].astype(jnp.int32)
    padded = (counts + MOE_BLOCK - 1) // MOE_BLOCK * MOE_BLOCK
    pad_end = jnp.cumsum(padded)
    pad_start = pad_end - padded
    dest = _dest_call(idx, rank, pad_start)
    assign = (jnp.arange(n_tok, dtype=jnp.int32)[None, :] * TOP_K
              + jnp.arange(TOP_K, dtype=jnp.int32)[:, None])
    assert (nblk - 1) % BLOCKS_IN_FLIGHT == BLOCKS_IN_FLIGHT - 1
    pad_a = n_assign + jnp.arange(n_slots, dtype=jnp.int32) % (BLOCKS_IN_FLIGHT * MOE_BLOCK)
    slot_a = pad_a.at[dest.reshape(-1)].set(assign.reshape(-1), unique_indices=True)
    t_pad = n_tok + PAD_ROWS
    tok = lax.shift_right_logical(slot_a, K_SHIFT)
    tok4 = jnp.minimum(tok, n_tok - 1) * PSLAB
    dst4 = ((slot_a & (TOP_K - 1)) * t_pad + tok) * PSLAB
    table = jnp.concatenate([tok4.reshape(nblk, 1, MOE_BLOCK), dst4.reshape(nblk, 1, MOE_BLOCK)], axis=2)
    total = (pad_end[-1] // MOE_BLOCK).astype(jnp.int32).reshape(1)
    return pad_start // MOE_BLOCK, padded // MOE_BLOCK, total, table


def _rope_tables(s, ntok):
    rows_n = s // GRID_W
    row = jnp.repeat(jnp.arange(rows_n, dtype=F32), GRID_W)
    col = jnp.tile(jnp.arange(GRID_W, dtype=F32), rows_n)
    axis_dim = HEAD_DIM // 2
    inv = jnp.power(ROPE_THETA, -jnp.arange(0, axis_dim, 2, dtype=F32) / axis_dim)
    ar = row[:, None] * inv[None]
    ac = col[:, None] * inv[None]
    ang = jnp.concatenate([ar, ar, ac, ac], -1)
    cos, sin = jnp.cos(ang), jnp.sin(ang)
    quarter = (jnp.arange(HEAD_DIM) // 16) % 2
    s_up = jnp.where(quarter == 0, -sin, 0.0)
    s_dn = jnp.where(quarter == 1, sin, 0.0)
    nctx = ntok - s

    def expand(t, ctx_val):
        t = jnp.concatenate([jnp.full((nctx, HEAD_DIM), ctx_val, F32), t], axis=0)
        return jnp.tile(t, (1, LANES // HEAD_DIM))

    return expand(cos, 1.0), expand(s_up, 0.0), expand(s_dn, 0.0)


def _head_mean_matrix(width):
    hid = np.arange(width) // HEAD_DIM
    return jnp.asarray((hid[:, None] == hid[None, :]).astype(np.float32) / HEAD_DIM).astype(BF16)


def _dup_heads(a, n_heads):
    parts = []
    for hd in range(n_heads):
        p = a[..., hd * HEAD_DIM:(hd + 1) * HEAD_DIM]
        parts += [p, p]
    return jnp.concatenate(parts, axis=-1)


def kernel(x, c, ctx, c_ctx, w_mod, b_mod, w_in, qn_a, kn_a, lam_q1, lam_k1, lam_q2, lam_k2, subln_c, w_br_a, w_br_b, w_br_c, w_out, ln1_g, ln1_b, w_router, b_router, w_sh_gate, w_sh_up, w_sh_down, w_e_gate, w_e_up, w_e_down, ln2_g, ln2_b):
    nb, s, d = x.shape
    lc = ctx.shape[1]
    depth = w_mod.shape[0]
    assert d == D_MODEL and lc == CTX_LEN and s % ROW_TILE == 0 and s % GRID_W == 0
    ntok = lc + s
    alpha = (2 * depth) ** 0.25

    tabs = _rope_tables(s, ntok)
    pos_lat = _dft_tables(s)
    pos_ctx = _dft_tables(lc)
    chan = _channel_table()
    e_q = _head_mean_matrix(A_Q_W)
    e_k = _head_mean_matrix(2 * A_KV_W)

    cc = jnp.concatenate([c, c_ctx[None, :]], axis=0)
    cc = jnp.pad(cc, ((0, (-(nb + 1)) % 8), (0, 0)))
    h_all = jnp.concatenate([ctx, x], axis=1).reshape(nb * ntok, d)

    offs = np.cumsum([0, A_Q_W, A_KV_W, A_KV_W, B_W, C_QK_W, C_QK_W, C_V_W, GATE_W])
    out = None
    for l in range(depth):
        last = l == depth - 1
        mod = _mod_call(cc, w_mod[l], b_mod[l])[:nb + 1].reshape(nb + 1, 6, d)
        mod = jnp.pad(mod, ((0, 0), (0, 2), (0, 0)))
        lam_init = 0.8 - 0.6 * math.exp(-0.3 * l)
        lam = (jnp.exp(jnp.sum(lam_q1[l].astype(F32) * lam_k1[l].astype(F32)))
               - jnp.exp(jnp.sum(lam_q2[l].astype(F32) * lam_k2[l].astype(F32)))) + lam_init
        lam = lam.reshape(1, 1).astype(F32)

        wl = w_in[l]
        seg = [wl[:, offs[i]:offs[i + 1]] for i in range(8)]
        in_wts = (seg[0].astype(BF16), _dup_heads(seg[1], A_KV_HEADS).astype(BF16),
                  _dup_heads(seg[2], A_KV_HEADS).astype(BF16), seg[3].astype(BF16),
                  seg[4].astype(BF16), seg[5].astype(BF16), seg[6].astype(BF16),
                  jnp.tile(qn_a[l].astype(F32), A_Q_HEADS).reshape(1, A_Q_W),
                  jnp.tile(kn_a[l].astype(F32), 2 * A_KV_HEADS).reshape(1, 2 * A_KV_W),
                  e_q, e_k)
        qa, ka, va, fb, qc, kc, vc = _inproj_call(h_all, mod, tabs, in_wts, nb, ntok)

        first_tile = 1 if last else 0
        r3 = lambda a: a.reshape(nb, ntok, a.shape[-1])
        oa = _attn_a_call(r3(qa), ka, r3(va), nb, ntok, first_tile)
        oc = _attn_c_call(lam, r3(qc), kc, r3(vc), subln_c[l].astype(F32).reshape(1, C_V_DIM),
                          nb, ntok, first_tile, 1.0 - lam_init)
        ob = _fourier_call(r3(fb), chan, pos_lat, pos_ctx, nb, ntok, first_tile)

        mix_wts = (seg[7].astype(BF16), w_br_a[l].astype(BF16), w_br_b[l].astype(BF16),
                   w_br_c[l].astype(BF16), w_out[l].astype(BF16),
                   ln1_g[l].astype(F32).reshape(1, d), ln1_b[l].astype(F32).reshape(1, d),
                   w_router[l].T.astype(BF16))
        flat = lambda a: a.reshape(-1, a.shape[-1])
        h1, xp, logits = _mixout_call(h_all, mod, flat(oa), flat(ob), flat(oc), mix_wts,
                                      nb, ntok, last, alpha)

        n_tok = h1.shape[0]
        idx, rw, rank, counts = _route_call(logits, b_router[l])
        first_blk, n_blk, total, table = _dispatch(idx, rank, counts, n_tok)
        y8 = _moe_call(first_blk, n_blk, total, table, xp,
                       w_e_gate, w_e_up, w_e_down, l, n_tok)
        y8 = y8.reshape(TOP_K, (n_tok + PAD_ROWS) * PSLAB, LANES)

        comb_wts = (w_sh_gate[l].astype(BF16), w_sh_up[l].astype(BF16), w_sh_down[l].astype(BF16),
                    ln2_g[l].astype(F32).reshape(1, d), ln2_b[l].astype(F32).reshape(1, d))
        h2 = _combine_call(h1, mod, y8, rw, comb_wts, s if last else ntok, not last, nb, alpha)
        if last:
            out = h2.reshape(nb, s, d)
        else:
            h_all = h2
    return out
```

```python
import functools
import math

import numpy as np
import jax
import jax.numpy as jnp
from jax import lax
from jax.experimental import pallas as pl
from jax.experimental.pallas import tpu as pltpu

F32 = jnp.float32
BF16 = jnp.bfloat16
U32 = jnp.uint32

D_MODEL = 1024
CTX_LEN = 256
GRID_W = 64
HEAD_DIM = 64
ROPE_THETA = 10000.0
A_Q_HEADS = 8
A_KV_HEADS = 2
A_Q_W = A_Q_HEADS * HEAD_DIM
A_KV_W = A_KV_HEADS * HEAD_DIM
F_GROUPS = 4
F_GROUP_W = 128
B_W = F_GROUPS * F_GROUP_W
C_HEADS = 4
C_V_DIM = 2 * HEAD_DIM
C_QK_W = C_HEADS * 2 * HEAD_DIM
C_V_W = C_HEADS * C_V_DIM
N_BRANCH = 3
GATE_W = N_BRANCH * D_MODEL
N_EXPERTS = 256
TOP_K = 8
N_GROUPS = 8
TOPK_GROUPS = 4
EXPERT_FF = 256
SHARED_FF = 256
ROUTED_SCALE = 2.5
LN_EPS = 1e-5
RMS_EPS = 1e-6
K_SHIFT = TOP_K.bit_length() - 1
assert 1 << K_SHIFT == TOP_K

LANES = 128
PSLAB = D_MODEL // (2 * LANES)
ROW_TILE = 256
MOE_BLOCK = 128
BLOCKS_IN_FLIGHT = 4
TAB_AHEAD = 4
TAB_SLOTS = 8
PAD_ROWS = BLOCKS_IN_FLIGHT * MOE_BLOCK // TOP_K
COMBINE_TILE = 128
VMEM_LIMIT = 56 * 1024 * 1024

_Q_SCALE = HEAD_DIM ** -0.5 * math.log2(math.e)


def _cparams(sem):
    return pltpu.CompilerParams(dimension_semantics=sem, vmem_limit_bytes=VMEM_LIMIT)


def _dot(a, b):
    return jnp.dot(a, b, preferred_element_type=F32)


def _dot_nt(a, b):
    return lax.dot_general(a, b, (((1,), (1,)), ((), ())), preferred_element_type=F32)


def _layer_norm_rows(x):
    mu = jnp.mean(x, axis=-1, keepdims=True)
    xc = x - mu
    var = jnp.mean(xc * xc, axis=-1, keepdims=True)
    return xc * lax.rsqrt(var + LN_EPS)


def _dot_split(x, e):
    hi = x.astype(BF16)
    lo = (x - hi.astype(F32)).astype(BF16)
    return _dot(hi, e) + _dot(lo, e)


def _mod_kernel(c_ref, w_ref, b_ref, o_ref):
    c = c_ref[...]
    sc = c * jax.nn.sigmoid(c)
    o_ref[...] = _dot(sc.astype(BF16), w_ref[...].astype(BF16)) + b_ref[...]


def _mod_call(cc, w_mod_l, b_mod_l):
    r = cc.shape[0]
    n = w_mod_l.shape[1]
    tn = D_MODEL
    return pl.pallas_call(
        _mod_kernel,
        out_shape=jax.ShapeDtypeStruct((r, n), F32),
        grid=(n // tn,),
        in_specs=[
            pl.BlockSpec((r, D_MODEL), lambda j: (0, 0)),
            pl.BlockSpec((D_MODEL, tn), lambda j: (0, j)),
            pl.BlockSpec((1, tn), lambda j: (0, j)),
        ],
        out_specs=pl.BlockSpec((r, tn), lambda j: (0, j)),
        compiler_params=_cparams(("arbitrary",)),
        name="mod_vectors",
    )(cc, w_mod_l, b_mod_l.reshape(1, n))


def _rope_cols(x, cos, sin_up, sin_dn):
    cols = []
    for c in range(x.shape[1] // LANES):
        xc = x[:, c * LANES:(c + 1) * LANES]
        up = pltpu.roll(xc, LANES - 16, axis=1)
        dn = pltpu.roll(xc, 16, axis=1)
        cols.append(xc * cos + up * sin_up + dn * sin_dn)
    return jnp.concatenate(cols, axis=1) if len(cols) > 1 else cols[0]


def _inproj_kernel(h_ref, mod_ref, cos_ref, su_ref, sd_ref,
                   wq_ref, wk_ref, wv_ref, wf_ref, wqc_ref, wkc_ref, wvc_ref,
                   qn_ref, kn_ref, eq_ref, ek_ref,
                   qa_o, ka_o, va_o, fb_o, qc_o, kc_o, vc_o):
    h = h_ref[...]
    shift = mod_ref[0, 0:1, :]
    scale = mod_ref[0, 1:2, :]
    u = (_layer_norm_rows(h) * (1.0 + scale) + shift).astype(BF16)
    cos = cos_ref[...]
    s_up = su_ref[...]
    s_dn = sd_ref[...]

    q = _dot(u, wq_ref[...])
    ms = _dot_split(q * q, eq_ref[...])
    q = q * lax.rsqrt(ms + RMS_EPS) * qn_ref[...]
    qa_o[...] = (_rope_cols(q, cos, s_up, s_dn) * _Q_SCALE).astype(BF16)

    k = _dot(u, wk_ref[...])
    ms = _dot_split(k * k, ek_ref[...])
    k = k * lax.rsqrt(ms + RMS_EPS) * kn_ref[...]
    ka_o[0] = jnp.transpose(_rope_cols(k, cos, s_up, s_dn)).astype(BF16)

    va_o[...] = _dot(u, wv_ref[...]).astype(BF16)
    fb_o[...] = _dot(u, wf_ref[...]).astype(BF16)
    qc = _dot(u, wqc_ref[...])
    qc_o[...] = (_rope_cols(qc, cos, s_up, s_dn) * _Q_SCALE).astype(BF16)
    kc = _dot(u, wkc_ref[...])
    kc_o[0] = jnp.transpose(_rope_cols(kc, cos, s_up, s_dn)).astype(BF16)
    vc_o[...] = _dot(u, wvc_ref[...]).astype(BF16)


def _inproj_call(h_all, mod, tabs, wts, nb, ntok):
    rows = h_all.shape[0]
    tpb = ntok // ROW_TILE
    cos, s_up, s_dn = tabs

    def full(a):
        return pl.BlockSpec(a.shape, lambda i: (0,) * a.ndim)

    def rowspec(w):
        return pl.BlockSpec((ROW_TILE, w), lambda i: (i, 0))

    tabspec = pl.BlockSpec((ROW_TILE, LANES), lambda i: (i % tpb, 0))
    modspec = pl.BlockSpec((1, 8, D_MODEL),
                           lambda i: (jnp.where(i % tpb == 0, nb, i // tpb), 0, 0))
    widths = (A_Q_W, 2 * A_KV_W, 2 * A_KV_W, B_W, C_QK_W, C_QK_W, C_V_W)
    transposed = (1, 5)

    def oshape(n, w):
        return (nb, w, ntok) if n in transposed else (rows, w)

    def ospec(n, w):
        if n in transposed:
            return pl.BlockSpec((1, w, ROW_TILE), lambda i: (i // tpb, 0, i % tpb))
        return rowspec(w)

    return pl.pallas_call(
        _inproj_kernel,
        out_shape=[jax.ShapeDtypeStruct(oshape(n, w), BF16) for n, w in enumerate(widths)],
        grid=(rows // ROW_TILE,),
        in_specs=[rowspec(D_MODEL), modspec, tabspec, tabspec, tabspec]
                 + [full(w) for w in wts],
        out_specs=[ospec(n, w) for n, w in enumerate(widths)],
        compiler_params=_cparams(("parallel",)),
        name="in_projection",
    )(h_all, mod, cos, s_up, s_dn, *wts)


def _softmax_parts(s):
    m = jnp.max(s, axis=-1, keepdims=True)
    e = jnp.exp2(s - m)
    return e, jnp.sum(e, axis=-1, keepdims=True)


def _gqa_tile(q_ref, k_ref, v_ref, o_ref, nk):
    lane = lax.broadcasted_iota(jnp.int32, (1, LANES), 1)
    low = lane < HEAD_DIM
    for c in range(A_Q_W // LANES):
        kvh = (2 * c) // (A_Q_HEADS // A_KV_HEADS)
        qc = q_ref[0, :, c * LANES:(c + 1) * LANES]
        kk = k_ref[0, kvh * LANES:(kvh + 1) * LANES, 0:nk]
        vv = v_ref[0, 0:nk, kvh * LANES:(kvh + 1) * LANES]
        halves = []
        for keep in (low, jnp.logical_not(low)):
            qm = jnp.where(keep, qc, jnp.zeros_like(qc))
            e, l = _softmax_parts(_dot(qm, kk))
            halves.append(_dot(e.astype(BF16), vv) * (1.0 / l))
        o_ref[0, :, c * LANES:(c + 1) * LANES] = jnp.where(low, halves[0], halves[1]).astype(BF16)


def _attn_a_kernel(q_ref, k_ref, v_ref, o_ref, *, ntok, first_tile):
    j = pl.program_id(1) + first_tile
    if first_tile == 0:
        @pl.when(j == 0)
        def _():
            _gqa_tile(q_ref, k_ref, v_ref, o_ref, CTX_LEN)

        @pl.when(j != 0)
        def _():
            _gqa_tile(q_ref, k_ref, v_ref, o_ref, ntok)
    else:
        _gqa_tile(q_ref, k_ref, v_ref, o_ref, ntok)


def _diff_tile(lam_ref, q_ref, k_ref, v_ref, g_ref, o_ref, nk, out_scale):
    lane = lax.broadcasted_iota(jnp.int32, (1, LANES), 1)
    low = lane < HEAD_DIM
    lam = lam_ref[0, 0]
    for hd in range(C_HEADS):
        sl = slice(hd * LANES, (hd + 1) * LANES)
        qc = q_ref[0, :, sl]
        kk = k_ref[0, sl, 0:nk]
        vv = v_ref[0, 0:nk, sl]
        q1 = jnp.where(low, qc, jnp.zeros_like(qc))
        q2 = jnp.where(low, jnp.zeros_like(qc), qc)
        e1, l1 = _softmax_parts(_dot(q1, kk))
        e2, l2 = _softmax_parts(_dot(q2, kk))
        o = _dot(e1.astype(BF16), vv) * (1.0 / l1) - _dot(e2.astype(BF16), vv) * (lam / l2)
        ms = jnp.mean(o * o, axis=-1, keepdims=True)
        o = o * lax.rsqrt(ms + RMS_EPS) * g_ref[...] * out_scale
        o_ref[0, :, sl] = o.astype(BF16)


def _attn_c_kernel(lam_ref, q_ref, k_ref, v_ref, g_ref, o_ref, *, ntok, first_tile, out_scale):
    j = pl.program_id(1) + first_tile
    if first_tile == 0:
        @pl.when(j == 0)
        def _():
            _diff_tile(lam_ref, q_ref, k_ref, v_ref, g_ref, o_ref, CTX_LEN, out_scale)

        @pl.when(j != 0)
        def _():
            _diff_tile(lam_ref, q_ref, k_ref, v_ref, g_ref, o_ref, ntok, out_scale)
    else:
        _diff_tile(lam_ref, q_ref, k_ref, v_ref, g_ref, o_ref, ntok, out_scale)


def _attn_specs(nb, ntok, first_tile, wq, wkv):
    tpb = ntok // ROW_TILE
    grid = (nb, tpb - first_tile)
    qspec = pl.BlockSpec((1, ROW_TILE, wq), lambda b, j: (b, j + first_tile, 0))
    kspec = pl.BlockSpec((1, wkv, ntok), lambda b, j: (b, 0, 0))
    vspec = pl.BlockSpec((1, ntok, wkv), lambda b, j: (b, 0, 0))
    ospec = pl.BlockSpec((1, ROW_TILE, wq), lambda b, j: (b, j, 0))
    out_rows = ntok - first_tile * ROW_TILE
    return grid, qspec, kspec, vspec, ospec, out_rows


def _attn_a_call(qa, ka, va, nb, ntok, first_tile):
    grid, qspec, kspec, vspec, ospec, out_rows = _attn_specs(nb, ntok, first_tile, A_Q_W, 2 * A_KV_W)
    return pl.pallas_call(
        functools.partial(_attn_a_kernel, ntok=ntok, first_tile=first_tile),
        out_shape=jax.ShapeDtypeStruct((nb, out_rows, A_Q_W), BF16),
        grid=grid,
        in_specs=[qspec, kspec, vspec],
        out_specs=ospec,
        compiler_params=_cparams(("parallel", "arbitrary")),
        name="gqa_attention",
    )(qa, ka, va)


def _attn_c_call(lam, qc, kc, vc, subln, nb, ntok, first_tile, out_scale):
    grid, qspec, kspec, vspec, ospec, out_rows = _attn_specs(nb, ntok, first_tile, C_QK_W, C_QK_W)
    return pl.pallas_call(
        functools.partial(_attn_c_kernel, ntok=ntok, first_tile=first_tile, out_scale=out_scale),
        out_shape=jax.ShapeDtypeStruct((nb, out_rows, C_V_W), BF16),
        grid=grid,
        in_specs=[pl.BlockSpec(memory_space=pltpu.SMEM), qspec, kspec, vspec,
                  pl.BlockSpec((1, C_V_DIM), lambda b, j: (0, 0))],
        out_specs=ospec,
        compiler_params=_cparams(("parallel", "arbitrary")),
        name="diff_attention",
    )(lam, qc, kc, vc, subln)


def _fourier_kernel(f_ref, chan_ref, pos_ref, posc_ref, o_ref, g_ref, *, ntok, first_tile):
    j = pl.program_id(1) + first_tile
    nlat = ntok - CTX_LEN

    def channel_stage(rows):
        g = _dot(rows, chan_ref[...])
        return jnp.concatenate([g[:, :B_W], g[:, B_W:]], axis=0).astype(BF16)

    if first_tile == 0:
        @pl.when(j == 0)
        def _():
            gc = channel_stage(f_ref[0, 0:CTX_LEN, :])
            y = _dot(posc_ref[...], gc) * (1.0 / math.sqrt(CTX_LEN * F_GROUP_W))
            o_ref[0] = y.astype(BF16)

    @pl.when(j == 1)
    def _():
        g_ref[...] = channel_stage(f_ref[0, CTX_LEN:ntok, :])

    @pl.when(j >= 1)
    def _():
        y = _dot(pos_ref[...], g_ref[...]) * (1.0 / math.sqrt(nlat * F_GROUP_W))
        o_ref[0] = y.astype(BF16)


def _fourier_call(fb, chan, pos, posc, nb, ntok, first_tile):
    tpb = ntok // ROW_TILE
    nlat = ntok - CTX_LEN
    return pl.pallas_call(
        functools.partial(_fourier_kernel, ntok=ntok, first_tile=first_tile),
        out_shape=jax.ShapeDtypeStruct((nb, ntok - first_tile * ROW_TILE, B_W), BF16),
        grid=(nb, tpb - first_tile),
        in_specs=[
            pl.BlockSpec((1, ntok, B_W), lambda b, j: (b, 0, 0)),
            pl.BlockSpec(chan.shape, lambda b, j: (0, 0)),
            pl.BlockSpec((ROW_TILE, 2 * nlat),
                         lambda b, j: (jnp.maximum(j + first_tile - 1, 0), 0)),
            pl.BlockSpec(posc.shape, lambda b, j: (0, 0)),
        ],
        out_specs=pl.BlockSpec((1, ROW_TILE, B_W), lambda b, j: (b, j, 0)),
        scratch_shapes=[pltpu.VMEM((2 * nlat, B_W), BF16)],
        compiler_params=_cparams(("parallel", "arbitrary")),
        name="fourier_mix",
    )(fb, chan, pos, posc)


def _dft_tables(n):
    n1 = 32
    n0 = n // n1
    k = np.arange(n, dtype=np.int64)
    a = 2.0 * np.pi * ((k[:, None] * np.arange(n1)[None, :] * n0) % n) / n
    b = 2.0 * np.pi * ((k[:, None] * np.arange(n0)[None, :]) % n) / n
    ca, sa = jnp.asarray(np.cos(a), F32)[:, :, None], jnp.asarray(np.sin(a), F32)[:, :, None]
    cb, sb = jnp.asarray(np.cos(b), F32)[:, None, :], jnp.asarray(np.sin(b), F32)[:, None, :]
    cos = (ca * cb - sa * sb).reshape(n, n)
    sin = (sa * cb + ca * sb).reshape(n, n)
    return jnp.concatenate([cos, -sin], axis=1).astype(BF16)


def _channel_table():
    c = np.arange(F_GROUP_W)
    ang = 2.0 * np.pi * ((c[:, None] * c[None, :]) % F_GROUP_W) / F_GROUP_W
    eye = np.eye(F_GROUPS)
    cos = np.kron(eye, np.cos(ang))
    sin = np.kron(eye, np.sin(ang))
    return jnp.asarray(np.concatenate([cos, sin], axis=1), F32).astype(BF16)


def _store_packed_slabs(dst_ref, u):
    rows = u.shape[0]
    for jb in range(PSLAB):
        lo = u[:, 2 * jb * LANES:(2 * jb + 1) * LANES]
        hi = u[:, (2 * jb + 1) * LANES:(2 * jb + 2) * LANES]
        dst_ref[pl.ds(jb, rows, stride=PSLAB), :] = pltpu.pack_elementwise([lo, hi], packed_dtype=BF16)


def _load_packed_slabs(src_ref, rows):
    out = []
    for jb in range(PSLAB):
        words = src_ref[pl.ds(jb, rows, stride=PSLAB), :]
        out.append((pltpu.unpack_elementwise(words, index=0, packed_dtype=BF16, unpacked_dtype=F32),
                    pltpu.unpack_elementwise(words, index=1, packed_dtype=BF16, unpacked_dtype=F32)))
    return out


def _mixout_kernel(h_ref, mod_ref, oa_ref, ob_ref, oc_ref,
                   wg_ref, wa_ref, wb_ref, wc_ref, wo_ref, g1_ref, b1_ref, wr_ref,
                   h1_o, xp_o, lg_o, *, alpha):
    h = h_ref[...]
    m = mod_ref[0]
    u = (_layer_norm_rows(h) * (1.0 + m[1:2, :]) + m[0:1, :]).astype(BF16)
    y = None
    for n, (o_ref, w_ref) in enumerate(((oa_ref, wa_ref), (ob_ref, wb_ref), (oc_ref, wc_ref))):
        gate = jax.nn.sigmoid(_dot(u, wg_ref[:, n * D_MODEL:(n + 1) * D_MODEL]))
        t = gate * _dot(o_ref[...], w_ref[...])
        y = t if y is None else y + t
    z = _dot(y.astype(BF16), wo_ref[...])
    h1 = _layer_norm_rows(alpha * h + m[2:3, :] * z) * g1_ref[...] + b1_ref[...]
    h1_o[...] = h1
    u2 = _layer_norm_rows(h1) * (1.0 + m[4:5, :]) + m[3:4, :]
    _store_packed_slabs(xp_o, u2)
    lg_o[...] = _dot_nt(wr_ref[...], u2.astype(BF16))


def _tile_maps(nb, ntok, lat_only):
    tpb = ntok // ROW_TILE
    if lat_only:
        lpb = tpb - 1
        n_tiles = nb * lpb
        src = lambda i: (i // lpb) * tpb + 1 + i % lpb
        modi = lambda i: i // lpb
    else:
        n_tiles = nb * tpb
        src = lambda i: i
        modi = lambda i: jnp.where(i % tpb == 0, nb, i // tpb)
    return n_tiles, src, modi


def _mixout_call(h_all, mod, oa, ob, oc, wts, nb, ntok, lat_only, alpha):
    n_tiles, src, modi = _tile_maps(nb, ntok, lat_only)
    rows_out = n_tiles * ROW_TILE

    def full(a):
        return pl.BlockSpec(a.shape, lambda i: (0,) * a.ndim)

    def inrow(w):
        return pl.BlockSpec((ROW_TILE, w), lambda i: (src(i), 0))

    def outrow(w):
        return pl.BlockSpec((ROW_TILE, w), lambda i: (i, 0))

    modspec = pl.BlockSpec((1, 8, D_MODEL), lambda i: (modi(i), 0, 0))
    return pl.pallas_call(
        functools.partial(_mixout_kernel, alpha=alpha),
        out_shape=[jax.ShapeDtypeStruct((rows_out, D_MODEL), F32),
                   jax.ShapeDtypeStruct((rows_out * PSLAB, LANES), U32),
                   jax.ShapeDtypeStruct((N_EXPERTS, rows_out), F32)],
        grid=(n_tiles,),
        in_specs=[inrow(D_MODEL), modspec, outrow(A_Q_W), outrow(B_W), outrow(C_V_W)]
                 + [full(w) for w in wts],
        out_specs=[outrow(D_MODEL), pl.BlockSpec((ROW_TILE * PSLAB, LANES), lambda i: (i, 0)),
                   pl.BlockSpec((N_EXPERTS, ROW_TILE), lambda i: (0, i))],
        compiler_params=_cparams(("parallel",)),
        name="mixer_output",
    )(h_all, mod, oa, ob, oc, *wts)


def _moe_kernel(first_ref, count_ref, total_ref,
                tab_hbm, xp_ref, wg_ref, wu_ref, wd_ref,
                out_hbm,
                tab, gath, wgu_s, wd_s, ybuf, sem, tsem, *, t_pad, n_tok, lead_row):
    nfl = BLOCKS_IN_FLIGHT
    tmask = TAB_SLOTS - 1
    e = pl.program_id(0)
    total = total_ref[0]

    def row_copy(s, m, dst4):
        return pltpu.make_async_copy(ybuf.at[s, pl.ds(PSLAB * m, PSLAB), :],
                                     out_hbm.at[pl.ds(pl.multiple_of(dst4, PSLAB), PSLAB), :],
                                     sem.at[s])

    def wait_rows(s):
        for m in range(MOE_BLOCK):
            row_copy(s, m, 0).wait()

    def send_rows(ts, ys):
        for m in range(MOE_BLOCK):
            row_copy(ys, m, tab[ts, 0, MOE_BLOCK + m]).start(priority=m % 2)

    def tab_copy(row, s):
        return pltpu.make_async_copy(tab_hbm.at[row], tab.at[s], tsem.at[s])

    def gather_rows(ts, gs):
        for m in range(MOE_BLOCK):
            t4 = pl.multiple_of(tab[ts, 0, m], PSLAB)
            gath[gs, PSLAB * m:PSLAB * (m + 1), :] = xp_ref[pl.ds(t4, PSLAB), :]

    @pl.when(e == 0)
    def _():
        ybuf[...] = jnp.zeros_like(ybuf)
        tab_copy(lead_row, tmask).start()
        for r in range(TAB_AHEAD):
            tab_copy(r, r).start()
        for b in range(nfl - 1):
            for m in range(MOE_BLOCK):
                q = b * MOE_BLOCK + m
                row_copy(b, m, ((q % TOP_K) * t_pad + n_tok + q // TOP_K) * PSLAB).start(priority=m % 2)
        tab_copy(lead_row, tmask).wait()
        tab_copy(0, 0).wait()
        gather_rows(0, 0)

    wgu_s[:, 0:EXPERT_FF] = wg_ref[0, 0].astype(BF16)
    wgu_s[:, EXPERT_FF:2 * EXPERT_FF] = wu_ref[0, 0].astype(BF16)
    wd_s[...] = wd_ref[0, 0].astype(BF16)

    def block(j, carry):
        g = first_ref[e] + j
        tab_copy(g + 1, (g + 1) & tmask).wait()
        tab_copy(g + TAB_AHEAD, (g + TAB_AHEAD) & tmask).start()
        ys = lax.rem(g, nfl)
        wait_rows(ys)

        send_rows((g - 1) & tmask, lax.rem(g + nfl - 1, nfl))
        gather_rows((g + 1) & tmask, (g + 1) & 1)

        acc = None
        for jb, (lo, hi) in enumerate(_load_packed_slabs(gath.at[g & 1], MOE_BLOCK)):
            a = jnp.concatenate([lo, hi], axis=1).astype(BF16)
            t = _dot(a, wgu_s[jb * 2 * LANES:(jb + 1) * 2 * LANES, :])
            acc = t if acc is None else acc + t
        gt = acc[:, 0:EXPERT_FF]
        act = (gt * jax.nn.sigmoid(gt)) * acc[:, EXPERT_FF:2 * EXPERT_FF]
        y = _dot(act.astype(BF16), wd_s[...])
        _store_packed_slabs(ybuf.at[ys], y)
        return carry

    lax.fori_loop(0, count_ref[e], block, 0)

    @pl.when(e == pl.num_programs(0) - 1)
    def _():
        send_rows((total - 1) & tmask, lax.rem(total + nfl - 1, nfl))
        for s in range(nfl):
            wait_rows(s)
        for r in range(1, TAB_AHEAD):
            tab_copy(0, (total + r) & tmask).wait()


def _moe_call(first_blk, n_blk, total, table, xp4, wg, wu, wd, layer, n_tok):
    t_pad = n_tok + PAD_ROWS
    lead_row = table.shape[0] - 1
    wspec = lambda r, c: pl.BlockSpec((1, 1, r, c), lambda e, fb, nb, tt: (layer, e, 0, 0))
    grid_spec = pltpu.PrefetchScalarGridSpec(
        num_scalar_prefetch=3,
        grid=(N_EXPERTS,),
        in_specs=[
            pl.BlockSpec(memory_space=pl.ANY),
            pl.BlockSpec(memory_space=pltpu.VMEM),
            wspec(D_MODEL, EXPERT_FF), wspec(D_MODEL, EXPERT_FF), wspec(EXPERT_FF, D_MODEL),
        ],
        out_specs=pl.BlockSpec(memory_space=pl.ANY),
        scratch_shapes=[
            pltpu.SMEM((TAB_SLOTS, 1, 2 * MOE_BLOCK), jnp.int32),
            pltpu.VMEM((2, MOE_BLOCK * PSLAB, LANES), U32),
            pltpu.VMEM((D_MODEL, 2 * EXPERT_FF), BF16),
            pltpu.VMEM((EXPERT_FF, D_MODEL), BF16),
            pltpu.VMEM((BLOCKS_IN_FLIGHT, MOE_BLOCK * PSLAB, LANES), U32),
            pltpu.SemaphoreType.DMA((BLOCKS_IN_FLIGHT,)),
            pltpu.SemaphoreType.DMA((TAB_SLOTS,)),
        ],
    )
    return pl.pallas_call(
        functools.partial(_moe_kernel, t_pad=t_pad, n_tok=n_tok, lead_row=lead_row),
        out_shape=jax.ShapeDtypeStruct((TOP_K * t_pad * PSLAB, LANES), U32),
        grid_spec=grid_spec,
        compiler_params=_cparams(("arbitrary",)),
        name="routed_experts",
    )(first_blk, n_blk, total, table, xp4, wg, wu, wd)


def _combine_kernel(h1_ref, mod_ref, y8_ref, rw_ref, sg_ref, su_ref, sd_ref, g2_ref, b2_ref, o_ref,
                    *, alpha):
    h1 = h1_ref[...]
    m = mod_ref[0]
    u2 = (_layer_norm_rows(h1) * (1.0 + m[4:5, :]) + m[3:4, :]).astype(BF16)
    g = _dot(u2, sg_ref[...])
    act = (g * jax.nn.sigmoid(g)) * _dot(u2, su_ref[...])
    f = _dot(act.astype(BF16), sd_ref[...])
    tm = h1.shape[0]
    rw = rw_ref[...]
    wk = [jnp.broadcast_to(rw[:, kk:kk + 1], (tm, LANES)) for kk in range(TOP_K)]
    cols = [None] * (2 * PSLAB)
    for kk in range(TOP_K):
        for jb, pair in enumerate(_load_packed_slabs(y8_ref.at[kk], tm)):
            for half in range(2):
                t = wk[kk] * pair[half]
                c = 2 * jb + half
                cols[c] = t if cols[c] is None else cols[c] + t
    f = f + jnp.concatenate(cols, axis=1)
    o_ref[...] = _layer_norm_rows(alpha * h1 + m[5:6, :] * f) * g2_ref[...] + b2_ref[...]


def _combine_call(h1, mod, y8, rw, wts, rows_per_group, group_is_ctx_first, nb, alpha):
    rows = h1.shape[0]
    tm = COMBINE_TILE
    tpg = rows_per_group // tm
    if group_is_ctx_first:
        first = CTX_LEN // tm
        modi = lambda i: jnp.where(i % tpg < first, nb, i // tpg)
    else:
        modi = lambda i: i // tpg

    def full(a):
        return pl.BlockSpec(a.shape, lambda i: (0,) * a.ndim)

    return pl.pallas_call(
        functools.partial(_combine_kernel, alpha=alpha),
        out_shape=jax.ShapeDtypeStruct((rows, D_MODEL), F32),
        grid=(rows // tm,),
        in_specs=[pl.BlockSpec((tm, D_MODEL), lambda i: (i, 0)),
                  pl.BlockSpec((1, 8, D_MODEL), lambda i: (modi(i), 0, 0)),
                  pl.BlockSpec((TOP_K, tm * PSLAB, LANES), lambda i: (0, i, 0)),
                  pl.BlockSpec((tm, TOP_K), lambda i: (i, 0))]
                 + [full(w) for w in wts],
        out_specs=pl.BlockSpec((tm, D_MODEL), lambda i: (i, 0)),
        compiler_params=_cparams(("parallel",)),
        name="moe_combine",
    )(h1, mod, y8, rw, *wts)


_BIG_LANE = 1 << 30


def _route_kernel(lg_ref, b_ref, idx_o, w_o, rank_o, cnt_o, carry):
    i = pl.program_id(0)

    @pl.when(i == 0)
    def _():
        carry[...] = jnp.zeros_like(carry)

    tm = lg_ref.shape[1]
    gsize = N_EXPERTS // N_GROUPS
    scores = jax.nn.sigmoid(lg_ref[...])
    biased = scores + b_ref[...]
    eid = lax.broadcasted_iota(jnp.int32, (N_EXPERTS, tm), 0)

    b3 = biased.reshape(N_GROUPS, gsize, tm)
    in_g = lax.broadcasted_iota(jnp.int32, (N_GROUPS, gsize, tm), 1)
    m1 = jnp.max(b3, axis=1, keepdims=True)
    first = jnp.min(jnp.where(b3 == m1, in_g, _BIG_LANE), axis=1, keepdims=True)
    m2 = jnp.max(jnp.where(in_g == first, -jnp.inf, b3), axis=1, keepdims=True)
    gscore = (m1 + m2).reshape(N_GROUPS, tm)

    gid = lax.broadcasted_iota(jnp.int32, (N_GROUPS, tm), 0)
    beaten = jnp.zeros((N_GROUPS, tm), jnp.int32)
    for g in range(N_GROUPS):
        sg = gscore[g:g + 1, :]
        ahead = jnp.logical_or(sg > gscore, jnp.logical_and(sg == gscore, g < gid))
        beaten = beaten + ahead.astype(jnp.int32)
    keep = jnp.broadcast_to((beaten < TOPK_GROUPS).astype(jnp.int32).reshape(N_GROUPS, 1, tm),
                            (N_GROUPS, gsize, tm)).reshape(N_EXPERTS, tm)
    masked = jnp.where(keep > 0, biased, -jnp.inf)

    idxs, ws, hots = [], [], []
    for _ in range(TOP_K):
        m = jnp.max(masked, axis=0, keepdims=True)
        ix = jnp.min(jnp.where(masked == m, eid, _BIG_LANE), axis=0, keepdims=True)
        hot = eid == ix
        idxs.append(ix)
        ws.append(jnp.sum(jnp.where(hot, scores, 0.0), axis=0, keepdims=True))
        hots.append(hot)
        masked = jnp.where(hot, -jnp.inf, masked)
    wsum = ws[0]
    for r in range(1, TOP_K):
        wsum = wsum + ws[r]
    idx_o[...] = jnp.concatenate(idxs, axis=0)
    w_rows = jnp.concatenate([wr / wsum * ROUTED_SCALE for wr in ws], axis=0)

    eye = (lax.broadcasted_iota(jnp.int32, (tm, tm), 0)
           == lax.broadcasted_iota(jnp.int32, (tm, tm), 1)).astype(BF16)
    w_cols = jnp.zeros((tm, TOP_K), F32)
    rest = w_rows
    for _ in range(3):
        part = rest.astype(BF16)
        rest = rest - part.astype(F32)
        w_cols = w_cols + _dot_nt(eye, part)
    w_o[...] = w_cols

    sel = jnp.zeros((N_EXPERTS, tm), F32)
    for hot in hots:
        sel = sel + hot.astype(F32)
    sel = sel.astype(BF16)
    earlier = (lax.broadcasted_iota(jnp.int32, (tm, tm), 0)
               < lax.broadcasted_iota(jnp.int32, (tm, tm), 1)).astype(BF16)
    prefix = _dot(sel, earlier) + carry[:, 0:1]
    rank_o[...] = jnp.concatenate(
        [jnp.sum(jnp.where(hot, prefix, 0.0), axis=0, keepdims=True) for hot in hots],
        axis=0).astype(jnp.int32)
    carry[...] = carry[...] + _dot(sel, jnp.ones((tm, LANES), BF16))
    cnt_o[...] = carry[...]


def _route_call(logits_t, b_r):
    t = logits_t.shape[1]
    tm = ROW_TILE
    kt = pl.BlockSpec((TOP_K, tm), lambda i: (0, i))
    return pl.pallas_call(
        _route_kernel,
        out_shape=[jax.ShapeDtypeStruct((TOP_K, t), jnp.int32),
                   jax.ShapeDtypeStruct((t, TOP_K), F32),
                   jax.ShapeDtypeStruct((TOP_K, t), jnp.int32),
                   jax.ShapeDtypeStruct((N_EXPERTS, LANES), F32)],
        grid=(t // tm,),
        in_specs=[pl.BlockSpec((N_EXPERTS, tm), lambda i: (0, i)),
                  pl.BlockSpec((N_EXPERTS, 1), lambda i: (0, 0))],
        out_specs=[kt, pl.BlockSpec((tm, TOP_K), lambda i: (i, 0)), kt,
                   pl.BlockSpec((N_EXPERTS, LANES), lambda i: (0, 0))],
        scratch_shapes=[pltpu.VMEM((N_EXPERTS, LANES), F32)],
        compiler_params=_cparams(("arbitrary",)),
        name="route_topk",
    )(logits_t, b_r.astype(F32).reshape(N_EXPERTS, 1))


def _dest_kernel(idx_ref, rank_ref, start_ref, o_ref):
    tm = idx_ref.shape[1]
    eid = lax.broadcasted_iota(jnp.int32, (N_EXPERTS, tm), 0)
    idx = idx_ref[...]
    start = start_ref[...]
    rows = [jnp.sum(jnp.where(eid == idx[r:r + 1, :], start, 0), axis=0, keepdims=True)
            for r in range(TOP_K)]
    o_ref[...] = jnp.concatenate(rows, axis=0) + rank_ref[...]


def _dest_call(idx, rank, pad_start):
    t = idx.shape[1]
    tm = ROW_TILE
    blk = pl.BlockSpec((TOP_K, tm), lambda i: (0, i))
    return pl.pallas_call(
        _dest_kernel,
        out_shape=jax.ShapeDtypeStruct((TOP_K, t), jnp.int32),
        grid=(t // tm,),
        in_specs=[blk, blk, pl.BlockSpec((N_EXPERTS, 1), lambda i: (0, 0))],
        out_specs=blk,
        compiler_params=_cparams(("parallel",)),
        name="slot_of_assignment",
    )(idx, rank, pad_start.reshape(N_EXPERTS, 1))


def _dispatch(idx, rank, counts, n_tok):
    n_assign = n_tok * TOP_K
    used_max = (n_assign + N_EXPERTS * (MOE_BLOCK - 1) + MOE_BLOCK - 1) // MOE_BLOCK
    nblk = -(-(used_max + TAB_AHEAD + 1) // BLOCKS_IN_FLIGHT) * BLOCKS_IN_FLIGHT
    n_slots = nblk * MOE_BLOCK
    counts = counts[:, 0].astype(jnp.int32)
    padded = (counts + MOE_BLOCK - 1) // MOE_BLOCK * MOE_BLOCK
    pad_end = jnp.cumsum(padded)
    pad_start = pad_end - padded
    dest = _dest_call(idx, rank, pad_start)
    assign = (jnp.arange(n_tok, dtype=jnp.int32)[None, :] * TOP_K
              + jnp.arange(TOP_K, dtype=jnp.int32)[:, None])
    assert (nblk - 1) % BLOCKS_IN_FLIGHT == BLOCKS_IN_FLIGHT - 1
    pad_a = n_assign + jnp.arange(n_slots, dtype=jnp.int32) % (BLOCKS_IN_FLIGHT * MOE_BLOCK)
    slot_a = pad_a.at[dest.reshape(-1)].set(assign.reshape(-1), unique_indices=True)
    t_pad = n_tok + PAD_ROWS
    tok = lax.shift_right_logical(slot_a, K_SHIFT)
    tok4 = jnp.minimum(tok, n_tok - 1) * PSLAB
    dst4 = ((slot_a & (TOP_K - 1)) * t_pad + tok) * PSLAB
    table = jnp.concatenate([tok4.reshape(nblk, 1, MOE_BLOCK), dst4.reshape(nblk, 1, MOE_BLOCK)], axis=2)
    total = (pad_end[-1] // MOE_BLOCK).astype(jnp.int32).reshape(1)
    return pad_start // MOE_BLOCK, padded // MOE_BLOCK, total, table


def _rope_tables(s, ntok):
    rows_n = s // GRID_W
    row = jnp.repeat(jnp.arange(rows_n, dtype=F32), GRID_W)
    col = jnp.tile(jnp.arange(GRID_W, dtype=F32), rows_n)
    axis_dim = HEAD_DIM // 2
    inv = jnp.power(ROPE_THETA, -jnp.arange(0, axis_dim, 2, dtype=F32) / axis_dim)
    ar = row[:, None] * inv[None]
    ac = col[:, None] * inv[None]
    ang = jnp.concatenate([ar, ar, ac, ac], -1)
    cos, sin = jnp.cos(ang), jnp.sin(ang)
    quarter = (jnp.arange(HEAD_DIM) // 16) % 2
    s_up = jnp.where(quarter == 0, -sin, 0.0)
    s_dn = jnp.where(quarter == 1, sin, 0.0)
    nctx = ntok - s

    def expand(t, ctx_val):
        t = jnp.concatenate([jnp.full((nctx, HEAD_DIM), ctx_val, F32), t], axis=0)
        return jnp.tile(t, (1, LANES // HEAD_DIM))

    return expand(cos, 1.0), expand(s_up, 0.0), expand(s_dn, 0.0)


def _head_mean_matrix(width):
    hid = np.arange(width) // HEAD_DIM
    return jnp.asarray((hid[:, None] == hid[None, :]).astype(np.float32) / HEAD_DIM).astype(BF16)


def _dup_heads(a, n_heads):
    parts = []
    for hd in range(n_heads):
        p = a[..., hd * HEAD_DIM:(hd + 1) * HEAD_DIM]
        parts += [p, p]
    return jnp.concatenate(parts, axis=-1)


def kernel(x, c, ctx, c_ctx, w_mod, b_mod, w_in, qn_a, kn_a, lam_q1, lam_k1, lam_q2, lam_k2, subln_c, w_br_a, w_br_b, w_br_c, w_out, ln1_g, ln1_b, w_router, b_router, w_sh_gate, w_sh_up, w_sh_down, w_e_gate, w_e_up, w_e_down, ln2_g, ln2_b):
    nb, s, d = x.shape
    lc = ctx.shape[1]
    depth = w_mod.shape[0]
    assert d == D_MODEL and lc == CTX_LEN and s % ROW_TILE == 0 and s % GRID_W == 0
    ntok = lc + s
    alpha = (2 * depth) ** 0.25

    tabs = _rope_tables(s, ntok)
    pos_lat = _dft_tables(s)
    pos_ctx = _dft_tables(lc)
    chan = _channel_table()
    e_q = _head_mean_matrix(A_Q_W)
    e_k = _head_mean_matrix(2 * A_KV_W)

    cc = jnp.concatenate([c, c_ctx[None, :]], axis=0)
    cc = jnp.pad(cc, ((0, (-(nb + 1)) % 8), (0, 0)))
    h_all = jnp.concatenate([ctx, x], axis=1).reshape(nb * ntok, d)

    offs = np.cumsum([0, A_Q_W, A_KV_W, A_KV_W, B_W, C_QK_W, C_QK_W, C_V_W, GATE_W])
    out = None
    for l in range(depth):
        last = l == depth - 1
        mod = _mod_call(cc, w_mod[l], b_mod[l])[:nb + 1].reshape(nb + 1, 6, d)
        mod = jnp.pad(mod, ((0, 0), (0, 2), (0, 0)))
        lam_init = 0.8 - 0.6 * math.exp(-0.3 * l)
        lam = (jnp.exp(jnp.sum(lam_q1[l].astype(F32) * lam_k1[l].astype(F32)))
               - jnp.exp(jnp.sum(lam_q2[l].astype(F32) * lam_k2[l].astype(F32)))) + lam_init
        lam = lam.reshape(1, 1).astype(F32)

        wl = w_in[l]
        seg = [wl[:, offs[i]:offs[i + 1]] for i in range(8)]
        in_wts = (seg[0].astype(BF16), _dup_heads(seg[1], A_KV_HEADS).astype(BF16),
                  _dup_heads(seg[2], A_KV_HEADS).astype(BF16), seg[3].astype(BF16),
                  seg[4].astype(BF16), seg[5].astype(BF16), seg[6].astype(BF16),
                  jnp.tile(qn_a[l].astype(F32), A_Q_HEADS).reshape(1, A_Q_W),
                  jnp.tile(kn_a[l].astype(F32), 2 * A_KV_HEADS).reshape(1, 2 * A_KV_W),
                  e_q, e_k)
        qa, ka, va, fb, qc, kc, vc = _inproj_call(h_all, mod, tabs, in_wts, nb, ntok)

        first_tile = 1 if last else 0
        r3 = lambda a: a.reshape(nb, ntok, a.shape[-1])
        oa = _attn_a_call(r3(qa), ka, r3(va), nb, ntok, first_tile)
        oc = _attn_c_call(lam, r3(qc), kc, r3(vc), subln_c[l].astype(F32).reshape(1, C_V_DIM),
                          nb, ntok, first_tile, 1.0 - lam_init)
        ob = _fourier_call(r3(fb), chan, pos_lat, pos_ctx, nb, ntok, first_tile)

        mix_wts = (seg[7].astype(BF16), w_br_a[l].astype(BF16), w_br_b[l].astype(BF16),
                   w_br_c[l].astype(BF16), w_out[l].astype(BF16),
                   ln1_g[l].astype(F32).reshape(1, d), ln1_b[l].astype(F32).reshape(1, d),
                   w_router[l].T.astype(BF16))
        flat = lambda a: a.reshape(-1, a.shape[-1])
        h1, xp, logits = _mixout_call(h_all, mod, flat(oa), flat(ob), flat(oc), mix_wts,
                                      nb, ntok, last, alpha)

        n_tok = h1.shape[0]
        idx, rw, rank, counts = _route_call(logits, b_router[l])
        first_blk, n_blk, total, table = _dispatch(idx, rank, counts, n_tok)
        y8 = _moe_call(first_blk, n_blk, total, table, xp,
                       w_e_gate, w_e_up, w_e_down, l, n_tok)
        y8 = y8.reshape(TOP_K, (n_tok + PAD_ROWS) * PSLAB, LANES)

        comb_wts = (w_sh_gate[l].astype(BF16), w_sh_up[l].astype(BF16), w_sh_down[l].astype(BF16),
                    ln2_g[l].astype(F32).reshape(1, d), ln2_b[l].astype(F32).reshape(1, d))
        h2 = _combine_call(h1, mod, y8, rw, comb_wts, s if last else ntok, not last, nb, alpha)
        if last:
            out = h2.reshape(nb, s, d)
        else:
            h_all = h2
    return out
```

```python
import functools
import math

import numpy as np
import jax
import jax.numpy as jnp
from jax import lax
from jax.experimental import pallas as pl
from jax.experimental.pallas import tpu as pltpu

F32 = jnp.float32
BF16 = jnp.bfloat16
U32 = jnp.uint32

D_MODEL = 1024
CTX_LEN = 256
GRID_W = 64
HEAD_DIM = 64
ROPE_THETA = 10000.0
A_Q_HEADS = 8
A_KV_HEADS = 2
A_Q_W = A_Q_HEADS * HEAD_DIM
A_KV_W = A_KV_HEADS * HEAD_DIM
F_GROUPS = 4
F_GROUP_W = 128
B_W = F_GROUPS * F_GROUP_W
C_HEADS = 4
C_V_DIM = 2 * HEAD_DIM
C_QK_W = C_HEADS * 2 * HEAD_DIM
C_V_W = C_HEADS * C_V_DIM
N_BRANCH = 3
GATE_W = N_BRANCH * D_MODEL
N_EXPERTS = 256
TOP_K = 8
N_GROUPS = 8
TOPK_GROUPS = 4
EXPERT_FF = 256
SHARED_FF = 256
ROUTED_SCALE = 2.5
LN_EPS = 1e-5
RMS_EPS = 1e-6
K_SHIFT = TOP_K.bit_length() - 1
assert 1 << K_SHIFT == TOP_K

LANES = 128
PSLAB = D_MODEL // (2 * LANES)
ROW_TILE = 256
ATTN_TILE = 512
MOE_BLOCK = 128
BLOCKS_IN_FLIGHT = 4
TAB_AHEAD = 4
TAB_SLOTS = 8
PAD_ROWS = BLOCKS_IN_FLIGHT * MOE_BLOCK // TOP_K
COMBINE_TILE = 128
VMEM_LIMIT = 56 * 1024 * 1024

_Q_SCALE = HEAD_DIM ** -0.5 * math.log2(math.e)


def _cparams(sem):
    return pltpu.CompilerParams(dimension_semantics=sem, vmem_limit_bytes=VMEM_LIMIT)


def _dot(a, b):
    return jnp.dot(a, b, preferred_element_type=F32)


def _dot_nt(a, b):
    return lax.dot_general(a, b, (((1,), (1,)), ((), ())), preferred_element_type=F32)


def _layer_norm_rows(x):
    mu = jnp.mean(x, axis=-1, keepdims=True)
    xc = x - mu
    var = jnp.mean(xc * xc, axis=-1, keepdims=True)
    return xc * lax.rsqrt(var + LN_EPS)


def _dot_split(x, e):
    hi = x.astype(BF16)
    lo = (x - hi.astype(F32)).astype(BF16)
    return _dot(hi, e) + _dot(lo, e)


def _mod_kernel(c_ref, w_ref, b_ref, o_ref):
    c = c_ref[...]
    sc = c * jax.nn.sigmoid(c)
    o_ref[...] = _dot(sc.astype(BF16), w_ref[...].astype(BF16)) + b_ref[...]


def _mod_call(cc, w_mod_l, b_mod_l):
    r = cc.shape[0]
    n = w_mod_l.shape[1]
    tn = D_MODEL
    return pl.pallas_call(
        _mod_kernel,
        out_shape=jax.ShapeDtypeStruct((r, n), F32),
        grid=(n // tn,),
        in_specs=[
            pl.BlockSpec((r, D_MODEL), lambda j: (0, 0)),
            pl.BlockSpec((D_MODEL, tn), lambda j: (0, j)),
            pl.BlockSpec((1, tn), lambda j: (0, j)),
        ],
        out_specs=pl.BlockSpec((r, tn), lambda j: (0, j)),
        compiler_params=_cparams(("arbitrary",)),
        name="mod_vectors",
    )(cc, w_mod_l, b_mod_l.reshape(1, n))


def _rope_cols(x, cos, sin_up, sin_dn):
    cols = []
    for c in range(x.shape[1] // LANES):
        xc = x[:, c * LANES:(c + 1) * LANES]
        up = pltpu.roll(xc, LANES - 16, axis=1)
        dn = pltpu.roll(xc, 16, axis=1)
        cols.append(xc * cos + up * sin_up + dn * sin_dn)
    return jnp.concatenate(cols, axis=1) if len(cols) > 1 else cols[0]


def _inproj_kernel(h_ref, mod_ref, cos_ref, su_ref, sd_ref,
                   wq_ref, wk_ref, wv_ref, wf_ref, wqc_ref, wkc_ref, wvc_ref,
                   qn_ref, kn_ref, eq_ref, ek_ref,
                   qa_o, ka_o, va_o, fb_o, qc_o, kc_o, vc_o):
    h = h_ref[...]
    shift = mod_ref[0, 0:1, :]
    scale = mod_ref[0, 1:2, :]
    u = (_layer_norm_rows(h) * (1.0 + scale) + shift).astype(BF16)
    cos = cos_ref[...]
    s_up = su_ref[...]
    s_dn = sd_ref[...]

    q = _dot(u, wq_ref[...])
    ms = _dot_split(q * q, eq_ref[...])
    q = q * lax.rsqrt(ms + RMS_EPS) * qn_ref[...]
    qa_o[...] = (_rope_cols(q, cos, s_up, s_dn) * _Q_SCALE).astype(BF16)

    k = _dot(u, wk_ref[...])
    ms = _dot_split(k * k, ek_ref[...])
    k = k * lax.rsqrt(ms + RMS_EPS) * kn_ref[...]
    ka_o[0] = jnp.transpose(_rope_cols(k, cos, s_up, s_dn)).astype(BF16)

    va_o[...] = _dot(u, wv_ref[...]).astype(BF16)
    fb_o[...] = _dot(u, wf_ref[...]).astype(BF16)
    qc = _dot(u, wqc_ref[...])
    qc_o[...] = (_rope_cols(qc, cos, s_up, s_dn) * _Q_SCALE).astype(BF16)
    kc = _dot(u, wkc_ref[...])
    kc_o[0] = jnp.transpose(_rope_cols(kc, cos, s_up, s_dn)).astype(BF16)
    vc_o[...] = _dot(u, wvc_ref[...]).astype(BF16)


def _inproj_call(h_all, mod, tabs, wts, nb, ntok):
    rows = h_all.shape[0]
    tpb = ntok // ROW_TILE
    cos, s_up, s_dn = tabs

    def full(a):
        return pl.BlockSpec(a.shape, lambda i: (0,) * a.ndim)

    def rowspec(w):
        return pl.BlockSpec((ROW_TILE, w), lambda i: (i, 0))

    tabspec = pl.BlockSpec((ROW_TILE, LANES), lambda i: (i % tpb, 0))
    modspec = pl.BlockSpec((1, 8, D_MODEL),
                           lambda i: (jnp.where(i % tpb == 0, nb, i // tpb), 0, 0))
    widths = (A_Q_W, 2 * A_KV_W, 2 * A_KV_W, B_W, C_QK_W, C_QK_W, C_V_W)
    transposed = (1, 5)

    def oshape(n, w):
        return (nb, w, ntok) if n in transposed else (rows, w)

    def ospec(n, w):
        if n in transposed:
            return pl.BlockSpec((1, w, ROW_TILE), lambda i: (i // tpb, 0, i % tpb))
        return rowspec(w)

    return pl.pallas_call(
        _inproj_kernel,
        out_shape=[jax.ShapeDtypeStruct(oshape(n, w), BF16) for n, w in enumerate(widths)],
        grid=(rows // ROW_TILE,),
        in_specs=[rowspec(D_MODEL), modspec, tabspec, tabspec, tabspec]
                 + [full(w) for w in wts],
        out_specs=[ospec(n, w) for n, w in enumerate(widths)],
        compiler_params=_cparams(("parallel",)),
        name="in_projection",
    )(h_all, mod, cos, s_up, s_dn, *wts)


def _softmax_parts(s):
    m = jnp.max(s, axis=-1, keepdims=True)
    e = jnp.exp2(s - m)
    return e, jnp.sum(e, axis=-1, keepdims=True)


def _query_rows(q_refs):
    rows = [r[0] for r in q_refs]
    return rows[0] if len(rows) == 1 else jnp.concatenate(rows, axis=0)


def _gqa_kernel(*refs):
    q_refs, (k_ref, v_ref, o_ref) = refs[:-3], refs[-3:]
    q = _query_rows(q_refs)
    lane = lax.broadcasted_iota(jnp.int32, (1, LANES), 1)
    low = lane < HEAD_DIM
    for c in range(A_Q_W // LANES):
        kvh = (2 * c) // (A_Q_HEADS // A_KV_HEADS)
        qc = q[:, c * LANES:(c + 1) * LANES]
        kk = k_ref[0, kvh * LANES:(kvh + 1) * LANES, :]
        vv = v_ref[0, :, kvh * LANES:(kvh + 1) * LANES]
        halves = []
        for keep in (low, jnp.logical_not(low)):
            qm = jnp.where(keep, qc, jnp.zeros_like(qc))
            e, l = _softmax_parts(_dot(qm, kk))
            halves.append(_dot(e.astype(BF16), vv) * (1.0 / l))
        o_ref[0, :, c * LANES:(c + 1) * LANES] = jnp.where(low, halves[0], halves[1]).astype(BF16)


def _diff_kernel(lam_ref, *refs, out_scale):
    q_refs, (k_ref, v_ref, g_ref, o_ref) = refs[:-4], refs[-4:]
    q = _query_rows(q_refs)
    lane = lax.broadcasted_iota(jnp.int32, (1, LANES), 1)
    low = lane < HEAD_DIM
    lam = lam_ref[0, 0]
    for hd in range(C_HEADS):
        sl = slice(hd * LANES, (hd + 1) * LANES)
        qc = q[:, sl]
        kk = k_ref[0, sl, :]
        vv = v_ref[0, :, sl]
        q1 = jnp.where(low, qc, jnp.zeros_like(qc))
        q2 = jnp.where(low, jnp.zeros_like(qc), qc)
        e1, l1 = _softmax_parts(_dot(q1, kk))
        e2, l2 = _softmax_parts(_dot(q2, kk))
        o = _dot(e1.astype(BF16), vv) * (1.0 / l1) - _dot(e2.astype(BF16), vv) * (lam / l2)
        ms = jnp.mean(o * o, axis=-1, keepdims=True)
        o = o * lax.rsqrt(ms + RMS_EPS) * g_ref[...] * out_scale
        o_ref[0, :, sl] = o.astype(BF16)


def _attn_call(kernel_fn, name, head, tail, q, kt, v, nb, ntok, wq, wkv, ctx_queries):
    cblk = CTX_LEN // ROW_TILE
    if ctx_queries:
        rows, nk, grid = CTX_LEN, CTX_LEN, (nb, 1)
        qspecs = [pl.BlockSpec((1, ROW_TILE, wq), lambda b, j: (b, 0, 0))]
    else:
        rows, nk, grid = ATTN_TILE, ntok, (nb, (ntok - CTX_LEN) // ATTN_TILE)
        per = ATTN_TILE // ROW_TILE
        qspecs = [pl.BlockSpec((1, ROW_TILE, wq), lambda b, j, r=r: (b, cblk + per * j + r, 0))
                  for r in range(per)]
    kspec = pl.BlockSpec((1, wkv, nk), lambda b, j: (b, 0, 0))
    vspec = pl.BlockSpec((1, nk, wkv), lambda b, j: (b, 0, 0))
    return pl.pallas_call(
        kernel_fn,
        out_shape=jax.ShapeDtypeStruct((nb, rows * grid[1], wq), BF16),
        grid=grid,
        in_specs=[sp for _, sp in head] + qspecs + [kspec, vspec] + [sp for _, sp in tail],
        out_specs=pl.BlockSpec((1, rows, wq), lambda b, j: (b, j, 0)),
        compiler_params=_cparams(("parallel", "arbitrary")),
        name=name,
    )(*[a for a, _ in head], *([q] * len(qspecs)), kt, v, *[a for a, _ in tail])


def _attn_a_call(qa, ka, va, nb, ntok, ctx_queries):
    return _attn_call(_gqa_kernel, "gqa_attention_ctx" if ctx_queries else "gqa_attention", [], [],
                      qa, ka, va, nb, ntok, A_Q_W, 2 * A_KV_W, ctx_queries)


def _attn_c_call(lam, qc, kc, vc, subln, nb, ntok, ctx_queries, out_scale):
    head = [(lam, pl.BlockSpec(memory_space=pltpu.SMEM))]
    tail = [(subln, pl.BlockSpec((1, C_V_DIM), lambda b, j: (0, 0)))]
    return _attn_call(functools.partial(_diff_kernel, out_scale=out_scale),
                      "diff_attention_ctx" if ctx_queries else "diff_attention", head, tail,
                      qc, kc, vc, nb, ntok, C_QK_W, C_QK_W, ctx_queries)


def _fourier_kernel(f_ref, mir_ref, chan_ref, pos_ref, posc_ref, o_ref, g_ref, mid_ref,
                    *, ntok, first_tile):
    j = pl.program_id(1) + first_tile
    nlat = ntok - CTX_LEN
    half = nlat // 2

    if first_tile == 0:
        @pl.when(j == 0)
        def _():
            g = _dot(f_ref[0, 0:CTX_LEN, :], chan_ref[...])
            gc = jnp.concatenate([g[:, :B_W], g[:, B_W:]], axis=0).astype(BF16)
            y = _dot(posc_ref[...], gc) * (1.0 / math.sqrt(CTX_LEN * F_GROUP_W))
            o_ref[0] = y.astype(BF16)

    @pl.when(j == 1)
    def _():
        g = _dot(f_ref[0, CTX_LEN:CTX_LEN + half, :], chan_ref[...])
        gm = _dot(mir_ref[0], chan_ref[...])
        has_partner = lax.broadcasted_iota(jnp.int32, (half, 1), 0) > 0
        g_ref[0:half, :] = (g[:, :B_W] + jnp.where(has_partner, gm[:, :B_W], 0.0)).astype(BF16)
        g_ref[half:2 * half, :] = (g[:, B_W:] - gm[:, B_W:]).astype(BF16)
        gmid = _dot(f_ref[0, CTX_LEN + half:CTX_LEN + half + 8, :], chan_ref[...])
        mid_ref[...] = gmid[0:1, :B_W]

    @pl.when(j >= 1)
    def _():
        odd = lax.broadcasted_iota(jnp.int32, (ROW_TILE, 1), 0) & 1
        sign = (1 - 2 * odd).astype(F32)
        y = _dot(pos_ref[...], g_ref[...]) + sign * mid_ref[...]
        o_ref[0] = (y * (1.0 / math.sqrt(nlat * F_GROUP_W))).astype(BF16)


def _fourier_call(fb, chan, pos, posc, nb, ntok, first_tile):
    tpb = ntok // ROW_TILE
    nlat = ntok - CTX_LEN
    half = nlat // 2
    lat = fb[:, CTX_LEN:, :]
    mirror = jnp.concatenate([lat[:, :1], jnp.flip(lat[:, half + 1:], axis=1)], axis=1)
    return pl.pallas_call(
        functools.partial(_fourier_kernel, ntok=ntok, first_tile=first_tile),
        out_shape=jax.ShapeDtypeStruct((nb, ntok - first_tile * ROW_TILE, B_W), BF16),
        grid=(nb, tpb - first_tile),
        in_specs=[
            pl.BlockSpec((1, ntok, B_W), lambda b, j: (b, 0, 0)),
            pl.BlockSpec((1, half, B_W), lambda b, j: (b, 0, 0)),
            pl.BlockSpec(chan.shape, lambda b, j: (0, 0)),
            pl.BlockSpec((ROW_TILE, 2 * half),
                         lambda b, j: (jnp.maximum(j + first_tile - 1, 0), 0)),
            pl.BlockSpec(posc.shape, lambda b, j: (0, 0)),
        ],
        out_specs=pl.BlockSpec((1, ROW_TILE, B_W), lambda b, j: (b, j, 0)),
        scratch_shapes=[pltpu.VMEM((2 * half, B_W), BF16), pltpu.VMEM((1, B_W), F32)],
        compiler_params=_cparams(("parallel", "arbitrary")),
        name="fourier_mix",
    )(fb, mirror, chan, pos, posc)


def _dft_tables(n, cols):
    n1 = 32
    n0 = n // n1
    k = np.arange(n, dtype=np.int64)
    a = 2.0 * np.pi * ((k[:, None] * np.arange(n1)[None, :] * n0) % n) / n
    b = 2.0 * np.pi * ((k[:, None] * np.arange(n0)[None, :]) % n) / n
    ca, sa = jnp.asarray(np.cos(a), F32)[:, :, None], jnp.asarray(np.sin(a), F32)[:, :, None]
    cb, sb = jnp.asarray(np.cos(b), F32)[:, None, :], jnp.asarray(np.sin(b), F32)[:, None, :]
    cos = (ca * cb - sa * sb).reshape(n, n)[:, :cols]
    sin = (sa * cb + ca * sb).reshape(n, n)[:, :cols]
    return jnp.concatenate([cos, -sin], axis=1).astype(BF16)


def _channel_table():
    c = np.arange(F_GROUP_W)
    ang = 2.0 * np.pi * ((c[:, None] * c[None, :]) % F_GROUP_W) / F_GROUP_W
    eye = np.eye(F_GROUPS)
    cos = np.kron(eye, np.cos(ang))
    sin = np.kron(eye, np.sin(ang))
    return jnp.asarray(np.concatenate([cos, sin], axis=1), F32).astype(BF16)


def _store_packed_slabs(dst_ref, u):
    rows = u.shape[0]
    for jb in range(PSLAB):
        lo = u[:, 2 * jb * LANES:(2 * jb + 1) * LANES]
        hi = u[:, (2 * jb + 1) * LANES:(2 * jb + 2) * LANES]
        dst_ref[pl.ds(jb, rows, stride=PSLAB), :] = pltpu.pack_elementwise([lo, hi], packed_dtype=BF16)


def _load_packed_slabs(src_ref, rows):
    out = []
    for jb in range(PSLAB):
        words = src_ref[pl.ds(jb, rows, stride=PSLAB), :]
        out.append((pltpu.unpack_elementwise(words, index=0, packed_dtype=BF16, unpacked_dtype=F32),
                    pltpu.unpack_elementwise(words, index=1, packed_dtype=BF16, unpacked_dtype=F32)))
    return out


def _mixout_kernel(h_ref, mod_ref, oa_ref, oa_ctx_ref, ob_ref, oc_ref, oc_ctx_ref,
                   wg_ref, wa_ref, wb_ref, wc_ref, wo_ref, g1_ref, b1_ref, wr_ref,
                   h1_o, xp_o, lg_o, *, alpha, ctx_every):
    h = h_ref[...]
    m = mod_ref[0]
    u = (_layer_norm_rows(h) * (1.0 + m[1:2, :]) + m[0:1, :]).astype(BF16)
    oa, oc = oa_ref[...], oc_ref[...]
    if ctx_every:
        is_ctx = pl.program_id(0) % ctx_every == 0
        oa = jnp.where(is_ctx, oa_ctx_ref[...], oa)
        oc = jnp.where(is_ctx, oc_ctx_ref[...], oc)
    y = None
    for n, (o, w_ref) in enumerate(((oa, wa_ref), (ob_ref[...], wb_ref), (oc, wc_ref))):
        gate = jax.nn.sigmoid(_dot(u, wg_ref[:, n * D_MODEL:(n + 1) * D_MODEL]))
        t = gate * _dot(o, w_ref[...])
        y = t if y is None else y + t
    z = _dot(y.astype(BF16), wo_ref[...])
    h1 = _layer_norm_rows(alpha * h + m[2:3, :] * z) * g1_ref[...] + b1_ref[...]
    h1_o[...] = h1
    u2 = _layer_norm_rows(h1) * (1.0 + m[4:5, :]) + m[3:4, :]
    _store_packed_slabs(xp_o, u2)
    lg_o[...] = _dot_nt(wr_ref[...], u2.astype(BF16))


def _tile_maps(nb, ntok, lat_only):
    tpb = ntok // ROW_TILE
    if lat_only:
        lpb = tpb - 1
        n_tiles = nb * lpb
        src = lambda i: (i // lpb) * tpb + 1 + i % lpb
        modi = lambda i: i // lpb
    else:
        n_tiles = nb * tpb
        src = lambda i: i
        modi = lambda i: jnp.where(i % tpb == 0, nb, i // tpb)
    return n_tiles, src, modi


def _mixout_call(h_all, mod, oa, oa_ctx, ob, oc, oc_ctx, wts, nb, ntok, lat_only, alpha):
    n_tiles, src, modi = _tile_maps(nb, ntok, lat_only)
    rows_out = n_tiles * ROW_TILE
    tpb = ntok // ROW_TILE
    lpb = tpb - CTX_LEN // ROW_TILE
    if lat_only:
        lat_blk = lambda i: i
        ctx_blk = lambda i: 0
    else:
        lat_blk = lambda i: (i // tpb) * lpb + jnp.maximum(i % tpb - 1, 0)
        ctx_blk = lambda i: i // tpb

    def full(a):
        return pl.BlockSpec(a.shape, lambda i: (0,) * a.ndim)

    def inrow(w):
        return pl.BlockSpec((ROW_TILE, w), lambda i: (src(i), 0))

    def outrow(w):
        return pl.BlockSpec((ROW_TILE, w), lambda i: (i, 0))

    modspec = pl.BlockSpec((1, 8, D_MODEL), lambda i: (modi(i), 0, 0))
    def latrow(w):
        return pl.BlockSpec((ROW_TILE, w), lambda i: (lat_blk(i), 0))

    def ctxrow(w):
        return pl.BlockSpec((ROW_TILE, w), lambda i: (ctx_blk(i), 0))

    return pl.pallas_call(
        functools.partial(_mixout_kernel, alpha=alpha, ctx_every=0 if lat_only else tpb),
        out_shape=[jax.ShapeDtypeStruct((rows_out, D_MODEL), F32),
                   jax.ShapeDtypeStruct((rows_out * PSLAB, LANES), U32),
                   jax.ShapeDtypeStruct((N_EXPERTS, rows_out), F32)],
        grid=(n_tiles,),
        in_specs=[inrow(D_MODEL), modspec, latrow(A_Q_W), ctxrow(A_Q_W), outrow(B_W),
                  latrow(C_V_W), ctxrow(C_V_W)]
                 + [full(w) for w in wts],
        out_specs=[outrow(D_MODEL), pl.BlockSpec((ROW_TILE * PSLAB, LANES), lambda i: (i, 0)),
                   pl.BlockSpec((N_EXPERTS, ROW_TILE), lambda i: (0, i))],
        compiler_params=_cparams(("parallel",)),
        name="mixer_output",
    )(h_all, mod, oa, oa_ctx, ob, oc, oc_ctx, *wts)


def _moe_kernel(first_ref, count_ref, total_ref,
                tab_hbm, xp_ref, wg_ref, wu_ref, wd_ref,
                out_hbm,
                tab, gath, wgu_s, wd_s, ybuf, sem, tsem, *, t_pad, n_tok, lead_row):
    nfl = BLOCKS_IN_FLIGHT
    tmask = TAB_SLOTS - 1
    e = pl.program_id(0)
    total = total_ref[0]

    def row_copy(s, m, dst4):
        return pltpu.make_async_copy(ybuf.at[s, pl.ds(PSLAB * m, PSLAB), :],
                                     out_hbm.at[pl.ds(pl.multiple_of(dst4, PSLAB), PSLAB), :],
                                     sem.at[s])

    def wait_rows(s):
        for m in range(MOE_BLOCK):
            row_copy(s, m, 0).wait()

    def send_rows(ts, ys):
        for m in range(MOE_BLOCK):
            row_copy(ys, m, tab[ts, 0, MOE_BLOCK + m]).start(priority=m % 2)

    def tab_copy(row, s):
        return pltpu.make_async_copy(tab_hbm.at[row], tab.at[s], tsem.at[s])

    def gather_rows(ts, gs):
        for m in range(MOE_BLOCK):
            t4 = pl.multiple_of(tab[ts, 0, m], PSLAB)
            gath[gs, PSLAB * m:PSLAB * (m + 1), :] = xp_ref[pl.ds(t4, PSLAB), :]

    @pl.when(e == 0)
    def _():
        ybuf[...] = jnp.zeros_like(ybuf)
        tab_copy(lead_row, tmask).start()
        for r in range(TAB_AHEAD):
            tab_copy(r, r).start()
        for b in range(nfl - 1):
            for m in range(MOE_BLOCK):
                q = b * MOE_BLOCK + m
                row_copy(b, m, ((q % TOP_K) * t_pad + n_tok + q // TOP_K) * PSLAB).start(priority=m % 2)
        tab_copy(lead_row, tmask).wait()
        tab_copy(0, 0).wait()
        gather_rows(0, 0)

    wgu_s[:, 0:EXPERT_FF] = wg_ref[0, 0].astype(BF16)
    wgu_s[:, EXPERT_FF:2 * EXPERT_FF] = wu_ref[0, 0].astype(BF16)
    wd_s[...] = wd_ref[0, 0].astype(BF16)

    def block(j, carry):
        g = first_ref[e] + j
        tab_copy(g + 1, (g + 1) & tmask).wait()
        tab_copy(g + TAB_AHEAD, (g + TAB_AHEAD) & tmask).start()
        ys = lax.rem(g, nfl)
        wait_rows(ys)

        send_rows((g - 1) & tmask, lax.rem(g + nfl - 1, nfl))
        gather_rows((g + 1) & tmask, (g + 1) & 1)

        acc = None
        for jb, (lo, hi) in enumerate(_load_packed_slabs(gath.at[g & 1], MOE_BLOCK)):
            a = jnp.concatenate([lo, hi], axis=1).astype(BF16)
            t = _dot(a, wgu_s[jb * 2 * LANES:(jb + 1) * 2 * LANES, :])
            acc = t if acc is None else acc + t
        gt = acc[:, 0:EXPERT_FF]
        act = (gt * jax.nn.sigmoid(gt)) * acc[:, EXPERT_FF:2 * EXPERT_FF]
        y = _dot(act.astype(BF16), wd_s[...])
        _store_packed_slabs(ybuf.at[ys], y)
        return carry

    lax.fori_loop(0, count_ref[e], block, 0)

    @pl.when(e == pl.num_programs(0) - 1)
    def _():
        send_rows((total - 1) & tmask, lax.rem(total + nfl - 1, nfl))
        for s in range(nfl):
            wait_rows(s)
        for r in range(1, TAB_AHEAD):
            tab_copy(0, (total + r) & tmask).wait()


def _moe_call(first_blk, n_blk, total, table, xp4, wg, wu, wd, layer, n_tok):
    t_pad = n_tok + PAD_ROWS
    lead_row = table.shape[0] - 1
    wspec = lambda r, c: pl.BlockSpec((1, 1, r, c), lambda e, fb, nb, tt: (layer, e, 0, 0))
    grid_spec = pltpu.PrefetchScalarGridSpec(
        num_scalar_prefetch=3,
        grid=(N_EXPERTS,),
        in_specs=[
            pl.BlockSpec(memory_space=pl.ANY),
            pl.BlockSpec(memory_space=pltpu.VMEM),
            wspec(D_MODEL, EXPERT_FF), wspec(D_MODEL, EXPERT_FF), wspec(EXPERT_FF, D_MODEL),
        ],
        out_specs=pl.BlockSpec(memory_space=pl.ANY),
        scratch_shapes=[
            pltpu.SMEM((TAB_SLOTS, 1, 2 * MOE_BLOCK), jnp.int32),
            pltpu.VMEM((2, MOE_BLOCK * PSLAB, LANES), U32),
            pltpu.VMEM((D_MODEL, 2 * EXPERT_FF), BF16),
            pltpu.VMEM((EXPERT_FF, D_MODEL), BF16),
            pltpu.VMEM((BLOCKS_IN_FLIGHT, MOE_BLOCK * PSLAB, LANES), U32),
            pltpu.SemaphoreType.DMA((BLOCKS_IN_FLIGHT,)),
            pltpu.SemaphoreType.DMA((TAB_SLOTS,)),
        ],
    )
    return pl.pallas_call(
        functools.partial(_moe_kernel, t_pad=t_pad, n_tok=n_tok, lead_row=lead_row),
        out_shape=jax.ShapeDtypeStruct((TOP_K * t_pad * PSLAB, LANES), U32),
        grid_spec=grid_spec,
        compiler_params=_cparams(("arbitrary",)),
        name="routed_experts",
    )(first_blk, n_blk, total, table, xp4, wg, wu, wd)


def _combine_kernel(h1_ref, mod_ref, y8_ref, rw_ref, sg_ref, su_ref, sd_ref, g2_ref, b2_ref, o_ref,
                    *, alpha):
    h1 = h1_ref[...]
    m = mod_ref[0]
    u2 = (_layer_norm_rows(h1) * (1.0 + m[4:5, :]) + m[3:4, :]).astype(BF16)
    g = _dot(u2, sg_ref[...])
    act = (g * jax.nn.sigmoid(g)) * _dot(u2, su_ref[...])
    f = _dot(act.astype(BF16), sd_ref[...])
    tm = h1.shape[0]
    rw = rw_ref[...]
    wk = [jnp.broadcast_to(rw[:, kk:kk + 1], (tm, LANES)) for kk in range(TOP_K)]
    cols = [None] * (2 * PSLAB)
    for kk in range(TOP_K):
        for jb, pair in enumerate(_load_packed_slabs(y8_ref.at[kk], tm)):
            for half in range(2):
                t = wk[kk] * pair[half]
                c = 2 * jb + half
                cols[c] = t if cols[c] is None else cols[c] + t
    f = f + jnp.concatenate(cols, axis=1)
    o_ref[...] = _layer_norm_rows(alpha * h1 + m[5:6, :] * f) * g2_ref[...] + b2_ref[...]


def _combine_call(h1, mod, y8, rw, wts, rows_per_group, group_is_ctx_first, nb, alpha):
    rows = h1.shape[0]
    tm = COMBINE_TILE
    tpg = rows_per_group // tm
    if group_is_ctx_first:
        first = CTX_LEN // tm
        modi = lambda i: jnp.where(i % tpg < first, nb, i // tpg)
    else:
        modi = lambda i: i // tpg

    def full(a):
        return pl.BlockSpec(a.shape, lambda i: (0,) * a.ndim)

    return pl.pallas_call(
        functools.partial(_combine_kernel, alpha=alpha),
        out_shape=jax.ShapeDtypeStruct((rows, D_MODEL), F32),
        grid=(rows // tm,),
        in_specs=[pl.BlockSpec((tm, D_MODEL), lambda i: (i, 0)),
                  pl.BlockSpec((1, 8, D_MODEL), lambda i: (modi(i), 0, 0)),
                  pl.BlockSpec((TOP_K, tm * PSLAB, LANES), lambda i: (0, i, 0)),
                  pl.BlockSpec((tm, TOP_K), lambda i: (i, 0))]
                 + [full(w) for w in wts],
        out_specs=pl.BlockSpec((tm, D_MODEL), lambda i: (i, 0)),
        compiler_params=_cparams(("parallel",)),
        name="moe_combine",
    )(h1, mod, y8, rw, *wts)


_BIG_LANE = 1 << 30


def _route_kernel(lg_ref, b_ref, idx_o, w_o, rank_o, cnt_o, carry):
    i = pl.program_id(0)

    @pl.when(i == 0)
    def _():
        carry[...] = jnp.zeros_like(carry)

    tm = lg_ref.shape[1]
    gsize = N_EXPERTS // N_GROUPS
    scores = jax.nn.sigmoid(lg_ref[...])
    biased = scores + b_ref[...]
    eid = lax.broadcasted_iota(jnp.int32, (N_EXPERTS, tm), 0)

    b3 = biased.reshape(N_GROUPS, gsize, tm)
    in_g = lax.broadcasted_iota(jnp.int32, (N_GROUPS, gsize, tm), 1)
    m1 = jnp.max(b3, axis=1, keepdims=True)
    first = jnp.min(jnp.where(b3 == m1, in_g, _BIG_LANE), axis=1, keepdims=True)
    m2 = jnp.max(jnp.where(in_g == first, -jnp.inf, b3), axis=1, keepdims=True)
    gscore = (m1 + m2).reshape(N_GROUPS, tm)

    gid = lax.broadcasted_iota(jnp.int32, (N_GROUPS, tm), 0)
    beaten = jnp.zeros((N_GROUPS, tm), jnp.int32)
    for g in range(N_GROUPS):
        sg = gscore[g:g + 1, :]
        ahead = jnp.logical_or(sg > gscore, jnp.logical_and(sg == gscore, g < gid))
        beaten = beaten + ahead.astype(jnp.int32)
    keep = jnp.broadcast_to((beaten < TOPK_GROUPS).astype(jnp.int32).reshape(N_GROUPS, 1, tm),
                            (N_GROUPS, gsize, tm)).reshape(N_EXPERTS, tm)
    masked = jnp.where(keep > 0, biased, -jnp.inf)

    idxs, ws, hots = [], [], []
    for _ in range(TOP_K):
        m = jnp.max(masked, axis=0, keepdims=True)
        ix = jnp.min(jnp.where(masked == m, eid, _BIG_LANE), axis=0, keepdims=True)
        hot = eid == ix
        idxs.append(ix)
        ws.append(jnp.sum(jnp.where(hot, scores, 0.0), axis=0, keepdims=True))
        hots.append(hot)
        masked = jnp.where(hot, -jnp.inf, masked)
    wsum = ws[0]
    for r in range(1, TOP_K):
        wsum = wsum + ws[r]
    idx_o[...] = jnp.concatenate(idxs, axis=0)
    w_rows = jnp.concatenate([wr / wsum * ROUTED_SCALE for wr in ws], axis=0)

    eye = (lax.broadcasted_iota(jnp.int32, (tm, tm), 0)
           == lax.broadcasted_iota(jnp.int32, (tm, tm), 1)).astype(BF16)
    w_cols = jnp.zeros((tm, TOP_K), F32)
    rest = w_rows
    for _ in range(3):
        part = rest.astype(BF16)
        rest = rest - part.astype(F32)
        w_cols = w_cols + _dot_nt(eye, part)
    w_o[...] = w_cols

    sel = jnp.zeros((N_EXPERTS, tm), F32)
    for hot in hots:
        sel = sel + hot.astype(F32)
    sel = sel.astype(BF16)
    earlier = (lax.broadcasted_iota(jnp.int32, (tm, tm), 0)
               < lax.broadcasted_iota(jnp.int32, (tm, tm), 1)).astype(BF16)
    prefix = _dot(sel, earlier) + carry[:, 0:1]
    rank_o[...] = jnp.concatenate(
        [jnp.sum(jnp.where(hot, prefix, 0.0), axis=0, keepdims=True) for hot in hots],
        axis=0).astype(jnp.int32)
    carry[...] = carry[...] + _dot(sel, jnp.ones((tm, LANES), BF16))
    cnt_o[...] = carry[...]


def _route_call(logits_t, b_r):
    t = logits_t.shape[1]
    tm = ROW_TILE
    kt = pl.BlockSpec((TOP_K, tm), lambda i: (0, i))
    return pl.pallas_call(
        _route_kernel,
        out_shape=[jax.ShapeDtypeStruct((TOP_K, t), jnp.int32),
                   jax.ShapeDtypeStruct((t, TOP_K), F32),
                   jax.ShapeDtypeStruct((TOP_K, t), jnp.int32),
                   jax.ShapeDtypeStruct((N_EXPERTS, LANES), F32)],
        grid=(t // tm,),
        in_specs=[pl.BlockSpec((N_EXPERTS, tm), lambda i: (0, i)),
                  pl.BlockSpec((N_EXPERTS, 1), lambda i: (0, 0))],
        out_specs=[kt, pl.BlockSpec((tm, TOP_K), lambda i: (i, 0)), kt,
                   pl.BlockSpec((N_EXPERTS, LANES), lambda i: (0, 0))],
        scratch_shapes=[pltpu.VMEM((N_EXPERTS, LANES), F32)],
        compiler_params=_cparams(("arbitrary",)),
        name="route_topk",
    )(logits_t, b_r.astype(F32).reshape(N_EXPERTS, 1))


def _dest_kernel(idx_ref, rank_ref, start_ref, o_ref):
    tm = idx_ref.shape[1]
    eid = lax.broadcasted_iota(jnp.int32, (N_EXPERTS, tm), 0)
    idx = idx_ref[...]
    start = start_ref[...]
    rows = [jnp.sum(jnp.where(eid == idx[r:r + 1, :], start, 0), axis=0, keepdims=True)
            for r in range(TOP_K)]
    o_ref[...] = jnp.concatenate(rows, axis=0) + rank_ref[...]


def _dest_call(idx, rank, pad_start):
    t = idx.shape[1]
    tm = ROW_TILE
    blk = pl.BlockSpec((TOP_K, tm), lambda i: (0, i))
    return pl.pallas_call(
        _dest_kernel,
        out_shape=jax.ShapeDtypeStruct((TOP_K, t), jnp.int32),
        grid=(t // tm,),
        in_specs=[blk, blk, pl.BlockSpec((N_EXPERTS, 1), lambda i: (0, 0))],
        out_specs=blk,
        compiler_params=_cparams(("parallel",)),
        name="slot_of_assignment",
    )(idx, rank, pad_start.reshape(N_EXPERTS, 1))


def _dispatch(idx, rank, counts, n_tok):
    n_assign = n_tok * TOP_K
    used_max = (n_assign + N_EXPERTS * (MOE_BLOCK - 1) + MOE_BLOCK - 1) // MOE_BLOCK
    nblk = -(-(used_max + TAB_AHEAD + 1) // BLOCKS_IN_FLIGHT) * BLOCKS_IN_FLIGHT
    n_slots = nblk * MOE_BLOCK
    counts = counts[:, 0].astype(jnp.int32)
    padded = (counts + MOE_BLOCK - 1) // MOE_BLOCK * MOE_BLOCK
    pad_end = jnp.cumsum(padded)
    pad_start = pad_end - padded
    dest = _dest_call(idx, rank, pad_start)
    assign = (jnp.arange(n_tok, dtype=jnp.int32)[None, :] * TOP_K
              + jnp.arange(TOP_K, dtype=jnp.int32)[:, None])
    assert (nblk - 1) % BLOCKS_IN_FLIGHT == BLOCKS_IN_FLIGHT - 1
    pad_a = n_assign + jnp.arange(n_slots, dtype=jnp.int32) % (BLOCKS_IN_FLIGHT * MOE_BLOCK)
    slot_a = pad_a.at[dest.reshape(-1)].set(assign.reshape(-1), unique_indices=True)
    t_pad = n_tok + PAD_ROWS
    tok = lax.shift_right_logical(slot_a, K_SHIFT)
    tok4 = jnp.minimum(tok, n_tok - 1) * PSLAB
    dst4 = ((slot_a & (TOP_K - 1)) * t_pad + tok) * PSLAB
    table = jnp.concatenate([tok4.reshape(nblk, 1, MOE_BLOCK), dst4.reshape(nblk, 1, MOE_BLOCK)], axis=2)
    total = (pad_end[-1] // MOE_BLOCK).astype(jnp.int32).reshape(1)
    return pad_start // MOE_BLOCK, padded // MOE_BLOCK, total, table


def _rope_tables(s, ntok):
    rows_n = s // GRID_W
    row = jnp.repeat(jnp.arange(rows_n, dtype=F32), GRID_W)
    col = jnp.tile(jnp.arange(GRID_W, dtype=F32), rows_n)
    axis_dim = HEAD_DIM // 2
    inv = jnp.power(ROPE_THETA, -jnp.arange(0, axis_dim, 2, dtype=F32) / axis_dim)
    ar = row[:, None] * inv[None]
    ac = col[:, None] * inv[None]
    ang = jnp.concatenate([ar, ar, ac, ac], -1)
    cos, sin = jnp.cos(ang), jnp.sin(ang)
    quarter = (jnp.arange(HEAD_DIM) // 16) % 2
    s_up = jnp.where(quarter == 0, -sin, 0.0)
    s_dn = jnp.where(quarter == 1, sin, 0.0)
    nctx = ntok - s

    def expand(t, ctx_val):
        t = jnp.concatenate([jnp.full((nctx, HEAD_DIM), ctx_val, F32), t], axis=0)
        return jnp.tile(t, (1, LANES // HEAD_DIM))

    return expand(cos, 1.0), expand(s_up, 0.0), expand(s_dn, 0.0)


def _head_mean_matrix(width):
    hid = np.arange(width) // HEAD_DIM
    return jnp.asarray((hid[:, None] == hid[None, :]).astype(np.float32) / HEAD_DIM).astype(BF16)


def _dup_heads(a, n_heads):
    parts = []
    for hd in range(n_heads):
        p = a[..., hd * HEAD_DIM:(hd + 1) * HEAD_DIM]
        parts += [p, p]
    return jnp.concatenate(parts, axis=-1)


def kernel(x, c, ctx, c_ctx, w_mod, b_mod, w_in, qn_a, kn_a, lam_q1, lam_k1, lam_q2, lam_k2, subln_c, w_br_a, w_br_b, w_br_c, w_out, ln1_g, ln1_b, w_router, b_router, w_sh_gate, w_sh_up, w_sh_down, w_e_gate, w_e_up, w_e_down, ln2_g, ln2_b):
    nb, s, d = x.shape
    lc = ctx.shape[1]
    depth = w_mod.shape[0]
    assert d == D_MODEL and lc == CTX_LEN and s % ROW_TILE == 0 and s % GRID_W == 0
    ntok = lc + s
    alpha = (2 * depth) ** 0.25

    tabs = _rope_tables(s, ntok)
    pos_lat = _dft_tables(s, s // 2)
    pos_ctx = _dft_tables(lc, lc)
    chan = _channel_table()
    e_q = _head_mean_matrix(A_Q_W)
    e_k = _head_mean_matrix(2 * A_KV_W)

    cc = jnp.concatenate([c, c_ctx[None, :]], axis=0)
    cc = jnp.pad(cc, ((0, (-(nb + 1)) % 8), (0, 0)))
    h_all = jnp.concatenate([ctx, x], axis=1).reshape(nb * ntok, d)

    offs = np.cumsum([0, A_Q_W, A_KV_W, A_KV_W, B_W, C_QK_W, C_QK_W, C_V_W, GATE_W])
    out = None
    for l in range(depth):
        last = l == depth - 1
        mod = _mod_call(cc, w_mod[l], b_mod[l])[:nb + 1].reshape(nb + 1, 6, d)
        mod = jnp.pad(mod, ((0, 0), (0, 2), (0, 0)))
        lam_init = 0.8 - 0.6 * math.exp(-0.3 * l)
        lam = (jnp.exp(jnp.sum(lam_q1[l].astype(F32) * lam_k1[l].astype(F32)))
               - jnp.exp(jnp.sum(lam_q2[l].astype(F32) * lam_k2[l].astype(F32)))) + lam_init
        lam = lam.reshape(1, 1).astype(F32)

        wl = w_in[l]
        seg = [wl[:, offs[i]:offs[i + 1]] for i in range(8)]
        in_wts = (seg[0].astype(BF16), _dup_heads(seg[1], A_KV_HEADS).astype(BF16),
                  _dup_heads(seg[2], A_KV_HEADS).astype(BF16), seg[3].astype(BF16),
                  seg[4].astype(BF16), seg[5].astype(BF16), seg[6].astype(BF16),
                  jnp.tile(qn_a[l].astype(F32), A_Q_HEADS).reshape(1, A_Q_W),
                  jnp.tile(kn_a[l].astype(F32), 2 * A_KV_HEADS).reshape(1, 2 * A_KV_W),
                  e_q, e_k)
        qa, ka, va, fb, qc, kc, vc = _inproj_call(h_all, mod, tabs, in_wts, nb, ntok)

        first_tile = 1 if last else 0
        r3 = lambda a: a.reshape(nb, ntok, a.shape[-1])
        subln = subln_c[l].astype(F32).reshape(1, C_V_DIM)
        attn_a = functools.partial(_attn_a_call, r3(qa), ka, r3(va), nb, ntok)
        attn_c = functools.partial(_attn_c_call, lam, r3(qc), kc, r3(vc), subln, nb, ntok)
        oa, oc = attn_a(False), attn_c(False, 1.0 - lam_init)
        oa_ctx, oc_ctx = (oa, oc) if last else (attn_a(True), attn_c(True, 1.0 - lam_init))
        ob = _fourier_call(r3(fb), chan, pos_lat, pos_ctx, nb, ntok, first_tile)

        mix_wts = (seg[7].astype(BF16), w_br_a[l].astype(BF16), w_br_b[l].astype(BF16),
                   w_br_c[l].astype(BF16), w_out[l].astype(BF16),
                   ln1_g[l].astype(F32).reshape(1, d), ln1_b[l].astype(F32).reshape(1, d),
                   w_router[l].T.astype(BF16))
        flat = lambda a: a.reshape(-1, a.shape[-1])
        h1, xp, logits = _mixout_call(h_all, mod, flat(oa), flat(oa_ctx), flat(ob), flat(oc),
                                      flat(oc_ctx), mix_wts, nb, ntok, last, alpha)

        n_tok = h1.shape[0]
        idx, rw, rank, counts = _route_call(logits, b_router[l])
        first_blk, n_blk, total, table = _dispatch(idx, rank, counts, n_tok)
        y8 = _moe_call(first_blk, n_blk, total, table, xp,
                       w_e_gate, w_e_up, w_e_down, l, n_tok)
        y8 = y8.reshape(TOP_K, (n_tok + PAD_ROWS) * PSLAB, LANES)

        comb_wts = (w_sh_gate[l].astype(BF16), w_sh_up[l].astype(BF16), w_sh_down[l].astype(BF16),
                    ln2_g[l].astype(F32).reshape(1, d), ln2_b[l].astype(F32).reshape(1, d))
        h2 = _combine_call(h1, mod, y8, rw, comb_wts, s if last else ntok, not last, nb, alpha)
        if last:
            out = h2.reshape(nb, s, d)
        else:
            h_all = h2
    return out
```

```python
import functools
import math

import numpy as np
import jax
import jax.numpy as jnp
from jax import lax
from jax.experimental import pallas as pl
from jax.experimental.pallas import tpu as pltpu

F32 = jnp.float32
BF16 = jnp.bfloat16
U32 = jnp.uint32

D_MODEL = 1024
CTX_LEN = 256
GRID_W = 64
HEAD_DIM = 64
ROPE_THETA = 10000.0
A_Q_HEADS = 8
A_KV_HEADS = 2
A_Q_W = A_Q_HEADS * HEAD_DIM
A_KV_W = A_KV_HEADS * HEAD_DIM
F_GROUPS = 4
F_GROUP_W = 128
B_W = F_GROUPS * F_GROUP_W
C_HEADS = 4
C_V_DIM = 2 * HEAD_DIM
C_QK_W = C_HEADS * 2 * HEAD_DIM
C_V_W = C_HEADS * C_V_DIM
N_BRANCH = 3
GATE_W = N_BRANCH * D_MODEL
N_EXPERTS = 256
TOP_K = 8
N_GROUPS = 8
TOPK_GROUPS = 4
EXPERT_FF = 256
SHARED_FF = 256
ROUTED_SCALE = 2.5
LN_EPS = 1e-5
RMS_EPS = 1e-6
K_SHIFT = TOP_K.bit_length() - 1
assert 1 << K_SHIFT == TOP_K

LANES = 128
PSLAB = D_MODEL // (2 * LANES)
ROW_TILE = 256
ATTN_TILE = 512
MOE_BLOCK = 128
BLOCKS_IN_FLIGHT = 4
TAB_AHEAD = 4
TAB_SLOTS = 8
PAD_ROWS = BLOCKS_IN_FLIGHT * MOE_BLOCK // TOP_K
COMBINE_TILE = 128
VMEM_LIMIT = 56 * 1024 * 1024

_Q_SCALE = HEAD_DIM ** -0.5 * math.log2(math.e)


def _cparams(sem):
    return pltpu.CompilerParams(dimension_semantics=sem, vmem_limit_bytes=VMEM_LIMIT)


def _dot(a, b):
    return jnp.dot(a, b, preferred_element_type=F32)


def _dot_nt(a, b):
    return lax.dot_general(a, b, (((1,), (1,)), ((), ())), preferred_element_type=F32)


def _layer_norm_rows(x):
    mu = jnp.mean(x, axis=-1, keepdims=True)
    xc = x - mu
    var = jnp.mean(xc * xc, axis=-1, keepdims=True)
    return xc * lax.rsqrt(var + LN_EPS)


def _dot_split(x, e):
    hi = x.astype(BF16)
    lo = (x - hi.astype(F32)).astype(BF16)
    return _dot(hi, e) + _dot(lo, e)


def _mod_kernel(c_ref, w_ref, b_ref, o_ref):
    c = c_ref[...]
    sc = c * jax.nn.sigmoid(c)
    o_ref[...] = _dot(sc.astype(BF16), w_ref[...].astype(BF16)) + b_ref[...]


def _mod_call(cc, w_mod_l, b_mod_l):
    r = cc.shape[0]
    n = w_mod_l.shape[1]
    tn = D_MODEL
    return pl.pallas_call(
        _mod_kernel,
        out_shape=jax.ShapeDtypeStruct((r, n), F32),
        grid=(n // tn,),
        in_specs=[
            pl.BlockSpec((r, D_MODEL), lambda j: (0, 0)),
            pl.BlockSpec((D_MODEL, tn), lambda j: (0, j)),
            pl.BlockSpec((1, tn), lambda j: (0, j)),
        ],
        out_specs=pl.BlockSpec((r, tn), lambda j: (0, j)),
        compiler_params=_cparams(("arbitrary",)),
        name="mod_vectors",
    )(cc, w_mod_l, b_mod_l.reshape(1, n))


def _rope_cols(x, cos, sin_up, sin_dn):
    cols = []
    for c in range(x.shape[1] // LANES):
        xc = x[:, c * LANES:(c + 1) * LANES]
        up = pltpu.roll(xc, LANES - 16, axis=1)
        dn = pltpu.roll(xc, 16, axis=1)
        cols.append(xc * cos + up * sin_up + dn * sin_dn)
    return jnp.concatenate(cols, axis=1) if len(cols) > 1 else cols[0]


def _inproj_kernel(h_ref, mod_ref, cos_ref, su_ref, sd_ref,
                   wq_ref, wk_ref, wv_ref, wf_ref, wqc_ref, wkc_ref, wvc_ref,
                   qn_ref, kn_ref, eq_ref, ek_ref,
                   qa_o, ka_o, va_o, fb_o, qc_o, kc_o, vc_o):
    h = h_ref[...]
    shift = mod_ref[0, 0:1, :]
    scale = mod_ref[0, 1:2, :]
    u = (_layer_norm_rows(h) * (1.0 + scale) + shift).astype(BF16)
    cos = cos_ref[...]
    s_up = su_ref[...]
    s_dn = sd_ref[...]

    q = _dot(u, wq_ref[...])
    ms = _dot_split(q * q, eq_ref[...])
    q = q * lax.rsqrt(ms + RMS_EPS) * qn_ref[...]
    qa_o[...] = (_rope_cols(q, cos, s_up, s_dn) * _Q_SCALE).astype(BF16)

    k = _dot(u, wk_ref[...])
    ms = _dot_split(k * k, ek_ref[...])
    k = k * lax.rsqrt(ms + RMS_EPS) * kn_ref[...]
    ka_o[0] = jnp.transpose(_rope_cols(k, cos, s_up, s_dn)).astype(BF16)

    va_o[...] = _dot(u, wv_ref[...]).astype(BF16)
    fb_o[...] = _dot(u, wf_ref[...]).astype(BF16)
    qc = _dot(u, wqc_ref[...])
    qc_o[...] = (_rope_cols(qc, cos, s_up, s_dn) * _Q_SCALE).astype(BF16)
    kc = _dot(u, wkc_ref[...])
    kc_o[0] = jnp.transpose(_rope_cols(kc, cos, s_up, s_dn)).astype(BF16)
    vc_o[...] = _dot(u, wvc_ref[...]).astype(BF16)


def _inproj_call(h_all, mod, tabs, wts, nb, ntok):
    rows = h_all.shape[0]
    tpb = ntok // ROW_TILE
    cos, s_up, s_dn = tabs

    def full(a):
        return pl.BlockSpec(a.shape, lambda i: (0,) * a.ndim)

    def rowspec(w):
        return pl.BlockSpec((ROW_TILE, w), lambda i: (i, 0))

    tabspec = pl.BlockSpec((ROW_TILE, LANES), lambda i: (i % tpb, 0))
    modspec = pl.BlockSpec((1, 8, D_MODEL),
                           lambda i: (jnp.where(i % tpb == 0, nb, i // tpb), 0, 0))
    widths = (A_Q_W, 2 * A_KV_W, 2 * A_KV_W, B_W, C_QK_W, C_QK_W, C_V_W)
    transposed = (1, 5)

    def oshape(n, w):
        return (nb, w, ntok) if n in transposed else (rows, w)

    def ospec(n, w):
        if n in transposed:
            return pl.BlockSpec((1, w, ROW_TILE), lambda i: (i // tpb, 0, i % tpb))
        return rowspec(w)

    return pl.pallas_call(
        _inproj_kernel,
        out_shape=[jax.ShapeDtypeStruct(oshape(n, w), BF16) for n, w in enumerate(widths)],
        grid=(rows // ROW_TILE,),
        in_specs=[rowspec(D_MODEL), modspec, tabspec, tabspec, tabspec]
                 + [full(w) for w in wts],
        out_specs=[ospec(n, w) for n, w in enumerate(widths)],
        compiler_params=_cparams(("parallel",)),
        name="in_projection",
    )(h_all, mod, cos, s_up, s_dn, *wts)


def _softmax_parts(s):
    m = jnp.max(s, axis=-1, keepdims=True)
    e = jnp.exp2(s - m)
    return e, jnp.sum(e, axis=-1, keepdims=True)


def _query_rows(q_refs):
    rows = [r[0] for r in q_refs]
    return rows[0] if len(rows) == 1 else jnp.concatenate(rows, axis=0)


def _gqa_kernel(*refs):
    q_refs, (k_ref, v_ref, o_ref) = refs[:-3], refs[-3:]
    q = _query_rows(q_refs)
    lane = lax.broadcasted_iota(jnp.int32, (1, LANES), 1)
    low = lane < HEAD_DIM
    for c in range(A_Q_W // LANES):
        kvh = (2 * c) // (A_Q_HEADS // A_KV_HEADS)
        qc = q[:, c * LANES:(c + 1) * LANES]
        kk = k_ref[0, kvh * LANES:(kvh + 1) * LANES, :]
        vv = v_ref[0, :, kvh * LANES:(kvh + 1) * LANES]
        halves = []
        for keep in (low, jnp.logical_not(low)):
            qm = jnp.where(keep, qc, jnp.zeros_like(qc))
            e, l = _softmax_parts(_dot(qm, kk))
            halves.append(_dot(e.astype(BF16), vv) * (1.0 / l))
        o_ref[0, :, c * LANES:(c + 1) * LANES] = jnp.where(low, halves[0], halves[1]).astype(BF16)


def _diff_kernel(lam_ref, *refs, out_scale):
    q_refs, (k_ref, v_ref, g_ref, o_ref) = refs[:-4], refs[-4:]
    q = _query_rows(q_refs)
    lane = lax.broadcasted_iota(jnp.int32, (1, LANES), 1)
    low = lane < HEAD_DIM
    lam = lam_ref[0, 0]
    for hd in range(C_HEADS):
        sl = slice(hd * LANES, (hd + 1) * LANES)
        qc = q[:, sl]
        kk = k_ref[0, sl, :]
        vv = v_ref[0, :, sl]
        q1 = jnp.where(low, qc, jnp.zeros_like(qc))
        q2 = jnp.where(low, jnp.zeros_like(qc), qc)
        e1, l1 = _softmax_parts(_dot(q1, kk))
        e2, l2 = _softmax_parts(_dot(q2, kk))
        o = _dot(e1.astype(BF16), vv) * (1.0 / l1) - _dot(e2.astype(BF16), vv) * (lam / l2)
        ms = jnp.mean(o * o, axis=-1, keepdims=True)
        o = o * lax.rsqrt(ms + RMS_EPS) * g_ref[...] * out_scale
        o_ref[0, :, sl] = o.astype(BF16)


def _attn_call(kernel_fn, name, head, tail, q, kt, v, nb, ntok, wq, wkv, ctx_queries):
    cblk = CTX_LEN // ROW_TILE
    if ctx_queries:
        rows, nk, grid = CTX_LEN, CTX_LEN, (nb, 1)
        qspecs = [pl.BlockSpec((1, ROW_TILE, wq), lambda b, j: (b, 0, 0))]
    else:
        rows, nk, grid = ATTN_TILE, ntok, (nb, (ntok - CTX_LEN) // ATTN_TILE)
        per = ATTN_TILE // ROW_TILE
        qspecs = [pl.BlockSpec((1, ROW_TILE, wq), lambda b, j, r=r: (b, cblk + per * j + r, 0))
                  for r in range(per)]
    kspec = pl.BlockSpec((1, wkv, nk), lambda b, j: (b, 0, 0))
    vspec = pl.BlockSpec((1, nk, wkv), lambda b, j: (b, 0, 0))
    return pl.pallas_call(
        kernel_fn,
        out_shape=jax.ShapeDtypeStruct((nb, rows * grid[1], wq), BF16),
        grid=grid,
        in_specs=[sp for _, sp in head] + qspecs + [kspec, vspec] + [sp for _, sp in tail],
        out_specs=pl.BlockSpec((1, rows, wq), lambda b, j: (b, j, 0)),
        compiler_params=_cparams(("parallel", "arbitrary")),
        name=name,
    )(*[a for a, _ in head], *([q] * len(qspecs)), kt, v, *[a for a, _ in tail])


def _attn_a_call(qa, ka, va, nb, ntok, ctx_queries):
    return _attn_call(_gqa_kernel, "gqa_attention_ctx" if ctx_queries else "gqa_attention", [], [],
                      qa, ka, va, nb, ntok, A_Q_W, 2 * A_KV_W, ctx_queries)


def _attn_c_call(lam, qc, kc, vc, subln, nb, ntok, ctx_queries, out_scale):
    head = [(lam, pl.BlockSpec(memory_space=pltpu.SMEM))]
    tail = [(subln, pl.BlockSpec((1, C_V_DIM), lambda b, j: (0, 0)))]
    return _attn_call(functools.partial(_diff_kernel, out_scale=out_scale),
                      "diff_attention_ctx" if ctx_queries else "diff_attention", head, tail,
                      qc, kc, vc, nb, ntok, C_QK_W, C_QK_W, ctx_queries)


def _fourier_kernel(f_ref, chan_ref, pos_ref, posc_ref, o_ref, g_ref, *, ntok, first_tile):
    j = pl.program_id(1) + first_tile
    nlat = ntok - CTX_LEN

    def channel_stage(rows):
        g = _dot(rows, chan_ref[...])
        return jnp.concatenate([g[:, :B_W], g[:, B_W:]], axis=0).astype(BF16)

    if first_tile == 0:
        @pl.when(j == 0)
        def _():
            gc = channel_stage(f_ref[0, 0:CTX_LEN, :])
            y = _dot(posc_ref[...], gc) * (1.0 / math.sqrt(CTX_LEN * F_GROUP_W))
            o_ref[0] = y.astype(BF16)

    @pl.when(j == 1)
    def _():
        g_ref[...] = channel_stage(f_ref[0, CTX_LEN:ntok, :])

    @pl.when(j >= 1)
    def _():
        y = _dot(pos_ref[...], g_ref[...]) * (1.0 / math.sqrt(nlat * F_GROUP_W))
        o_ref[0] = y.astype(BF16)


def _fourier_call(fb, chan, pos, posc, nb, ntok, first_tile):
    tpb = ntok // ROW_TILE
    nlat = ntok - CTX_LEN
    return pl.pallas_call(
        functools.partial(_fourier_kernel, ntok=ntok, first_tile=first_tile),
        out_shape=jax.ShapeDtypeStruct((nb, ntok - first_tile * ROW_TILE, B_W), BF16),
        grid=(nb, tpb - first_tile),
        in_specs=[
            pl.BlockSpec((1, ntok, B_W), lambda b, j: (b, 0, 0)),
            pl.BlockSpec(chan.shape, lambda b, j: (0, 0)),
            pl.BlockSpec((ROW_TILE, 2 * nlat),
                         lambda b, j: (jnp.maximum(j + first_tile - 1, 0), 0)),
            pl.BlockSpec(posc.shape, lambda b, j: (0, 0)),
        ],
        out_specs=pl.BlockSpec((1, ROW_TILE, B_W), lambda b, j: (b, j, 0)),
        scratch_shapes=[pltpu.VMEM((2 * nlat, B_W), BF16)],
        compiler_params=_cparams(("parallel", "arbitrary")),
        name="fourier_mix",
    )(fb, chan, pos, posc)


def _dft_tables(n):
    n1 = 32
    n0 = n // n1
    k = np.arange(n, dtype=np.int64)
    a = 2.0 * np.pi * ((k[:, None] * np.arange(n1)[None, :] * n0) % n) / n
    b = 2.0 * np.pi * ((k[:, None] * np.arange(n0)[None, :]) % n) / n
    ca, sa = jnp.asarray(np.cos(a), F32)[:, :, None], jnp.asarray(np.sin(a), F32)[:, :, None]
    cb, sb = jnp.asarray(np.cos(b), F32)[:, None, :], jnp.asarray(np.sin(b), F32)[:, None, :]
    cos = (ca * cb - sa * sb).reshape(n, n)
    sin = (sa * cb + ca * sb).reshape(n, n)
    return jnp.concatenate([cos, -sin], axis=1).astype(BF16)


def _channel_table():
    c = np.arange(F_GROUP_W)
    ang = 2.0 * np.pi * ((c[:, None] * c[None, :]) % F_GROUP_W) / F_GROUP_W
    eye = np.eye(F_GROUPS)
    cos = np.kron(eye, np.cos(ang))
    sin = np.kron(eye, np.sin(ang))
    return jnp.asarray(np.concatenate([cos, sin], axis=1), F32).astype(BF16)


def _store_packed_slabs(dst_ref, u):
    rows = u.shape[0]
    for jb in range(PSLAB):
        lo = u[:, 2 * jb * LANES:(2 * jb + 1) * LANES]
        hi = u[:, (2 * jb + 1) * LANES:(2 * jb + 2) * LANES]
        dst_ref[pl.ds(jb, rows, stride=PSLAB), :] = pltpu.pack_elementwise([lo, hi], packed_dtype=BF16)


def _load_packed_slabs(src_ref, rows):
    out = []
    for jb in range(PSLAB):
        words = src_ref[pl.ds(jb, rows, stride=PSLAB), :]
        out.append((pltpu.unpack_elementwise(words, index=0, packed_dtype=BF16, unpacked_dtype=F32),
                    pltpu.unpack_elementwise(words, index=1, packed_dtype=BF16, unpacked_dtype=F32)))
    return out


def _mixout_kernel(h_ref, mod_ref, oa_ref, oa_ctx_ref, ob_ref, oc_ref, oc_ctx_ref,
                   wg_ref, wa_ref, wb_ref, wc_ref, wo_ref, g1_ref, b1_ref, wr_ref,
                   h1_o, xp_o, lg_o, *, alpha, ctx_every):
    h = h_ref[...]
    m = mod_ref[0]
    u = (_layer_norm_rows(h) * (1.0 + m[1:2, :]) + m[0:1, :]).astype(BF16)
    oa, oc = oa_ref[...], oc_ref[...]
    if ctx_every:
        is_ctx = pl.program_id(0) % ctx_every == 0
        oa = jnp.where(is_ctx, oa_ctx_ref[...], oa)
        oc = jnp.where(is_ctx, oc_ctx_ref[...], oc)
    y = None
    for n, (o, w_ref) in enumerate(((oa, wa_ref), (ob_ref[...], wb_ref), (oc, wc_ref))):
        gate = jax.nn.sigmoid(_dot(u, wg_ref[:, n * D_MODEL:(n + 1) * D_MODEL]))
        t = gate * _dot(o, w_ref[...])
        y = t if y is None else y + t
    z = _dot(y.astype(BF16), wo_ref[...])
    h1 = _layer_norm_rows(alpha * h + m[2:3, :] * z) * g1_ref[...] + b1_ref[...]
    h1_o[...] = h1
    u2 = _layer_norm_rows(h1) * (1.0 + m[4:5, :]) + m[3:4, :]
    _store_packed_slabs(xp_o, u2)
    lg_o[...] = _dot_nt(wr_ref[...], u2.astype(BF16))


def _tile_maps(nb, ntok, lat_only):
    tpb = ntok // ROW_TILE
    if lat_only:
        lpb = tpb - 1
        n_tiles = nb * lpb
        src = lambda i: (i // lpb) * tpb + 1 + i % lpb
        modi = lambda i: i // lpb
    else:
        n_tiles = nb * tpb
        src = lambda i: i
        modi = lambda i: jnp.where(i % tpb == 0, nb, i // tpb)
    return n_tiles, src, modi


def _mixout_call(h_all, mod, oa, oa_ctx, ob, oc, oc_ctx, wts, nb, ntok, lat_only, alpha):
    n_tiles, src, modi = _tile_maps(nb, ntok, lat_only)
    rows_out = n_tiles * ROW_TILE
    tpb = ntok // ROW_TILE
    lpb = tpb - CTX_LEN // ROW_TILE
    if lat_only:
        lat_blk = lambda i: i
        ctx_blk = lambda i: 0
    else:
        lat_blk = lambda i: (i // tpb) * lpb + jnp.maximum(i % tpb - 1, 0)
        ctx_blk = lambda i: i // tpb

    def full(a):
        return pl.BlockSpec(a.shape, lambda i: (0,) * a.ndim)

    def inrow(w):
        return pl.BlockSpec((ROW_TILE, w), lambda i: (src(i), 0))

    def outrow(w):
        return pl.BlockSpec((ROW_TILE, w), lambda i: (i, 0))

    modspec = pl.BlockSpec((1, 8, D_MODEL), lambda i: (modi(i), 0, 0))
    def latrow(w):
        return pl.BlockSpec((ROW_TILE, w), lambda i: (lat_blk(i), 0))

    def ctxrow(w):
        return pl.BlockSpec((ROW_TILE, w), lambda i: (ctx_blk(i), 0))

    return pl.pallas_call(
        functools.partial(_mixout_kernel, alpha=alpha, ctx_every=0 if lat_only else tpb),
        out_shape=[jax.ShapeDtypeStruct((rows_out, D_MODEL), F32),
                   jax.ShapeDtypeStruct((rows_out * PSLAB, LANES), U32),
                   jax.ShapeDtypeStruct((N_EXPERTS, rows_out), F32)],
        grid=(n_tiles,),
        in_specs=[inrow(D_MODEL), modspec, latrow(A_Q_W), ctxrow(A_Q_W), outrow(B_W),
                  latrow(C_V_W), ctxrow(C_V_W)]
                 + [full(w) for w in wts],
        out_specs=[outrow(D_MODEL), pl.BlockSpec((ROW_TILE * PSLAB, LANES), lambda i: (i, 0)),
                   pl.BlockSpec((N_EXPERTS, ROW_TILE), lambda i: (0, i))],
        compiler_params=_cparams(("parallel",)),
        name="mixer_output",
    )(h_all, mod, oa, oa_ctx, ob, oc, oc_ctx, *wts)


def _moe_kernel(first_ref, count_ref, total_ref,
                tab_hbm, xp_ref, wg_ref, wu_ref, wd_ref,
                out_hbm,
                tab, gath, wgu_s, wd_s, ybuf, sem, tsem, *, t_pad, n_tok, lead_row):
    nfl = BLOCKS_IN_FLIGHT
    tmask = TAB_SLOTS - 1
    e = pl.program_id(0)
    total = total_ref[0]

    def row_copy(s, m, dst4):
        return pltpu.make_async_copy(ybuf.at[s, pl.ds(PSLAB * m, PSLAB), :],
                                     out_hbm.at[pl.ds(pl.multiple_of(dst4, PSLAB), PSLAB), :],
                                     sem.at[s])

    def wait_rows(s):
        for m in range(MOE_BLOCK):
            row_copy(s, m, 0).wait()

    def send_rows(ts, ys):
        for m in range(MOE_BLOCK):
            row_copy(ys, m, tab[ts, 0, MOE_BLOCK + m]).start(priority=m % 2)

    def tab_copy(row, s):
        return pltpu.make_async_copy(tab_hbm.at[row], tab.at[s], tsem.at[s])

    def gather_rows(ts, gs):
        for m in range(MOE_BLOCK):
            t4 = pl.multiple_of(tab[ts, 0, m], PSLAB)
            gath[gs, PSLAB * m:PSLAB * (m + 1), :] = xp_ref[pl.ds(t4, PSLAB), :]

    @pl.when(e == 0)
    def _():
        ybuf[...] = jnp.zeros_like(ybuf)
        tab_copy(lead_row, tmask).start()
        for r in range(TAB_AHEAD):
            tab_copy(r, r).start()
        for b in range(nfl - 1):
            for m in range(MOE_BLOCK):
                q = b * MOE_BLOCK + m
                row_copy(b, m, ((q % TOP_K) * t_pad + n_tok + q // TOP_K) * PSLAB).start(priority=m % 2)
        tab_copy(lead_row, tmask).wait()
        tab_copy(0, 0).wait()
        gather_rows(0, 0)

    wgu_s[:, 0:EXPERT_FF] = wg_ref[0, 0].astype(BF16)
    wgu_s[:, EXPERT_FF:2 * EXPERT_FF] = wu_ref[0, 0].astype(BF16)
    wd_s[...] = wd_ref[0, 0].astype(BF16)

    def block(j, carry):
        g = first_ref[e] + j
        tab_copy(g + 1, (g + 1) & tmask).wait()
        tab_copy(g + TAB_AHEAD, (g + TAB_AHEAD) & tmask).start()
        ys = lax.rem(g, nfl)
        wait_rows(ys)

        send_rows((g - 1) & tmask, lax.rem(g + nfl - 1, nfl))
        gather_rows((g + 1) & tmask, (g + 1) & 1)

        acc = None
        for jb, (lo, hi) in enumerate(_load_packed_slabs(gath.at[g & 1], MOE_BLOCK)):
            a = jnp.concatenate([lo, hi], axis=1).astype(BF16)
            t = _dot(a, wgu_s[jb * 2 * LANES:(jb + 1) * 2 * LANES, :])
            acc = t if acc is None else acc + t
        gt = acc[:, 0:EXPERT_FF]
        act = (gt * jax.nn.sigmoid(gt)) * acc[:, EXPERT_FF:2 * EXPERT_FF]
        y = _dot(act.astype(BF16), wd_s[...])
        _store_packed_slabs(ybuf.at[ys], y)
        return carry

    lax.fori_loop(0, count_ref[e], block, 0)

    @pl.when(e == pl.num_programs(0) - 1)
    def _():
        send_rows((total - 1) & tmask, lax.rem(total + nfl - 1, nfl))
        for s in range(nfl):
            wait_rows(s)
        for r in range(1, TAB_AHEAD):
            tab_copy(0, (total + r) & tmask).wait()


def _moe_call(first_blk, n_blk, total, table, xp4, wg, wu, wd, layer, n_tok):
    t_pad = n_tok + PAD_ROWS
    lead_row = table.shape[0] - 1
    wspec = lambda r, c: pl.BlockSpec((1, 1, r, c), lambda e, fb, nb, tt: (layer, e, 0, 0))
    grid_spec = pltpu.PrefetchScalarGridSpec(
        num_scalar_prefetch=3,
        grid=(N_EXPERTS,),
        in_specs=[
            pl.BlockSpec(memory_space=pl.ANY),
            pl.BlockSpec(memory_space=pltpu.VMEM),
            wspec(D_MODEL, EXPERT_FF), wspec(D_MODEL, EXPERT_FF), wspec(EXPERT_FF, D_MODEL),
        ],
        out_specs=pl.BlockSpec(memory_space=pl.ANY),
        scratch_shapes=[
            pltpu.SMEM((TAB_SLOTS, 1, 2 * MOE_BLOCK), jnp.int32),
            pltpu.VMEM((2, MOE_BLOCK * PSLAB, LANES), U32),
            pltpu.VMEM((D_MODEL, 2 * EXPERT_FF), BF16),
            pltpu.VMEM((EXPERT_FF, D_MODEL), BF16),
            pltpu.VMEM((BLOCKS_IN_FLIGHT, MOE_BLOCK * PSLAB, LANES), U32),
            pltpu.SemaphoreType.DMA((BLOCKS_IN_FLIGHT,)),
            pltpu.SemaphoreType.DMA((TAB_SLOTS,)),
        ],
    )
    return pl.pallas_call(
        functools.partial(_moe_kernel, t_pad=t_pad, n_tok=n_tok, lead_row=lead_row),
        out_shape=jax.ShapeDtypeStruct((TOP_K * t_pad * PSLAB, LANES), U32),
        grid_spec=grid_spec,
        compiler_params=_cparams(("arbitrary",)),
        name="routed_experts",
    )(first_blk, n_blk, total, table, xp4, wg, wu, wd)


def _combine_kernel(h1_ref, mod_ref, y8_ref, rw_ref, sg_ref, su_ref, sd_ref, g2_ref, b2_ref, o_ref,
                    *, alpha):
    h1 = h1_ref[...]
    m = mod_ref[0]
    u2 = (_layer_norm_rows(h1) * (1.0 + m[4:5, :]) + m[3:4, :]).astype(BF16)
    g = _dot(u2, sg_ref[...])
    act = (g * jax.nn.sigmoid(g)) * _dot(u2, su_ref[...])
    f = _dot(act.astype(BF16), sd_ref[...])
    tm = h1.shape[0]
    rw = rw_ref[...]
    wk = [jnp.broadcast_to(rw[:, kk:kk + 1], (tm, LANES)) for kk in range(TOP_K)]
    cols = [None] * (2 * PSLAB)
    for kk in range(TOP_K):
        for jb, pair in enumerate(_load_packed_slabs(y8_ref.at[kk], tm)):
            for half in range(2):
                t = wk[kk] * pair[half]
                c = 2 * jb + half
                cols[c] = t if cols[c] is None else cols[c] + t
    f = f + jnp.concatenate(cols, axis=1)
    o_ref[...] = _layer_norm_rows(alpha * h1 + m[5:6, :] * f) * g2_ref[...] + b2_ref[...]


def _combine_call(h1, mod, y8, rw, wts, rows_per_group, group_is_ctx_first, nb, alpha):
    rows = h1.shape[0]
    tm = COMBINE_TILE
    tpg = rows_per_group // tm
    if group_is_ctx_first:
        first = CTX_LEN // tm
        modi = lambda i: jnp.where(i % tpg < first, nb, i // tpg)
    else:
        modi = lambda i: i // tpg

    def full(a):
        return pl.BlockSpec(a.shape, lambda i: (0,) * a.ndim)

    return pl.pallas_call(
        functools.partial(_combine_kernel, alpha=alpha),
        out_shape=jax.ShapeDtypeStruct((rows, D_MODEL), F32),
        grid=(rows // tm,),
        in_specs=[pl.BlockSpec((tm, D_MODEL), lambda i: (i, 0)),
                  pl.BlockSpec((1, 8, D_MODEL), lambda i: (modi(i), 0, 0)),
                  pl.BlockSpec((TOP_K, tm * PSLAB, LANES), lambda i: (0, i, 0)),
                  pl.BlockSpec((tm, TOP_K), lambda i: (i, 0))]
                 + [full(w) for w in wts],
        out_specs=pl.BlockSpec((tm, D_MODEL), lambda i: (i, 0)),
        compiler_params=_cparams(("parallel",)),
        name="moe_combine",
    )(h1, mod, y8, rw, *wts)


_BIG_LANE = 1 << 30


def _route_kernel(lg_ref, b_ref, idx_o, w_o, rank_o, cnt_o, carry):
    i = pl.program_id(0)

    @pl.when(i == 0)
    def _():
        carry[...] = jnp.zeros_like(carry)

    tm = lg_ref.shape[1]
    gsize = N_EXPERTS // N_GROUPS
    scores = jax.nn.sigmoid(lg_ref[...])
    biased = scores + b_ref[...]
    eid = lax.broadcasted_iota(jnp.int32, (N_EXPERTS, tm), 0)

    b3 = biased.reshape(N_GROUPS, gsize, tm)
    in_g = lax.broadcasted_iota(jnp.int32, (N_GROUPS, gsize, tm), 1)
    m1 = jnp.max(b3, axis=1, keepdims=True)
    first = jnp.min(jnp.where(b3 == m1, in_g, _BIG_LANE), axis=1, keepdims=True)
    m2 = jnp.max(jnp.where(in_g == first, -jnp.inf, b3), axis=1, keepdims=True)
    gscore = (m1 + m2).reshape(N_GROUPS, tm)

    gid = lax.broadcasted_iota(jnp.int32, (N_GROUPS, tm), 0)
    beaten = jnp.zeros((N_GROUPS, tm), jnp.int32)
    for g in range(N_GROUPS):
        sg = gscore[g:g + 1, :]
        ahead = jnp.logical_or(sg > gscore, jnp.logical_and(sg == gscore, g < gid))
        beaten = beaten + ahead.astype(jnp.int32)
    keep = jnp.broadcast_to((beaten < TOPK_GROUPS).astype(jnp.int32).reshape(N_GROUPS, 1, tm),
                            (N_GROUPS, gsize, tm)).reshape(N_EXPERTS, tm)
    masked = jnp.where(keep > 0, biased, -jnp.inf)

    idxs, ws, hots = [], [], []
    for _ in range(TOP_K):
        m = jnp.max(masked, axis=0, keepdims=True)
        ix = jnp.min(jnp.where(masked == m, eid, _BIG_LANE), axis=0, keepdims=True)
        hot = eid == ix
        idxs.append(ix)
        ws.append(jnp.sum(jnp.where(hot, scores, 0.0), axis=0, keepdims=True))
        hots.append(hot)
        masked = jnp.where(hot, -jnp.inf, masked)
    wsum = ws[0]
    for r in range(1, TOP_K):
        wsum = wsum + ws[r]
    idx_o[...] = jnp.concatenate(idxs, axis=0)
    w_rows = jnp.concatenate([wr / wsum * ROUTED_SCALE for wr in ws], axis=0)

    eye = (lax.broadcasted_iota(jnp.int32, (tm, tm), 0)
           == lax.broadcasted_iota(jnp.int32, (tm, tm), 1)).astype(BF16)
    w_cols = jnp.zeros((tm, TOP_K), F32)
    rest = w_rows
    for _ in range(3):
        part = rest.astype(BF16)
        rest = rest - part.astype(F32)
        w_cols = w_cols + _dot_nt(eye, part)
    w_o[...] = w_cols

    sel = jnp.zeros((N_EXPERTS, tm), F32)
    for hot in hots:
        sel = sel + hot.astype(F32)
    sel = sel.astype(BF16)
    earlier = (lax.broadcasted_iota(jnp.int32, (tm, tm), 0)
               < lax.broadcasted_iota(jnp.int32, (tm, tm), 1)).astype(BF16)
    prefix = _dot(sel, earlier) + carry[:, 0:1]
    rank_o[...] = jnp.concatenate(
        [jnp.sum(jnp.where(hot, prefix, 0.0), axis=0, keepdims=True) for hot in hots],
        axis=0).astype(jnp.int32)
    carry[...] = carry[...] + _dot(sel, jnp.ones((tm, LANES), BF16))
    cnt_o[...] = carry[...]


def _route_call(logits_t, b_r):
    t = logits_t.shape[1]
    tm = ROW_TILE
    kt = pl.BlockSpec((TOP_K, tm), lambda i: (0, i))
    return pl.pallas_call(
        _route_kernel,
        out_shape=[jax.ShapeDtypeStruct((TOP_K, t), jnp.int32),
                   jax.ShapeDtypeStruct((t, TOP_K), F32),
                   jax.ShapeDtypeStruct((TOP_K, t), jnp.int32),
                   jax.ShapeDtypeStruct((N_EXPERTS, LANES), F32)],
        grid=(t // tm,),
        in_specs=[pl.BlockSpec((N_EXPERTS, tm), lambda i: (0, i)),
                  pl.BlockSpec((N_EXPERTS, 1), lambda i: (0, 0))],
        out_specs=[kt, pl.BlockSpec((tm, TOP_K), lambda i: (i, 0)), kt,
                   pl.BlockSpec((N_EXPERTS, LANES), lambda i: (0, 0))],
        scratch_shapes=[pltpu.VMEM((N_EXPERTS, LANES), F32)],
        compiler_params=_cparams(("arbitrary",)),
        name="route_topk",
    )(logits_t, b_r.astype(F32).reshape(N_EXPERTS, 1))


def _dest_kernel(idx_ref, rank_ref, start_ref, o_ref):
    tm = idx_ref.shape[1]
    eid = lax.broadcasted_iota(jnp.int32, (N_EXPERTS, tm), 0)
    idx = idx_ref[...]
    start = start_ref[...]
    rows = [jnp.sum(jnp.where(eid == idx[r:r + 1, :], start, 0), axis=0, keepdims=True)
            for r in range(TOP_K)]
    o_ref[...] = jnp.concatenate(rows, axis=0) + rank_ref[...]


def _dest_call(idx, rank, pad_start):
    t = idx.shape[1]
    tm = ROW_TILE
    blk = pl.BlockSpec((TOP_K, tm), lambda i: (0, i))
    return pl.pallas_call(
        _dest_kernel,
        out_shape=jax.ShapeDtypeStruct((TOP_K, t), jnp.int32),
        grid=(t // tm,),
        in_specs=[blk, blk, pl.BlockSpec((N_EXPERTS, 1), lambda i: (0, 0))],
        out_specs=blk,
        compiler_params=_cparams(("parallel",)),
        name="slot_of_assignment",
    )(idx, rank, pad_start.reshape(N_EXPERTS, 1))


def _dispatch(idx, rank, counts, n_tok):
    n_assign = n_tok * TOP_K
    used_max = (n_assign + N_EXPERTS * (MOE_BLOCK - 1) + MOE_BLOCK - 1) // MOE_BLOCK
    nblk = -(-(used_max + TAB_AHEAD + 1) // BLOCKS_IN_FLIGHT) * BLOCKS_IN_FLIGHT
    n_slots = nblk * MOE_BLOCK
    counts = counts[:, 0].astype(jnp.int32)
    padded = (counts + MOE_BLOCK - 1) // MOE_BLOCK * MOE_BLOCK
    pad_end = jnp.cumsum(padded)
    pad_start = pad_end - padded
    dest = _dest_call(idx, rank, pad_start)
    assign = (jnp.arange(n_tok, dtype=jnp.int32)[None, :] * TOP_K
              + jnp.arange(TOP_K, dtype=jnp.int32)[:, None])
    assert (nblk - 1) % BLOCKS_IN_FLIGHT == BLOCKS_IN_FLIGHT - 1
    pad_a = n_assign + jnp.arange(n_slots, dtype=jnp.int32) % (BLOCKS_IN_FLIGHT * MOE_BLOCK)
    slot_a = pad_a.at[dest.reshape(-1)].set(assign.reshape(-1), unique_indices=True,
                                            mode='promise_in_bounds')
    t_pad = n_tok + PAD_ROWS
    tok = lax.shift_right_logical(slot_a, K_SHIFT)
    tok4 = jnp.minimum(tok, n_tok - 1) * PSLAB
    dst4 = ((slot_a & (TOP_K - 1)) * t_pad + tok) * PSLAB
    table = jnp.concatenate([tok4.reshape(nblk, 1, MOE_BLOCK), dst4.reshape(nblk, 1, MOE_BLOCK)], axis=2)
    total = (pad_end[-1] // MOE_BLOCK).astype(jnp.int32).reshape(1)
    return pad_start // MOE_BLOCK, padded // MOE_BLOCK, total, table


def _rope_tables(s, ntok):
    rows_n = s // GRID_W
    row = jnp.repeat(jnp.arange(rows_n, dtype=F32), GRID_W)
    col = jnp.tile(jnp.arange(GRID_W, dtype=F32), rows_n)
    axis_dim = HEAD_DIM // 2
    inv = jnp.power(ROPE_THETA, -jnp.arange(0, axis_dim, 2, dtype=F32) / axis_dim)
    ar = row[:, None] * inv[None]
    ac = col[:, None] * inv[None]
    ang = jnp.concatenate([ar, ar, ac, ac], -1)
    cos, sin = jnp.cos(ang), jnp.sin(ang)
    quarter = (jnp.arange(HEAD_DIM) // 16) % 2
    s_up = jnp.where(quarter == 0, -sin, 0.0)
    s_dn = jnp.where(quarter == 1, sin, 0.0)
    nctx = ntok - s

    def expand(t, ctx_val):
        t = jnp.concatenate([jnp.full((nctx, HEAD_DIM), ctx_val, F32), t], axis=0)
        return jnp.tile(t, (1, LANES // HEAD_DIM))

    return expand(cos, 1.0), expand(s_up, 0.0), expand(s_dn, 0.0)


def _head_mean_matrix(width):
    hid = np.arange(width) // HEAD_DIM
    return jnp.asarray((hid[:, None] == hid[None, :]).astype(np.float32) / HEAD_DIM).astype(BF16)


def _dup_heads(a, n_heads):
    parts = []
    for hd in range(n_heads):
        p = a[..., hd * HEAD_DIM:(hd + 1) * HEAD_DIM]
        parts += [p, p]
    return jnp.concatenate(parts, axis=-1)


def kernel(x, c, ctx, c_ctx, w_mod, b_mod, w_in, qn_a, kn_a, lam_q1, lam_k1, lam_q2, lam_k2, subln_c, w_br_a, w_br_b, w_br_c, w_out, ln1_g, ln1_b, w_router, b_router, w_sh_gate, w_sh_up, w_sh_down, w_e_gate, w_e_up, w_e_down, ln2_g, ln2_b):
    nb, s, d = x.shape
    lc = ctx.shape[1]
    depth = w_mod.shape[0]
    assert d == D_MODEL and lc == CTX_LEN and s % ROW_TILE == 0 and s % GRID_W == 0
    ntok = lc + s
    alpha = (2 * depth) ** 0.25

    tabs = _rope_tables(s, ntok)
    pos_lat = _dft_tables(s)
    pos_ctx = _dft_tables(lc)
    chan = _channel_table()
    e_q = _head_mean_matrix(A_Q_W)
    e_k = _head_mean_matrix(2 * A_KV_W)

    cc = jnp.concatenate([c, c_ctx[None, :]], axis=0)
    cc = jnp.pad(cc, ((0, (-(nb + 1)) % 8), (0, 0)))
    h_all = jnp.concatenate([ctx, x], axis=1).reshape(nb * ntok, d)

    offs = np.cumsum([0, A_Q_W, A_KV_W, A_KV_W, B_W, C_QK_W, C_QK_W, C_V_W, GATE_W])
    out = None
    for l in range(depth):
        last = l == depth - 1
        mod = _mod_call(cc, w_mod[l], b_mod[l])[:nb + 1].reshape(nb + 1, 6, d)
        mod = jnp.pad(mod, ((0, 0), (0, 2), (0, 0)))
        lam_init = 0.8 - 0.6 * math.exp(-0.3 * l)
        lam = (jnp.exp(jnp.sum(lam_q1[l].astype(F32) * lam_k1[l].astype(F32)))
               - jnp.exp(jnp.sum(lam_q2[l].astype(F32) * lam_k2[l].astype(F32)))) + lam_init
        lam = lam.reshape(1, 1).astype(F32)

        wl = w_in[l]
        seg = [wl[:, offs[i]:offs[i + 1]] for i in range(8)]
        in_wts = (seg[0].astype(BF16), _dup_heads(seg[1], A_KV_HEADS).astype(BF16),
                  _dup_heads(seg[2], A_KV_HEADS).astype(BF16), seg[3].astype(BF16),
                  seg[4].astype(BF16), seg[5].astype(BF16), seg[6].astype(BF16),
                  jnp.tile(qn_a[l].astype(F32), A_Q_HEADS).reshape(1, A_Q_W),
                  jnp.tile(kn_a[l].astype(F32), 2 * A_KV_HEADS).reshape(1, 2 * A_KV_W),
                  e_q, e_k)
        qa, ka, va, fb, qc, kc, vc = _inproj_call(h_all, mod, tabs, in_wts, nb, ntok)

        first_tile = 1 if last else 0
        r3 = lambda a: a.reshape(nb, ntok, a.shape[-1])
        subln = subln_c[l].astype(F32).reshape(1, C_V_DIM)
        attn_a = functools.partial(_attn_a_call, r3(qa), ka, r3(va), nb, ntok)
        attn_c = functools.partial(_attn_c_call, lam, r3(qc), kc, r3(vc), subln, nb, ntok)
        oa, oc = attn_a(False), attn_c(False, 1.0 - lam_init)
        oa_ctx, oc_ctx = (oa, oc) if last else (attn_a(True), attn_c(True, 1.0 - lam_init))
        ob = _fourier_call(r3(fb), chan, pos_lat, pos_ctx, nb, ntok, first_tile)

        mix_wts = (seg[7].astype(BF16), w_br_a[l].astype(BF16), w_br_b[l].astype(BF16),
                   w_br_c[l].astype(BF16), w_out[l].astype(BF16),
                   ln1_g[l].astype(F32).reshape(1, d), ln1_b[l].astype(F32).reshape(1, d),
                   w_router[l].T.astype(BF16))
        flat = lambda a: a.reshape(-1, a.shape[-1])
        h1, xp, logits = _mixout_call(h_all, mod, flat(oa), flat(oa_ctx), flat(ob), flat(oc),
                                      flat(oc_ctx), mix_wts, nb, ntok, last, alpha)

        n_tok = h1.shape[0]
        idx, rw, rank, counts = _route_call(logits, b_router[l])
        first_blk, n_blk, total, table = _dispatch(idx, rank, counts, n_tok)
        y8 = _moe_call(first_blk, n_blk, total, table, xp,
                       w_e_gate, w_e_up, w_e_down, l, n_tok)
        y8 = y8.reshape(TOP_K, (n_tok + PAD_ROWS) * PSLAB, LANES)

        comb_wts = (w_sh_gate[l].astype(BF16), w_sh_up[l].astype(BF16), w_sh_down[l].astype(BF16),
                    ln2_g[l].astype(F32).reshape(1, d), ln2_b[l].astype(F32).reshape(1, d))
        h2 = _combine_call(h1, mod, y8, rw, comb_wts, s if last else ntok, not last, nb, alpha)
        if last:
            out = h2.reshape(nb, s, d)
        else:
            h_all = h2
    return out
```

```python
import functools
import math

import numpy as np
import jax
import jax.numpy as jnp
from jax import lax
from jax.experimental import pallas as pl
from jax.experimental.pallas import tpu as pltpu

F32 = jnp.float32
BF16 = jnp.bfloat16
U32 = jnp.uint32

D_MODEL = 1024
CTX_LEN = 256
GRID_W = 64
HEAD_DIM = 64
ROPE_THETA = 10000.0
A_Q_HEADS = 8
A_KV_HEADS = 2
A_Q_W = A_Q_HEADS * HEAD_DIM
A_KV_W = A_KV_HEADS * HEAD_DIM
F_GROUPS = 4
F_GROUP_W = 128
B_W = F_GROUPS * F_GROUP_W
C_HEADS = 4
C_V_DIM = 2 * HEAD_DIM
C_QK_W = C_HEADS * 2 * HEAD_DIM
C_V_W = C_HEADS * C_V_DIM
N_BRANCH = 3
GATE_W = N_BRANCH * D_MODEL
N_EXPERTS = 256
TOP_K = 8
N_GROUPS = 8
TOPK_GROUPS = 4
EXPERT_FF = 256
SHARED_FF = 256
ROUTED_SCALE = 2.5
LN_EPS = 1e-5
RMS_EPS = 1e-6
K_SHIFT = TOP_K.bit_length() - 1
assert 1 << K_SHIFT == TOP_K

LANES = 128
PSLAB = D_MODEL // (2 * LANES)
ROW_TILE = 256
ATTN_TILE = 512
MOE_BLOCK = 128
BLOCKS_IN_FLIGHT = 4
TAB_AHEAD = 4
TAB_SLOTS = 8
PAD_ROWS = BLOCKS_IN_FLIGHT * MOE_BLOCK // TOP_K
COMBINE_TILE = 256
VMEM_LIMIT = 56 * 1024 * 1024

_Q_SCALE = HEAD_DIM ** -0.5 * math.log2(math.e)


def _cparams(sem):
    return pltpu.CompilerParams(dimension_semantics=sem, vmem_limit_bytes=VMEM_LIMIT)


def _dot(a, b):
    return jnp.dot(a, b, preferred_element_type=F32)


def _dot_nt(a, b):
    return lax.dot_general(a, b, (((1,), (1,)), ((), ())), preferred_element_type=F32)


def _layer_norm_rows(x):
    mu = jnp.mean(x, axis=-1, keepdims=True)
    xc = x - mu
    var = jnp.mean(xc * xc, axis=-1, keepdims=True)
    return xc * lax.rsqrt(var + LN_EPS)


def _dot_split(x, e):
    hi = x.astype(BF16)
    lo = (x - hi.astype(F32)).astype(BF16)
    return _dot(hi, e) + _dot(lo, e)


def _mod_kernel(c_ref, w_ref, b_ref, o_ref):
    c = c_ref[...]
    sc = c * jax.nn.sigmoid(c)
    o_ref[...] = _dot(sc.astype(BF16), w_ref[...].astype(BF16)) + b_ref[...]


def _mod_call(cc, w_mod_l, b_mod_l):
    r = cc.shape[0]
    n = w_mod_l.shape[1]
    tn = D_MODEL
    return pl.pallas_call(
        _mod_kernel,
        out_shape=jax.ShapeDtypeStruct((r, n), F32),
        grid=(n // tn,),
        in_specs=[
            pl.BlockSpec((r, D_MODEL), lambda j: (0, 0)),
            pl.BlockSpec((D_MODEL, tn), lambda j: (0, j)),
            pl.BlockSpec((1, tn), lambda j: (0, j)),
        ],
        out_specs=pl.BlockSpec((r, tn), lambda j: (0, j)),
        compiler_params=_cparams(("arbitrary",)),
        name="mod_vectors",
    )(cc, w_mod_l, b_mod_l.reshape(1, n))


def _rope_cols(x, cos, sin_up, sin_dn):
    cols = []
    for c in range(x.shape[1] // LANES):
        xc = x[:, c * LANES:(c + 1) * LANES]
        up = pltpu.roll(xc, LANES - 16, axis=1)
        dn = pltpu.roll(xc, 16, axis=1)
        cols.append(xc * cos + up * sin_up + dn * sin_dn)
    return jnp.concatenate(cols, axis=1) if len(cols) > 1 else cols[0]


def _inproj_kernel(h_ref, mod_ref, cos_ref, su_ref, sd_ref,
                   wq_ref, wk_ref, wv_ref, wf_ref, wqc_ref, wkc_ref, wvc_ref,
                   qn_ref, kn_ref, eq_ref, ek_ref,
                   qa_o, ka_o, va_o, fb_o, qc_o, kc_o, vc_o):
    h = h_ref[...]
    shift = mod_ref[0, 0:1, :]
    scale = mod_ref[0, 1:2, :]
    u = (_layer_norm_rows(h) * (1.0 + scale) + shift).astype(BF16)
    cos = cos_ref[...]
    s_up = su_ref[...]
    s_dn = sd_ref[...]

    q = _dot(u, wq_ref[...])
    ms = _dot_split(q * q, eq_ref[...])
    q = q * lax.rsqrt(ms + RMS_EPS) * qn_ref[...]
    qa_o[...] = (_rope_cols(q, cos, s_up, s_dn) * _Q_SCALE).astype(BF16)

    k = _dot(u, wk_ref[...])
    ms = _dot_split(k * k, ek_ref[...])
    k = k * lax.rsqrt(ms + RMS_EPS) * kn_ref[...]
    ka_o[0] = jnp.transpose(_rope_cols(k, cos, s_up, s_dn)).astype(BF16)

    va_o[...] = _dot(u, wv_ref[...]).astype(BF16)
    fb_o[...] = _dot(u, wf_ref[...]).astype(BF16)
    qc = _dot(u, wqc_ref[...])
    qc_o[...] = (_rope_cols(qc, cos, s_up, s_dn) * _Q_SCALE).astype(BF16)
    kc = _dot(u, wkc_ref[...])
    kc_o[0] = jnp.transpose(_rope_cols(kc, cos, s_up, s_dn)).astype(BF16)
    vc_o[...] = _dot(u, wvc_ref[...]).astype(BF16)


def _inproj_call(h_all, mod, tabs, wts, nb, ntok):
    rows = h_all.shape[0]
    tpb = ntok // ROW_TILE
    cos, s_up, s_dn = tabs

    def full(a):
        return pl.BlockSpec(a.shape, lambda i: (0,) * a.ndim)

    def rowspec(w):
        return pl.BlockSpec((ROW_TILE, w), lambda i: (i, 0))

    tabspec = pl.BlockSpec((ROW_TILE, LANES), lambda i: (i % tpb, 0))
    modspec = pl.BlockSpec((1, 8, D_MODEL),
                           lambda i: (jnp.where(i % tpb == 0, nb, i // tpb), 0, 0))
    widths = (A_Q_W, 2 * A_KV_W, 2 * A_KV_W, B_W, C_QK_W, C_QK_W, C_V_W)
    transposed = (1, 5)

    def oshape(n, w):
        return (nb, w, ntok) if n in transposed else (rows, w)

    def ospec(n, w):
        if n in transposed:
            return pl.BlockSpec((1, w, ROW_TILE), lambda i: (i // tpb, 0, i % tpb))
        return rowspec(w)

    return pl.pallas_call(
        _inproj_kernel,
        out_shape=[jax.ShapeDtypeStruct(oshape(n, w), BF16) for n, w in enumerate(widths)],
        grid=(rows // ROW_TILE,),
        in_specs=[rowspec(D_MODEL), modspec, tabspec, tabspec, tabspec]
                 + [full(w) for w in wts],
        out_specs=[ospec(n, w) for n, w in enumerate(widths)],
        compiler_params=_cparams(("parallel",)),
        name="in_projection",
    )(h_all, mod, cos, s_up, s_dn, *wts)


def _softmax_parts(s):
    m = jnp.max(s, axis=-1, keepdims=True)
    e = jnp.exp2(s - m)
    return e, jnp.sum(e, axis=-1, keepdims=True)


def _query_rows(q_refs):
    rows = [r[0] for r in q_refs]
    return rows[0] if len(rows) == 1 else jnp.concatenate(rows, axis=0)


def _gqa_kernel(*refs):
    q_refs, (k_ref, v_ref, o_ref) = refs[:-3], refs[-3:]
    q = _query_rows(q_refs)
    lane = lax.broadcasted_iota(jnp.int32, (1, LANES), 1)
    low = lane < HEAD_DIM
    for c in range(A_Q_W // LANES):
        kvh = (2 * c) // (A_Q_HEADS // A_KV_HEADS)
        qc = q[:, c * LANES:(c + 1) * LANES]
        kk = k_ref[0, kvh * LANES:(kvh + 1) * LANES, :]
        vv = v_ref[0, :, kvh * LANES:(kvh + 1) * LANES]
        halves = []
        for keep in (low, jnp.logical_not(low)):
            qm = jnp.where(keep, qc, jnp.zeros_like(qc))
            e, l = _softmax_parts(_dot(qm, kk))
            halves.append(_dot(e.astype(BF16), vv) * (1.0 / l))
        o_ref[0, :, c * LANES:(c + 1) * LANES] = jnp.where(low, halves[0], halves[1]).astype(BF16)


def _diff_kernel(lam_ref, *refs, out_scale):
    q_refs, (k_ref, v_ref, g_ref, o_ref) = refs[:-4], refs[-4:]
    q = _query_rows(q_refs)
    lane = lax.broadcasted_iota(jnp.int32, (1, LANES), 1)
    low = lane < HEAD_DIM
    lam = lam_ref[0, 0]
    for hd in range(C_HEADS):
        sl = slice(hd * LANES, (hd + 1) * LANES)
        qc = q[:, sl]
        kk = k_ref[0, sl, :]
        vv = v_ref[0, :, sl]
        q1 = jnp.where(low, qc, jnp.zeros_like(qc))
        q2 = jnp.where(low, jnp.zeros_like(qc), qc)
        e1, l1 = _softmax_parts(_dot(q1, kk))
        e2, l2 = _softmax_parts(_dot(q2, kk))
        o = _dot(e1.astype(BF16), vv) * (1.0 / l1) - _dot(e2.astype(BF16), vv) * (lam / l2)
        ms = jnp.mean(o * o, axis=-1, keepdims=True)
        o = o * lax.rsqrt(ms + RMS_EPS) * g_ref[...] * out_scale
        o_ref[0, :, sl] = o.astype(BF16)


def _attn_call(kernel_fn, name, head, tail, q, kt, v, nb, ntok, wq, wkv, ctx_queries):
    cblk = CTX_LEN // ROW_TILE
    if ctx_queries:
        rows, nk, grid = CTX_LEN, CTX_LEN, (nb, 1)
        qspecs = [pl.BlockSpec((1, ROW_TILE, wq), lambda b, j: (b, 0, 0))]
    else:
        rows, nk, grid = ATTN_TILE, ntok, (nb, (ntok - CTX_LEN) // ATTN_TILE)
        per = ATTN_TILE // ROW_TILE
        qspecs = [pl.BlockSpec((1, ROW_TILE, wq), lambda b, j, r=r: (b, cblk + per * j + r, 0))
                  for r in range(per)]
    kspec = pl.BlockSpec((1, wkv, nk), lambda b, j: (b, 0, 0))
    vspec = pl.BlockSpec((1, nk, wkv), lambda b, j: (b, 0, 0))
    return pl.pallas_call(
        kernel_fn,
        out_shape=jax.ShapeDtypeStruct((nb, rows * grid[1], wq), BF16),
        grid=grid,
        in_specs=[sp for _, sp in head] + qspecs + [kspec, vspec] + [sp for _, sp in tail],
        out_specs=pl.BlockSpec((1, rows, wq), lambda b, j: (b, j, 0)),
        compiler_params=_cparams(("parallel", "arbitrary")),
        name=name,
    )(*[a for a, _ in head], *([q] * len(qspecs)), kt, v, *[a for a, _ in tail])


def _attn_a_call(qa, ka, va, nb, ntok, ctx_queries):
    return _attn_call(_gqa_kernel, "gqa_attention_ctx" if ctx_queries else "gqa_attention", [], [],
                      qa, ka, va, nb, ntok, A_Q_W, 2 * A_KV_W, ctx_queries)


def _attn_c_call(lam, qc, kc, vc, subln, nb, ntok, ctx_queries, out_scale):
    head = [(lam, pl.BlockSpec(memory_space=pltpu.SMEM))]
    tail = [(subln, pl.BlockSpec((1, C_V_DIM), lambda b, j: (0, 0)))]
    return _attn_call(functools.partial(_diff_kernel, out_scale=out_scale),
                      "diff_attention_ctx" if ctx_queries else "diff_attention", head, tail,
                      qc, kc, vc, nb, ntok, C_QK_W, C_QK_W, ctx_queries)


def _fourier_kernel(f_ref, chan_ref, pos_ref, posc_ref, o_ref, g_ref, *, ntok, first_tile):
    j = pl.program_id(1) + first_tile
    nlat = ntok - CTX_LEN

    def channel_stage(rows):
        g = _dot(rows, chan_ref[...])
        return jnp.concatenate([g[:, :B_W], g[:, B_W:]], axis=0).astype(BF16)

    if first_tile == 0:
        @pl.when(j == 0)
        def _():
            gc = channel_stage(f_ref[0, 0:CTX_LEN, :])
            y = _dot(posc_ref[...], gc) * (1.0 / math.sqrt(CTX_LEN * F_GROUP_W))
            o_ref[0] = y.astype(BF16)

    @pl.when(j == 1)
    def _():
        g_ref[...] = channel_stage(f_ref[0, CTX_LEN:ntok, :])

    @pl.when(j >= 1)
    def _():
        y = _dot(pos_ref[...], g_ref[...]) * (1.0 / math.sqrt(nlat * F_GROUP_W))
        o_ref[0] = y.astype(BF16)


def _fourier_call(fb, chan, pos, posc, nb, ntok, first_tile):
    tpb = ntok // ROW_TILE
    nlat = ntok - CTX_LEN
    return pl.pallas_call(
        functools.partial(_fourier_kernel, ntok=ntok, first_tile=first_tile),
        out_shape=jax.ShapeDtypeStruct((nb, ntok - first_tile * ROW_TILE, B_W), BF16),
        grid=(nb, tpb - first_tile),
        in_specs=[
            pl.BlockSpec((1, ntok, B_W), lambda b, j: (b, 0, 0)),
            pl.BlockSpec(chan.shape, lambda b, j: (0, 0)),
            pl.BlockSpec((ROW_TILE, 2 * nlat),
                         lambda b, j: (jnp.maximum(j + first_tile - 1, 0), 0)),
            pl.BlockSpec(posc.shape, lambda b, j: (0, 0)),
        ],
        out_specs=pl.BlockSpec((1, ROW_TILE, B_W), lambda b, j: (b, j, 0)),
        scratch_shapes=[pltpu.VMEM((2 * nlat, B_W), BF16)],
        compiler_params=_cparams(("parallel", "arbitrary")),
        name="fourier_mix",
    )(fb, chan, pos, posc)


def _dft_tables(n):
    n1 = 32
    n0 = n // n1
    k = np.arange(n, dtype=np.int64)
    a = 2.0 * np.pi * ((k[:, None] * np.arange(n1)[None, :] * n0) % n) / n
    b = 2.0 * np.pi * ((k[:, None] * np.arange(n0)[None, :]) % n) / n
    ca, sa = jnp.asarray(np.cos(a), F32)[:, :, None], jnp.asarray(np.sin(a), F32)[:, :, None]
    cb, sb = jnp.asarray(np.cos(b), F32)[:, None, :], jnp.asarray(np.sin(b), F32)[:, None, :]
    cos = (ca * cb - sa * sb).reshape(n, n)
    sin = (sa * cb + ca * sb).reshape(n, n)
    return jnp.concatenate([cos, -sin], axis=1).astype(BF16)


def _channel_table():
    c = np.arange(F_GROUP_W)
    ang = 2.0 * np.pi * ((c[:, None] * c[None, :]) % F_GROUP_W) / F_GROUP_W
    eye = np.eye(F_GROUPS)
    cos = np.kron(eye, np.cos(ang))
    sin = np.kron(eye, np.sin(ang))
    return jnp.asarray(np.concatenate([cos, sin], axis=1), F32).astype(BF16)


def _store_packed_slabs(dst_ref, u):
    rows = u.shape[0]
    for jb in range(PSLAB):
        lo = u[:, 2 * jb * LANES:(2 * jb + 1) * LANES]
        hi = u[:, (2 * jb + 1) * LANES:(2 * jb + 2) * LANES]
        dst_ref[pl.ds(jb, rows, stride=PSLAB), :] = pltpu.pack_elementwise([lo, hi], packed_dtype=BF16)


def _load_packed_slabs(src_ref, rows):
    out = []
    for jb in range(PSLAB):
        words = src_ref[pl.ds(jb, rows, stride=PSLAB), :]
        out.append((pltpu.unpack_elementwise(words, index=0, packed_dtype=BF16, unpacked_dtype=F32),
                    pltpu.unpack_elementwise(words, index=1, packed_dtype=BF16, unpacked_dtype=F32)))
    return out


def _mixout_kernel(h_ref, mod_ref, oa_ref, oa_ctx_ref, ob_ref, oc_ref, oc_ctx_ref,
                   wg_ref, wa_ref, wb_ref, wc_ref, wo_ref, g1_ref, b1_ref, wr_ref,
                   h1_o, xp_o, lg_o, *, alpha, ctx_every):
    h = h_ref[...]
    m = mod_ref[0]
    u = (_layer_norm_rows(h) * (1.0 + m[1:2, :]) + m[0:1, :]).astype(BF16)
    oa, oc = oa_ref[...], oc_ref[...]
    if ctx_every:
        is_ctx = pl.program_id(0) % ctx_every == 0
        oa = jnp.where(is_ctx, oa_ctx_ref[...], oa)
        oc = jnp.where(is_ctx, oc_ctx_ref[...], oc)
    y = None
    for n, (o, w_ref) in enumerate(((oa, wa_ref), (ob_ref[...], wb_ref), (oc, wc_ref))):
        gate = jax.nn.sigmoid(_dot(u, wg_ref[:, n * D_MODEL:(n + 1) * D_MODEL]))
        t = gate * _dot(o, w_ref[...])
        y = t if y is None else y + t
    z = _dot(y.astype(BF16), wo_ref[...])
    h1 = _layer_norm_rows(alpha * h + m[2:3, :] * z) * g1_ref[...] + b1_ref[...]
    h1_o[...] = h1
    u2 = _layer_norm_rows(h1) * (1.0 + m[4:5, :]) + m[3:4, :]
    _store_packed_slabs(xp_o, u2)
    lg_o[...] = _dot_nt(wr_ref[...], u2.astype(BF16))


def _tile_maps(nb, ntok, lat_only):
    tpb = ntok // ROW_TILE
    if lat_only:
        lpb = tpb - 1
        n_tiles = nb * lpb
        src = lambda i: (i // lpb) * tpb + 1 + i % lpb
        modi = lambda i: i // lpb
    else:
        n_tiles = nb * tpb
        src = lambda i: i
        modi = lambda i: jnp.where(i % tpb == 0, nb, i // tpb)
    return n_tiles, src, modi


def _mixout_call(h_all, mod, oa, oa_ctx, ob, oc, oc_ctx, wts, nb, ntok, lat_only, alpha):
    n_tiles, src, modi = _tile_maps(nb, ntok, lat_only)
    rows_out = n_tiles * ROW_TILE
    tpb = ntok // ROW_TILE
    lpb = tpb - CTX_LEN // ROW_TILE
    if lat_only:
        lat_blk = lambda i: i
        ctx_blk = lambda i: 0
    else:
        lat_blk = lambda i: (i // tpb) * lpb + jnp.maximum(i % tpb - 1, 0)
        ctx_blk = lambda i: i // tpb

    def full(a):
        return pl.BlockSpec(a.shape, lambda i: (0,) * a.ndim)

    def inrow(w):
        return pl.BlockSpec((ROW_TILE, w), lambda i: (src(i), 0))

    def outrow(w):
        return pl.BlockSpec((ROW_TILE, w), lambda i: (i, 0))

    modspec = pl.BlockSpec((1, 8, D_MODEL), lambda i: (modi(i), 0, 0))
    def latrow(w):
        return pl.BlockSpec((ROW_TILE, w), lambda i: (lat_blk(i), 0))

    def ctxrow(w):
        return pl.BlockSpec((ROW_TILE, w), lambda i: (ctx_blk(i), 0))

    return pl.pallas_call(
        functools.partial(_mixout_kernel, alpha=alpha, ctx_every=0 if lat_only else tpb),
        out_shape=[jax.ShapeDtypeStruct((rows_out, D_MODEL), F32),
                   jax.ShapeDtypeStruct((rows_out * PSLAB, LANES), U32),
                   jax.ShapeDtypeStruct((N_EXPERTS, rows_out), F32)],
        grid=(n_tiles,),
        in_specs=[inrow(D_MODEL), modspec, latrow(A_Q_W), ctxrow(A_Q_W), outrow(B_W),
                  latrow(C_V_W), ctxrow(C_V_W)]
                 + [full(w) for w in wts],
        out_specs=[outrow(D_MODEL), pl.BlockSpec((ROW_TILE * PSLAB, LANES), lambda i: (i, 0)),
                   pl.BlockSpec((N_EXPERTS, ROW_TILE), lambda i: (0, i))],
        compiler_params=_cparams(("parallel",)),
        name="mixer_output",
    )(h_all, mod, oa, oa_ctx, ob, oc, oc_ctx, *wts)


def _moe_kernel(first_ref, count_ref, total_ref,
                tab_hbm, xp_ref, wg_ref, wu_ref, wd_ref,
                out_hbm,
                tab, gath, wgu_s, wd_s, ybuf, sem, tsem, *, t_pad, n_tok, lead_row):
    nfl = BLOCKS_IN_FLIGHT
    tmask = TAB_SLOTS - 1
    e = pl.program_id(0)
    total = total_ref[0]

    def row_copy(s, m, dst4):
        return pltpu.make_async_copy(ybuf.at[s, pl.ds(PSLAB * m, PSLAB), :],
                                     out_hbm.at[pl.ds(pl.multiple_of(dst4, PSLAB), PSLAB), :],
                                     sem.at[s])

    def wait_rows(s):
        for m in range(MOE_BLOCK):
            row_copy(s, m, 0).wait()

    def send_rows(ts, ys):
        for m in range(MOE_BLOCK):
            row_copy(ys, m, tab[ts, 0, MOE_BLOCK + m]).start(priority=m % 2)

    def tab_copy(row, s):
        return pltpu.make_async_copy(tab_hbm.at[row], tab.at[s], tsem.at[s])

    def gather_rows(ts, gs):
        for m in range(MOE_BLOCK):
            t4 = pl.multiple_of(tab[ts, 0, m], PSLAB)
            gath[gs, PSLAB * m:PSLAB * (m + 1), :] = xp_ref[pl.ds(t4, PSLAB), :]

    @pl.when(e == 0)
    def _():
        ybuf[...] = jnp.zeros_like(ybuf)
        tab_copy(lead_row, tmask).start()
        for r in range(TAB_AHEAD):
            tab_copy(r, r).start()
        for b in range(nfl - 1):
            for m in range(MOE_BLOCK):
                q = b * MOE_BLOCK + m
                row_copy(b, m, ((q % TOP_K) * t_pad + n_tok + q // TOP_K) * PSLAB).start(priority=m % 2)
        tab_copy(lead_row, tmask).wait()
        tab_copy(0, 0).wait()
        gather_rows(0, 0)

    wgu_s[:, 0:EXPERT_FF] = wg_ref[0, 0].astype(BF16)
    wgu_s[:, EXPERT_FF:2 * EXPERT_FF] = wu_ref[0, 0].astype(BF16)
    wd_s[...] = wd_ref[0, 0].astype(BF16)

    def block(j, carry):
        g = first_ref[e] + j
        tab_copy(g + 1, (g + 1) & tmask).wait()
        tab_copy(g + TAB_AHEAD, (g + TAB_AHEAD) & tmask).start()
        ys = lax.rem(g, nfl)
        wait_rows(ys)

        send_rows((g - 1) & tmask, lax.rem(g + nfl - 1, nfl))
        gather_rows((g + 1) & tmask, (g + 1) & 1)

        acc = None
        for jb, (lo, hi) in enumerate(_load_packed_slabs(gath.at[g & 1], MOE_BLOCK)):
            a = jnp.concatenate([lo, hi], axis=1).astype(BF16)
            t = _dot(a, wgu_s[jb * 2 * LANES:(jb + 1) * 2 * LANES, :])
            acc = t if acc is None else acc + t
        gt = acc[:, 0:EXPERT_FF]
        act = (gt * jax.nn.sigmoid(gt)) * acc[:, EXPERT_FF:2 * EXPERT_FF]
        y = _dot(act.astype(BF16), wd_s[...])
        _store_packed_slabs(ybuf.at[ys], y)
        return carry

    lax.fori_loop(0, count_ref[e], block, 0)

    @pl.when(e == pl.num_programs(0) - 1)
    def _():
        send_rows((total - 1) & tmask, lax.rem(total + nfl - 1, nfl))
        for s in range(nfl):
            wait_rows(s)
        for r in range(1, TAB_AHEAD):
            tab_copy(0, (total + r) & tmask).wait()


def _moe_call(first_blk, n_blk, total, table, xp4, wg, wu, wd, layer, n_tok):
    t_pad = n_tok + PAD_ROWS
    lead_row = table.shape[0] - 1
    wspec = lambda r, c: pl.BlockSpec((1, 1, r, c), lambda e, fb, nb, tt: (layer, e, 0, 0))
    grid_spec = pltpu.PrefetchScalarGridSpec(
        num_scalar_prefetch=3,
        grid=(N_EXPERTS,),
        in_specs=[
            pl.BlockSpec(memory_space=pl.ANY),
            pl.BlockSpec(memory_space=pltpu.VMEM),
            wspec(D_MODEL, EXPERT_FF), wspec(D_MODEL, EXPERT_FF), wspec(EXPERT_FF, D_MODEL),
        ],
        out_specs=pl.BlockSpec(memory_space=pl.ANY),
        scratch_shapes=[
            pltpu.SMEM((TAB_SLOTS, 1, 2 * MOE_BLOCK), jnp.int32),
            pltpu.VMEM((2, MOE_BLOCK * PSLAB, LANES), U32),
            pltpu.VMEM((D_MODEL, 2 * EXPERT_FF), BF16),
            pltpu.VMEM((EXPERT_FF, D_MODEL), BF16),
            pltpu.VMEM((BLOCKS_IN_FLIGHT, MOE_BLOCK * PSLAB, LANES), U32),
            pltpu.SemaphoreType.DMA((BLOCKS_IN_FLIGHT,)),
            pltpu.SemaphoreType.DMA((TAB_SLOTS,)),
        ],
    )
    return pl.pallas_call(
        functools.partial(_moe_kernel, t_pad=t_pad, n_tok=n_tok, lead_row=lead_row),
        out_shape=jax.ShapeDtypeStruct((TOP_K * t_pad * PSLAB, LANES), U32),
        grid_spec=grid_spec,
        compiler_params=_cparams(("arbitrary",)),
        name="routed_experts",
    )(first_blk, n_blk, total, table, xp4, wg, wu, wd)


def _combine_kernel(h1_ref, mod_ref, y8_ref, rw_ref, sg_ref, su_ref, sd_ref, g2_ref, b2_ref, o_ref,
                    *, alpha):
    h1 = h1_ref[...]
    m = mod_ref[0]
    u2 = (_layer_norm_rows(h1) * (1.0 + m[4:5, :]) + m[3:4, :]).astype(BF16)
    g = _dot(u2, sg_ref[...])
    act = (g * jax.nn.sigmoid(g)) * _dot(u2, su_ref[...])
    f = _dot(act.astype(BF16), sd_ref[...])
    tm = h1.shape[0]
    rw = rw_ref[...]
    wk = [jnp.broadcast_to(rw[:, kk:kk + 1], (tm, LANES)) for kk in range(TOP_K)]
    cols = [None] * (2 * PSLAB)
    for kk in range(TOP_K):
        for jb, pair in enumerate(_load_packed_slabs(y8_ref.at[kk], tm)):
            for half in range(2):
                t = wk[kk] * pair[half]
                c = 2 * jb + half
                cols[c] = t if cols[c] is None else cols[c] + t
    f = f + jnp.concatenate(cols, axis=1)
    o_ref[...] = _layer_norm_rows(alpha * h1 + m[5:6, :] * f) * g2_ref[...] + b2_ref[...]


def _combine_call(h1, mod, y8, rw, wts, rows_per_group, group_is_ctx_first, nb, alpha):
    rows = h1.shape[0]
    tm = COMBINE_TILE
    tpg = rows_per_group // tm
    if group_is_ctx_first:
        first = CTX_LEN // tm
        modi = lambda i: jnp.where(i % tpg < first, nb, i // tpg)
    else:
        modi = lambda i: i // tpg

    def full(a):
        return pl.BlockSpec(a.shape, lambda i: (0,) * a.ndim)

    return pl.pallas_call(
        functools.partial(_combine_kernel, alpha=alpha),
        out_shape=jax.ShapeDtypeStruct((rows, D_MODEL), F32),
        grid=(rows // tm,),
        in_specs=[pl.BlockSpec((tm, D_MODEL), lambda i: (i, 0)),
                  pl.BlockSpec((1, 8, D_MODEL), lambda i: (modi(i), 0, 0)),
                  pl.BlockSpec((TOP_K, tm * PSLAB, LANES), lambda i: (0, i, 0)),
                  pl.BlockSpec((tm, TOP_K), lambda i: (i, 0))]
                 + [full(w) for w in wts],
        out_specs=pl.BlockSpec((tm, D_MODEL), lambda i: (i, 0)),
        compiler_params=_cparams(("parallel",)),
        name="moe_combine",
    )(h1, mod, y8, rw, *wts)


_BIG_LANE = 1 << 30


def _route_kernel(lg_ref, b_ref, idx_o, w_o, rank_o, cnt_o, carry):
    i = pl.program_id(0)

    @pl.when(i == 0)
    def _():
        carry[...] = jnp.zeros_like(carry)

    tm = lg_ref.shape[1]
    gsize = N_EXPERTS // N_GROUPS
    scores = jax.nn.sigmoid(lg_ref[...])
    biased = scores + b_ref[...]
    eid = lax.broadcasted_iota(jnp.int32, (N_EXPERTS, tm), 0)

    b3 = biased.reshape(N_GROUPS, gsize, tm)
    in_g = lax.broadcasted_iota(jnp.int32, (N_GROUPS, gsize, tm), 1)
    m1 = jnp.max(b3, axis=1, keepdims=True)
    first = jnp.min(jnp.where(b3 == m1, in_g, _BIG_LANE), axis=1, keepdims=True)
    m2 = jnp.max(jnp.where(in_g == first, -jnp.inf, b3), axis=1, keepdims=True)
    gscore = (m1 + m2).reshape(N_GROUPS, tm)

    gid = lax.broadcasted_iota(jnp.int32, (N_GROUPS, tm), 0)
    beaten = jnp.zeros((N_GROUPS, tm), jnp.int32)
    for g in range(N_GROUPS):
        sg = gscore[g:g + 1, :]
        ahead = jnp.logical_or(sg > gscore, jnp.logical_and(sg == gscore, g < gid))
        beaten = beaten + ahead.astype(jnp.int32)
    keep = jnp.broadcast_to((beaten < TOPK_GROUPS).astype(jnp.int32).reshape(N_GROUPS, 1, tm),
                            (N_GROUPS, gsize, tm)).reshape(N_EXPERTS, tm)
    masked = jnp.where(keep > 0, biased, -jnp.inf)

    idxs, ws, hots = [], [], []
    for _ in range(TOP_K):
        m = jnp.max(masked, axis=0, keepdims=True)
        ix = jnp.min(jnp.where(masked == m, eid, _BIG_LANE), axis=0, keepdims=True)
        hot = eid == ix
        idxs.append(ix)
        ws.append(jnp.sum(jnp.where(hot, scores, 0.0), axis=0, keepdims=True))
        hots.append(hot)
        masked = jnp.where(hot, -jnp.inf, masked)
    wsum = ws[0]
    for r in range(1, TOP_K):
        wsum = wsum + ws[r]
    idx_o[...] = jnp.concatenate(idxs, axis=0)
    w_rows = jnp.concatenate([wr / wsum * ROUTED_SCALE for wr in ws], axis=0)

    eye = (lax.broadcasted_iota(jnp.int32, (tm, tm), 0)
           == lax.broadcasted_iota(jnp.int32, (tm, tm), 1)).astype(BF16)
    w_cols = jnp.zeros((tm, TOP_K), F32)
    rest = w_rows
    for _ in range(3):
        part = rest.astype(BF16)
        rest = rest - part.astype(F32)
        w_cols = w_cols + _dot_nt(eye, part)
    w_o[...] = w_cols

    sel = jnp.zeros((N_EXPERTS, tm), F32)
    for hot in hots:
        sel = sel + hot.astype(F32)
    sel = sel.astype(BF16)
    earlier = (lax.broadcasted_iota(jnp.int32, (tm, tm), 0)
               < lax.broadcasted_iota(jnp.int32, (tm, tm), 1)).astype(BF16)
    prefix = _dot(sel, earlier) + carry[:, 0:1]
    rank_o[...] = jnp.concatenate(
        [jnp.sum(jnp.where(hot, prefix, 0.0), axis=0, keepdims=True) for hot in hots],
        axis=0).astype(jnp.int32)
    carry[...] = carry[...] + _dot(sel, jnp.ones((tm, LANES), BF16))
    cnt_o[...] = carry[...]


def _route_call(logits_t, b_r):
    t = logits_t.shape[1]
    tm = ROW_TILE
    kt = pl.BlockSpec((TOP_K, tm), lambda i: (0, i))
    return pl.pallas_call(
        _route_kernel,
        out_shape=[jax.ShapeDtypeStruct((TOP_K, t), jnp.int32),
                   jax.ShapeDtypeStruct((t, TOP_K), F32),
                   jax.ShapeDtypeStruct((TOP_K, t), jnp.int32),
                   jax.ShapeDtypeStruct((N_EXPERTS, LANES), F32)],
        grid=(t // tm,),
        in_specs=[pl.BlockSpec((N_EXPERTS, tm), lambda i: (0, i)),
                  pl.BlockSpec((N_EXPERTS, 1), lambda i: (0, 0))],
        out_specs=[kt, pl.BlockSpec((tm, TOP_K), lambda i: (i, 0)), kt,
                   pl.BlockSpec((N_EXPERTS, LANES), lambda i: (0, 0))],
        scratch_shapes=[pltpu.VMEM((N_EXPERTS, LANES), F32)],
        compiler_params=_cparams(("arbitrary",)),
        name="route_topk",
    )(logits_t, b_r.astype(F32).reshape(N_EXPERTS, 1))


def _dest_kernel(idx_ref, rank_ref, start_ref, o_ref):
    tm = idx_ref.shape[1]
    eid = lax.broadcasted_iota(jnp.int32, (N_EXPERTS, tm), 0)
    idx = idx_ref[...]
    start = start_ref[...]
    rows = [jnp.sum(jnp.where(eid == idx[r:r + 1, :], start, 0), axis=0, keepdims=True)
            for r in range(TOP_K)]
    o_ref[...] = jnp.concatenate(rows, axis=0) + rank_ref[...]


def _dest_call(idx, rank, pad_start):
    t = idx.shape[1]
    tm = ROW_TILE
    blk = pl.BlockSpec((TOP_K, tm), lambda i: (0, i))
    return pl.pallas_call(
        _dest_kernel,
        out_shape=jax.ShapeDtypeStruct((TOP_K, t), jnp.int32),
        grid=(t // tm,),
        in_specs=[blk, blk, pl.BlockSpec((N_EXPERTS, 1), lambda i: (0, 0))],
        out_specs=blk,
        compiler_params=_cparams(("parallel",)),
        name="slot_of_assignment",
    )(idx, rank, pad_start.reshape(N_EXPERTS, 1))


def _dispatch(idx, rank, counts, n_tok):
    n_assign = n_tok * TOP_K
    used_max = (n_assign + N_EXPERTS * (MOE_BLOCK - 1) + MOE_BLOCK - 1) // MOE_BLOCK
    nblk = -(-(used_max + TAB_AHEAD + 1) // BLOCKS_IN_FLIGHT) * BLOCKS_IN_FLIGHT
    n_slots = nblk * MOE_BLOCK
    counts = counts[:, 0].astype(jnp.int32)
    padded = (counts + MOE_BLOCK - 1) // MOE_BLOCK * MOE_BLOCK
    pad_end = jnp.cumsum(padded)
    pad_start = pad_end - padded
    dest = _dest_call(idx, rank, pad_start)
    assign = (jnp.arange(n_tok, dtype=jnp.int32)[None, :] * TOP_K
              + jnp.arange(TOP_K, dtype=jnp.int32)[:, None])
    assert (nblk - 1) % BLOCKS_IN_FLIGHT == BLOCKS_IN_FLIGHT - 1
    pad_a = n_assign + jnp.arange(n_slots, dtype=jnp.int32) % (BLOCKS_IN_FLIGHT * MOE_BLOCK)
    slot_a = pad_a.at[dest.reshape(-1)].set(assign.reshape(-1), unique_indices=True,
                                            mode='promise_in_bounds')
    t_pad = n_tok + PAD_ROWS
    tok = lax.shift_right_logical(slot_a, K_SHIFT)
    tok4 = jnp.minimum(tok, n_tok - 1) * PSLAB
    dst4 = ((slot_a & (TOP_K - 1)) * t_pad + tok) * PSLAB
    table = jnp.concatenate([tok4.reshape(nblk, 1, MOE_BLOCK), dst4.reshape(nblk, 1, MOE_BLOCK)], axis=2)
    total = (pad_end[-1] // MOE_BLOCK).astype(jnp.int32).reshape(1)
    return pad_start // MOE_BLOCK, padded // MOE_BLOCK, total, table


def _rope_tables(s, ntok):
    rows_n = s // GRID_W
    row = jnp.repeat(jnp.arange(rows_n, dtype=F32), GRID_W)
    col = jnp.tile(jnp.arange(GRID_W, dtype=F32), rows_n)
    axis_dim = HEAD_DIM // 2
    inv = jnp.power(ROPE_THETA, -jnp.arange(0, axis_dim, 2, dtype=F32) / axis_dim)
    ar = row[:, None] * inv[None]
    ac = col[:, None] * inv[None]
    ang = jnp.concatenate([ar, ar, ac, ac], -1)
    cos, sin = jnp.cos(ang), jnp.sin(ang)
    quarter = (jnp.arange(HEAD_DIM) // 16) % 2
    s_up = jnp.where(quarter == 0, -sin, 0.0)
    s_dn = jnp.where(quarter == 1, sin, 0.0)
    nctx = ntok - s

    def expand(t, ctx_val):
        t = jnp.concatenate([jnp.full((nctx, HEAD_DIM), ctx_val, F32), t], axis=0)
        return jnp.tile(t, (1, LANES // HEAD_DIM))

    return expand(cos, 1.0), expand(s_up, 0.0), expand(s_dn, 0.0)


def _head_mean_matrix(width):
    hid = np.arange(width) // HEAD_DIM
    return jnp.asarray((hid[:, None] == hid[None, :]).astype(np.float32) / HEAD_DIM).astype(BF16)


def _dup_heads(a, n_heads):
    parts = []
    for hd in range(n_heads):
        p = a[..., hd * HEAD_DIM:(hd + 1) * HEAD_DIM]
        parts += [p, p]
    return jnp.concatenate(parts, axis=-1)


def kernel(x, c, ctx, c_ctx, w_mod, b_mod, w_in, qn_a, kn_a, lam_q1, lam_k1, lam_q2, lam_k2, subln_c, w_br_a, w_br_b, w_br_c, w_out, ln1_g, ln1_b, w_router, b_router, w_sh_gate, w_sh_up, w_sh_down, w_e_gate, w_e_up, w_e_down, ln2_g, ln2_b):
    nb, s, d = x.shape
    lc = ctx.shape[1]
    depth = w_mod.shape[0]
    assert d == D_MODEL and lc == CTX_LEN and s % ROW_TILE == 0 and s % GRID_W == 0
    ntok = lc + s
    alpha = (2 * depth) ** 0.25

    tabs = _rope_tables(s, ntok)
    pos_lat = _dft_tables(s)
    pos_ctx = _dft_tables(lc)
    chan = _channel_table()
    e_q = _head_mean_matrix(A_Q_W)
    e_k = _head_mean_matrix(2 * A_KV_W)

    cc = jnp.concatenate([c, c_ctx[None, :]], axis=0)
    cc = jnp.pad(cc, ((0, (-(nb + 1)) % 8), (0, 0)))
    h_all = jnp.concatenate([ctx, x], axis=1).reshape(nb * ntok, d)

    offs = np.cumsum([0, A_Q_W, A_KV_W, A_KV_W, B_W, C_QK_W, C_QK_W, C_V_W, GATE_W])
    out = None
    for l in range(depth):
        last = l == depth - 1
        mod = _mod_call(cc, w_mod[l], b_mod[l])[:nb + 1].reshape(nb + 1, 6, d)
        mod = jnp.pad(mod, ((0, 0), (0, 2), (0, 0)))
        lam_init = 0.8 - 0.6 * math.exp(-0.3 * l)
        lam = (jnp.exp(jnp.sum(lam_q1[l].astype(F32) * lam_k1[l].astype(F32)))
               - jnp.exp(jnp.sum(lam_q2[l].astype(F32) * lam_k2[l].astype(F32)))) + lam_init
        lam = lam.reshape(1, 1).astype(F32)

        wl = w_in[l]
        seg = [wl[:, offs[i]:offs[i + 1]] for i in range(8)]
        in_wts = (seg[0].astype(BF16), _dup_heads(seg[1], A_KV_HEADS).astype(BF16),
                  _dup_heads(seg[2], A_KV_HEADS).astype(BF16), seg[3].astype(BF16),
                  seg[4].astype(BF16), seg[5].astype(BF16), seg[6].astype(BF16),
                  jnp.tile(qn_a[l].astype(F32), A_Q_HEADS).reshape(1, A_Q_W),
                  jnp.tile(kn_a[l].astype(F32), 2 * A_KV_HEADS).reshape(1, 2 * A_KV_W),
                  e_q, e_k)
        qa, ka, va, fb, qc, kc, vc = _inproj_call(h_all, mod, tabs, in_wts, nb, ntok)

        first_tile = 1 if last else 0
        r3 = lambda a: a.reshape(nb, ntok, a.shape[-1])
        subln = subln_c[l].astype(F32).reshape(1, C_V_DIM)
        attn_a = functools.partial(_attn_a_call, r3(qa), ka, r3(va), nb, ntok)
        attn_c = functools.partial(_attn_c_call, lam, r3(qc), kc, r3(vc), subln, nb, ntok)
        oa, oc = attn_a(False), attn_c(False, 1.0 - lam_init)
        oa_ctx, oc_ctx = (oa, oc) if last else (attn_a(True), attn_c(True, 1.0 - lam_init))
        ob = _fourier_call(r3(fb), chan, pos_lat, pos_ctx, nb, ntok, first_tile)

        mix_wts = (seg[7].astype(BF16), w_br_a[l].astype(BF16), w_br_b[l].astype(BF16),
                   w_br_c[l].astype(BF16), w_out[l].astype(BF16),
                   ln1_g[l].astype(F32).reshape(1, d), ln1_b[l].astype(F32).reshape(1, d),
                   w_router[l].T.astype(BF16))
        flat = lambda a: a.reshape(-1, a.shape[-1])
        h1, xp, logits = _mixout_call(h_all, mod, flat(oa), flat(oa_ctx), flat(ob), flat(oc),
                                      flat(oc_ctx), mix_wts, nb, ntok, last, alpha)

        n_tok = h1.shape[0]
        idx, rw, rank, counts = _route_call(logits, b_router[l])
        first_blk, n_blk, total, table = _dispatch(idx, rank, counts, n_tok)
        y8 = _moe_call(first_blk, n_blk, total, table, xp,
                       w_e_gate, w_e_up, w_e_down, l, n_tok)
        y8 = y8.reshape(TOP_K, (n_tok + PAD_ROWS) * PSLAB, LANES)

        comb_wts = (w_sh_gate[l].astype(BF16), w_sh_up[l].astype(BF16), w_sh_down[l].astype(BF16),
                    ln2_g[l].astype(F32).reshape(1, d), ln2_b[l].astype(F32).reshape(1, d))
        h2 = _combine_call(h1, mod, y8, rw, comb_wts, s if last else ntok, not last, nb, alpha)
        if last:
            out = h2.reshape(nb, s, d)
        else:
            h_all = h2
    return out
```

```python
import functools
import math

import numpy as np
import jax
import jax.numpy as jnp
from jax import lax
from jax.experimental import pallas as pl
from jax.experimental.pallas import tpu as pltpu

F32 = jnp.float32
BF16 = jnp.bfloat16
U32 = jnp.uint32

D_MODEL = 1024
CTX_LEN = 256
GRID_W = 64
HEAD_DIM = 64
ROPE_THETA = 10000.0
A_Q_HEADS = 8
A_KV_HEADS = 2
A_Q_W = A_Q_HEADS * HEAD_DIM
A_KV_W = A_KV_HEADS * HEAD_DIM
F_GROUPS = 4
F_GROUP_W = 128
B_W = F_GROUPS * F_GROUP_W
C_HEADS = 4
C_V_DIM = 2 * HEAD_DIM
C_QK_W = C_HEADS * 2 * HEAD_DIM
C_V_W = C_HEADS * C_V_DIM
N_BRANCH = 3
GATE_W = N_BRANCH * D_MODEL
N_EXPERTS = 256
TOP_K = 8
N_GROUPS = 8
TOPK_GROUPS = 4
EXPERT_FF = 256
SHARED_FF = 256
ROUTED_SCALE = 2.5
LN_EPS = 1e-5
RMS_EPS = 1e-6
K_SHIFT = TOP_K.bit_length() - 1
assert 1 << K_SHIFT == TOP_K

LANES = 128
PSLAB = D_MODEL // (2 * LANES)
ROW_TILE = 256
ATTN_TILE = 512
MOE_BLOCK = 128
BLOCKS_IN_FLIGHT = 4
TAB_AHEAD = 4
TAB_SLOTS = 8
PAD_ROWS = BLOCKS_IN_FLIGHT * MOE_BLOCK // TOP_K
COMBINE_TILE = 256
VMEM_LIMIT = 56 * 1024 * 1024

_Q_SCALE = HEAD_DIM ** -0.5 * math.log2(math.e)


def _cparams(sem):
    return pltpu.CompilerParams(dimension_semantics=sem, vmem_limit_bytes=VMEM_LIMIT)


def _dot(a, b):
    return jnp.dot(a, b, preferred_element_type=F32)


def _dot_nt(a, b):
    return lax.dot_general(a, b, (((1,), (1,)), ((), ())), preferred_element_type=F32)


def _layer_norm_rows(x):
    mu = jnp.mean(x, axis=-1, keepdims=True)
    xc = x - mu
    var = jnp.mean(xc * xc, axis=-1, keepdims=True)
    return xc * lax.rsqrt(var + LN_EPS)


def _dot_split(x, e):
    hi = x.astype(BF16)
    lo = (x - hi.astype(F32)).astype(BF16)
    return _dot(hi, e) + _dot(lo, e)


def _mod_kernel(c_ref, w_ref, b_ref, o_ref):
    c = c_ref[...]
    sc = c * jax.nn.sigmoid(c)
    o_ref[...] = _dot(sc.astype(BF16), w_ref[...].astype(BF16)) + b_ref[...]


def _mod_call(cc, w_mod_l, b_mod_l):
    r = cc.shape[0]
    n = w_mod_l.shape[1]
    tn = D_MODEL
    return pl.pallas_call(
        _mod_kernel,
        out_shape=jax.ShapeDtypeStruct((r, n), F32),
        grid=(n // tn,),
        in_specs=[
            pl.BlockSpec((r, D_MODEL), lambda j: (0, 0)),
            pl.BlockSpec((D_MODEL, tn), lambda j: (0, j)),
            pl.BlockSpec((1, tn), lambda j: (0, j)),
        ],
        out_specs=pl.BlockSpec((r, tn), lambda j: (0, j)),
        compiler_params=_cparams(("arbitrary",)),
        name="mod_vectors",
    )(cc, w_mod_l, b_mod_l.reshape(1, n))


def _rope_cols(x, cos, sin_up, sin_dn):
    cols = []
    for c in range(x.shape[1] // LANES):
        xc = x[:, c * LANES:(c + 1) * LANES]
        up = pltpu.roll(xc, LANES - 16, axis=1)
        dn = pltpu.roll(xc, 16, axis=1)
        cols.append(xc * cos + up * sin_up + dn * sin_dn)
    return jnp.concatenate(cols, axis=1) if len(cols) > 1 else cols[0]


def _inproj_kernel(h_ref, mod_ref, cos_ref, su_ref, sd_ref,
                   wq_ref, wk_ref, wv_ref, wf_ref, wqc_ref, wkc_ref, wvc_ref,
                   qn_ref, kn_ref, eq_ref, ek_ref,
                   qa_o, ka_o, va_o, fb_o, qc_o, kc_o, vc_o):
    h = h_ref[...]
    shift = mod_ref[0, 0:1, :]
    scale = mod_ref[0, 1:2, :]
    u = (_layer_norm_rows(h) * (1.0 + scale) + shift).astype(BF16)
    cos = cos_ref[...]
    s_up = su_ref[...]
    s_dn = sd_ref[...]

    q = _dot(u, wq_ref[...])
    ms = _dot_split(q * q, eq_ref[...])
    q = q * lax.rsqrt(ms + RMS_EPS) * qn_ref[...]
    qa_o[...] = (_rope_cols(q, cos, s_up, s_dn) * _Q_SCALE).astype(BF16)

    k = _dot(u, wk_ref[...])
    ms = _dot_split(k * k, ek_ref[...])
    k = k * lax.rsqrt(ms + RMS_EPS) * kn_ref[...]
    ka_o[0] = jnp.transpose(_rope_cols(k, cos, s_up, s_dn)).astype(BF16)

    va_o[...] = _dot(u, wv_ref[...]).astype(BF16)
    fb_o[...] = _dot(u, wf_ref[...]).astype(BF16)
    qc = _dot(u, wqc_ref[...])
    qc_o[...] = (_rope_cols(qc, cos, s_up, s_dn) * _Q_SCALE).astype(BF16)
    kc = _dot(u, wkc_ref[...])
    kc_o[0] = jnp.transpose(_rope_cols(kc, cos, s_up, s_dn)).astype(BF16)
    vc_o[...] = _dot(u, wvc_ref[...]).astype(BF16)


def _inproj_call(h_all, mod, tabs, wts, nb, ntok):
    rows = h_all.shape[0]
    tpb = ntok // ROW_TILE
    cos, s_up, s_dn = tabs

    def full(a):
        return pl.BlockSpec(a.shape, lambda i: (0,) * a.ndim)

    def rowspec(w):
        return pl.BlockSpec((ROW_TILE, w), lambda i: (i, 0))

    tabspec = pl.BlockSpec((ROW_TILE, LANES), lambda i: (i % tpb, 0))
    modspec = pl.BlockSpec((1, 8, D_MODEL),
                           lambda i: (jnp.where(i % tpb == 0, nb, i // tpb), 0, 0))
    widths = (A_Q_W, 2 * A_KV_W, 2 * A_KV_W, B_W, C_QK_W, C_QK_W, C_V_W)
    transposed = (1, 5)

    def oshape(n, w):
        return (nb, w, ntok) if n in transposed else (rows, w)

    def ospec(n, w):
        if n in transposed:
            return pl.BlockSpec((1, w, ROW_TILE), lambda i: (i // tpb, 0, i % tpb))
        return rowspec(w)

    return pl.pallas_call(
        _inproj_kernel,
        out_shape=[jax.ShapeDtypeStruct(oshape(n, w), BF16) for n, w in enumerate(widths)],
        grid=(rows // ROW_TILE,),
        in_specs=[rowspec(D_MODEL), modspec, tabspec, tabspec, tabspec]
                 + [full(w) for w in wts],
        out_specs=[ospec(n, w) for n, w in enumerate(widths)],
        compiler_params=_cparams(("parallel",)),
        name="in_projection",
    )(h_all, mod, cos, s_up, s_dn, *wts)


def _softmax_parts(s):
    m = jnp.max(s, axis=-1, keepdims=True)
    e = jnp.exp2(s - m)
    return e, jnp.sum(e, axis=-1, keepdims=True)


def _query_rows(q_refs):
    rows = [r[0] for r in q_refs]
    return rows[0] if len(rows) == 1 else jnp.concatenate(rows, axis=0)


def _gqa_kernel(*refs):
    q_refs, (k_ref, v_ref, o_ref) = refs[:-3], refs[-3:]
    q = _query_rows(q_refs)
    lane = lax.broadcasted_iota(jnp.int32, (1, LANES), 1)
    low = lane < HEAD_DIM
    for c in range(A_Q_W // LANES):
        kvh = (2 * c) // (A_Q_HEADS // A_KV_HEADS)
        qc = q[:, c * LANES:(c + 1) * LANES]
        kk = k_ref[0, kvh * LANES:(kvh + 1) * LANES, :]
        vv = v_ref[0, :, kvh * LANES:(kvh + 1) * LANES]
        halves = []
        for keep in (low, jnp.logical_not(low)):
            qm = jnp.where(keep, qc, jnp.zeros_like(qc))
            e, l = _softmax_parts(_dot(qm, kk))
            halves.append(_dot(e.astype(BF16), vv) * (1.0 / l))
        o_ref[0, :, c * LANES:(c + 1) * LANES] = jnp.where(low, halves[0], halves[1]).astype(BF16)


def _diff_kernel(lam_ref, *refs, out_scale):
    q_refs, (k_ref, v_ref, g_ref, o_ref) = refs[:-4], refs[-4:]
    q = _query_rows(q_refs)
    lane = lax.broadcasted_iota(jnp.int32, (1, LANES), 1)
    low = lane < HEAD_DIM
    lam = lam_ref[0, 0]
    for hd in range(C_HEADS):
        sl = slice(hd * LANES, (hd + 1) * LANES)
        qc = q[:, sl]
        kk = k_ref[0, sl, :]
        vv = v_ref[0, :, sl]
        q1 = jnp.where(low, qc, jnp.zeros_like(qc))
        q2 = jnp.where(low, jnp.zeros_like(qc), qc)
        e1, l1 = _softmax_parts(_dot(q1, kk))
        e2, l2 = _softmax_parts(_dot(q2, kk))
        o = _dot(e1.astype(BF16), vv) * (1.0 / l1) - _dot(e2.astype(BF16), vv) * (lam / l2)
        ms = jnp.mean(o * o, axis=-1, keepdims=True)
        o = o * lax.rsqrt(ms + RMS_EPS) * g_ref[...] * out_scale
        o_ref[0, :, sl] = o.astype(BF16)


def _attn_call(kernel_fn, name, head, tail, q, kt, v, nb, ntok, wq, wkv, ctx_queries):
    cblk = CTX_LEN // ROW_TILE
    if ctx_queries:
        rows, nk, grid = CTX_LEN, CTX_LEN, (nb, 1)
        qspecs = [pl.BlockSpec((1, ROW_TILE, wq), lambda b, j: (b, 0, 0))]
    else:
        rows, nk, grid = ATTN_TILE, ntok, (nb, (ntok - CTX_LEN) // ATTN_TILE)
        per = ATTN_TILE // ROW_TILE
        qspecs = [pl.BlockSpec((1, ROW_TILE, wq), lambda b, j, r=r: (b, cblk + per * j + r, 0))
                  for r in range(per)]
    kspec = pl.BlockSpec((1, wkv, nk), lambda b, j: (b, 0, 0))
    vspec = pl.BlockSpec((1, nk, wkv), lambda b, j: (b, 0, 0))
    return pl.pallas_call(
        kernel_fn,
        out_shape=jax.ShapeDtypeStruct((nb, rows * grid[1], wq), BF16),
        grid=grid,
        in_specs=[sp for _, sp in head] + qspecs + [kspec, vspec] + [sp for _, sp in tail],
        out_specs=pl.BlockSpec((1, rows, wq), lambda b, j: (b, j, 0)),
        compiler_params=_cparams(("parallel", "arbitrary")),
        name=name,
    )(*[a for a, _ in head], *([q] * len(qspecs)), kt, v, *[a for a, _ in tail])


def _attn_a_call(qa, ka, va, nb, ntok, ctx_queries):
    return _attn_call(_gqa_kernel, "gqa_attention_ctx" if ctx_queries else "gqa_attention", [], [],
                      qa, ka, va, nb, ntok, A_Q_W, 2 * A_KV_W, ctx_queries)


def _attn_c_call(lam, qc, kc, vc, subln, nb, ntok, ctx_queries, out_scale):
    head = [(lam, pl.BlockSpec(memory_space=pltpu.SMEM))]
    tail = [(subln, pl.BlockSpec((1, C_V_DIM), lambda b, j: (0, 0)))]
    return _attn_call(functools.partial(_diff_kernel, out_scale=out_scale),
                      "diff_attention_ctx" if ctx_queries else "diff_attention", head, tail,
                      qc, kc, vc, nb, ntok, C_QK_W, C_QK_W, ctx_queries)


def _fourier_kernel(f_ref, chan_ref, pos_ref, posc_ref, o_ref, g_ref, *, ntok, first_tile):
    j = pl.program_id(1) + first_tile
    nlat = ntok - CTX_LEN

    def channel_stage(rows):
        g = _dot(rows, chan_ref[...])
        return jnp.concatenate([g[:, :B_W], g[:, B_W:]], axis=0).astype(BF16)

    if first_tile == 0:
        @pl.when(j == 0)
        def _():
            gc = channel_stage(f_ref[0, 0:CTX_LEN, :])
            y = _dot(posc_ref[...], gc) * (1.0 / math.sqrt(CTX_LEN * F_GROUP_W))
            o_ref[0] = y.astype(BF16)

    @pl.when(j == 1)
    def _():
        g_ref[...] = channel_stage(f_ref[0, CTX_LEN:ntok, :])

    @pl.when(j >= 1)
    def _():
        y = _dot(pos_ref[...], g_ref[...]) * (1.0 / math.sqrt(nlat * F_GROUP_W))
        o_ref[0] = y.astype(BF16)


def _fourier_call(fb, chan, pos, posc, nb, ntok, first_tile):
    tpb = ntok // ROW_TILE
    nlat = ntok - CTX_LEN
    return pl.pallas_call(
        functools.partial(_fourier_kernel, ntok=ntok, first_tile=first_tile),
        out_shape=jax.ShapeDtypeStruct((nb, ntok - first_tile * ROW_TILE, B_W), BF16),
        grid=(nb, tpb - first_tile),
        in_specs=[
            pl.BlockSpec((1, ntok, B_W), lambda b, j: (b, 0, 0)),
            pl.BlockSpec(chan.shape, lambda b, j: (0, 0)),
            pl.BlockSpec((ROW_TILE, 2 * nlat),
                         lambda b, j: (jnp.maximum(j + first_tile - 1, 0), 0)),
            pl.BlockSpec(posc.shape, lambda b, j: (0, 0)),
        ],
        out_specs=pl.BlockSpec((1, ROW_TILE, B_W), lambda b, j: (b, j, 0)),
        scratch_shapes=[pltpu.VMEM((2 * nlat, B_W), BF16)],
        compiler_params=_cparams(("parallel", "arbitrary")),
        name="fourier_mix",
    )(fb, chan, pos, posc)


def _dft_tables(n):
    n1 = 32
    n0 = n // n1
    k = np.arange(n, dtype=np.int64)
    a = 2.0 * np.pi * ((k[:, None] * np.arange(n1)[None, :] * n0) % n) / n
    b = 2.0 * np.pi * ((k[:, None] * np.arange(n0)[None, :]) % n) / n
    ca, sa = jnp.asarray(np.cos(a), F32)[:, :, None], jnp.asarray(np.sin(a), F32)[:, :, None]
    cb, sb = jnp.asarray(np.cos(b), F32)[:, None, :], jnp.asarray(np.sin(b), F32)[:, None, :]
    cos = (ca * cb - sa * sb).reshape(n, n)
    sin = (sa * cb + ca * sb).reshape(n, n)
    return jnp.concatenate([cos, -sin], axis=1).astype(BF16)


def _channel_table():
    c = np.arange(F_GROUP_W)
    ang = 2.0 * np.pi * ((c[:, None] * c[None, :]) % F_GROUP_W) / F_GROUP_W
    eye = np.eye(F_GROUPS)
    cos = np.kron(eye, np.cos(ang))
    sin = np.kron(eye, np.sin(ang))
    return jnp.asarray(np.concatenate([cos, sin], axis=1), F32).astype(BF16)


def _store_packed_slabs(dst_ref, u):
    rows = u.shape[0]
    for jb in range(PSLAB):
        lo = u[:, 2 * jb * LANES:(2 * jb + 1) * LANES]
        hi = u[:, (2 * jb + 1) * LANES:(2 * jb + 2) * LANES]
        dst_ref[pl.ds(jb, rows, stride=PSLAB), :] = pltpu.pack_elementwise([lo, hi], packed_dtype=BF16)


def _load_packed_slabs(src_ref, rows):
    out = []
    for jb in range(PSLAB):
        words = src_ref[pl.ds(jb, rows, stride=PSLAB), :]
        out.append((pltpu.unpack_elementwise(words, index=0, packed_dtype=BF16, unpacked_dtype=F32),
                    pltpu.unpack_elementwise(words, index=1, packed_dtype=BF16, unpacked_dtype=F32)))
    return out


def _mixout_kernel(h_ref, mod_ref, oa_ref, oa_ctx_ref, ob_ref, oc_ref, oc_ctx_ref,
                   wg_ref, wa_ref, wb_ref, wc_ref, wo_ref, g1_ref, b1_ref, wr_ref,
                   h1_o, xp_o, lg_o, *, alpha, ctx_every):
    h = h_ref[...]
    m = mod_ref[0]
    u = (_layer_norm_rows(h) * (1.0 + m[1:2, :]) + m[0:1, :]).astype(BF16)
    oa, oc = oa_ref[...], oc_ref[...]
    if ctx_every:
        is_ctx = pl.program_id(0) % ctx_every == 0
        oa = jnp.where(is_ctx, oa_ctx_ref[...], oa)
        oc = jnp.where(is_ctx, oc_ctx_ref[...], oc)
    y = None
    for n, (o, w_ref) in enumerate(((oa, wa_ref), (ob_ref[...], wb_ref), (oc, wc_ref))):
        gate = jax.nn.sigmoid(_dot(u, wg_ref[:, n * D_MODEL:(n + 1) * D_MODEL]))
        t = gate * _dot(o, w_ref[...])
        y = t if y is None else y + t
    z = _dot(y.astype(BF16), wo_ref[...])
    h1 = _layer_norm_rows(alpha * h + m[2:3, :] * z) * g1_ref[...] + b1_ref[...]
    h1_o[...] = h1
    u2 = _layer_norm_rows(h1) * (1.0 + m[4:5, :]) + m[3:4, :]
    _store_packed_slabs(xp_o, u2)
    lg_o[...] = _dot_nt(wr_ref[...], u2.astype(BF16))


def _tile_maps(nb, ntok, lat_only):
    tpb = ntok // ROW_TILE
    if lat_only:
        lpb = tpb - 1
        n_tiles = nb * lpb
        src = lambda i: (i // lpb) * tpb + 1 + i % lpb
        modi = lambda i: i // lpb
    else:
        n_tiles = nb * tpb
        src = lambda i: i
        modi = lambda i: jnp.where(i % tpb == 0, nb, i // tpb)
    return n_tiles, src, modi


def _mixout_call(h_all, mod, oa, oa_ctx, ob, oc, oc_ctx, wts, nb, ntok, lat_only, alpha):
    n_tiles, src, modi = _tile_maps(nb, ntok, lat_only)
    rows_out = n_tiles * ROW_TILE
    tpb = ntok // ROW_TILE
    lpb = tpb - CTX_LEN // ROW_TILE
    if lat_only:
        lat_blk = lambda i: i
        ctx_blk = lambda i: 0
    else:
        lat_blk = lambda i: (i // tpb) * lpb + jnp.maximum(i % tpb - 1, 0)
        ctx_blk = lambda i: i // tpb

    def full(a):
        return pl.BlockSpec(a.shape, lambda i: (0,) * a.ndim)

    def inrow(w):
        return pl.BlockSpec((ROW_TILE, w), lambda i: (src(i), 0))

    def outrow(w):
        return pl.BlockSpec((ROW_TILE, w), lambda i: (i, 0))

    modspec = pl.BlockSpec((1, 8, D_MODEL), lambda i: (modi(i), 0, 0))
    def latrow(w):
        return pl.BlockSpec((ROW_TILE, w), lambda i: (lat_blk(i), 0))

    def ctxrow(w):
        return pl.BlockSpec((ROW_TILE, w), lambda i: (ctx_blk(i), 0))

    return pl.pallas_call(
        functools.partial(_mixout_kernel, alpha=alpha, ctx_every=0 if lat_only else tpb),
        out_shape=[jax.ShapeDtypeStruct((rows_out, D_MODEL), F32),
                   jax.ShapeDtypeStruct((rows_out * PSLAB, LANES), U32),
                   jax.ShapeDtypeStruct((N_EXPERTS, rows_out), F32)],
        grid=(n_tiles,),
        in_specs=[inrow(D_MODEL), modspec, latrow(A_Q_W), ctxrow(A_Q_W), outrow(B_W),
                  latrow(C_V_W), ctxrow(C_V_W)]
                 + [full(w) for w in wts],
        out_specs=[outrow(D_MODEL), pl.BlockSpec((ROW_TILE * PSLAB, LANES), lambda i: (i, 0)),
                   pl.BlockSpec((N_EXPERTS, ROW_TILE), lambda i: (0, i))],
        compiler_params=_cparams(("parallel",)),
        name="mixer_output",
    )(h_all, mod, oa, oa_ctx, ob, oc, oc_ctx, *wts)


def _moe_kernel(first_ref, count_ref, total_ref,
                tab_hbm, xp_ref, wg_ref, wu_ref, wd_ref,
                out_hbm,
                tab, gath, wgu_s, wd_s, ybuf, sem, tsem, *, t_pad, n_tok, lead_row):
    nfl = BLOCKS_IN_FLIGHT
    tmask = TAB_SLOTS - 1
    e = pl.program_id(0)
    total = total_ref[0]

    def row_copy(s, m, dst4):
        return pltpu.make_async_copy(ybuf.at[s, pl.ds(PSLAB * m, PSLAB), :],
                                     out_hbm.at[pl.ds(pl.multiple_of(dst4, PSLAB), PSLAB), :],
                                     sem.at[s])

    def wait_rows(s):
        for m in range(MOE_BLOCK):
            row_copy(s, m, 0).wait()

    def send_rows(ts, ys):
        for m in range(MOE_BLOCK):
            row_copy(ys, m, tab[ts, 1, m]).start(priority=m % 2)

    def tab_copy(row, s):
        return pltpu.make_async_copy(tab_hbm.at[row], tab.at[s], tsem.at[s])

    def gather_rows(ts, gs):
        for m in range(MOE_BLOCK):
            t4 = pl.multiple_of(tab[ts, 0, m], PSLAB)
            gath[gs, PSLAB * m:PSLAB * (m + 1), :] = xp_ref[pl.ds(t4, PSLAB), :]

    @pl.when(e == 0)
    def _():
        ybuf[...] = jnp.zeros_like(ybuf)
        tab_copy(lead_row, tmask).start()
        for r in range(TAB_AHEAD):
            tab_copy(r, r).start()
        for b in range(nfl - 1):
            for m in range(MOE_BLOCK):
                q = b * MOE_BLOCK + m
                row_copy(b, m, ((q % TOP_K) * t_pad + n_tok + q // TOP_K) * PSLAB).start(priority=m % 2)
        tab_copy(lead_row, tmask).wait()
        tab_copy(0, 0).wait()
        gather_rows(0, 0)

    wgu_s[:, 0:EXPERT_FF] = wg_ref[0, 0].astype(BF16)
    wgu_s[:, EXPERT_FF:2 * EXPERT_FF] = wu_ref[0, 0].astype(BF16)
    wd_s[...] = wd_ref[0, 0].astype(BF16)

    def block(j, carry):
        g = first_ref[e] + j
        tab_copy(g + 1, (g + 1) & tmask).wait()
        tab_copy(g + TAB_AHEAD, (g + TAB_AHEAD) & tmask).start()
        ys = lax.rem(g, nfl)
        wait_rows(ys)

        send_rows((g - 1) & tmask, lax.rem(g + nfl - 1, nfl))
        gather_rows((g + 1) & tmask, (g + 1) & 1)

        acc = None
        for jb, (lo, hi) in enumerate(_load_packed_slabs(gath.at[g & 1], MOE_BLOCK)):
            a = jnp.concatenate([lo, hi], axis=1).astype(BF16)
            t = _dot(a, wgu_s[jb * 2 * LANES:(jb + 1) * 2 * LANES, :])
            acc = t if acc is None else acc + t
        gt = acc[:, 0:EXPERT_FF]
        act = (gt * jax.nn.sigmoid(gt)) * acc[:, EXPERT_FF:2 * EXPERT_FF]
        y = _dot(act.astype(BF16), wd_s[...])
        _store_packed_slabs(ybuf.at[ys], y)
        return carry

    lax.fori_loop(0, count_ref[e], block, 0)

    @pl.when(e == pl.num_programs(0) - 1)
    def _():
        send_rows((total - 1) & tmask, lax.rem(total + nfl - 1, nfl))
        for s in range(nfl):
            wait_rows(s)
        for r in range(1, TAB_AHEAD):
            tab_copy(0, (total + r) & tmask).wait()


def _moe_call(first_blk, n_blk, total, table, xp4, wg, wu, wd, layer, n_tok):
    t_pad = n_tok + PAD_ROWS
    lead_row = table.shape[0] - 1
    wspec = lambda r, c: pl.BlockSpec((1, 1, r, c), lambda e, fb, nb, tt: (layer, e, 0, 0))
    grid_spec = pltpu.PrefetchScalarGridSpec(
        num_scalar_prefetch=3,
        grid=(N_EXPERTS,),
        in_specs=[
            pl.BlockSpec(memory_space=pl.ANY),
            pl.BlockSpec(memory_space=pltpu.VMEM),
            wspec(D_MODEL, EXPERT_FF), wspec(D_MODEL, EXPERT_FF), wspec(EXPERT_FF, D_MODEL),
        ],
        out_specs=pl.BlockSpec(memory_space=pl.ANY),
        scratch_shapes=[
            pltpu.SMEM((TAB_SLOTS, 2, MOE_BLOCK), jnp.int32),
            pltpu.VMEM((2, MOE_BLOCK * PSLAB, LANES), U32),
            pltpu.VMEM((D_MODEL, 2 * EXPERT_FF), BF16),
            pltpu.VMEM((EXPERT_FF, D_MODEL), BF16),
            pltpu.VMEM((BLOCKS_IN_FLIGHT, MOE_BLOCK * PSLAB, LANES), U32),
            pltpu.SemaphoreType.DMA((BLOCKS_IN_FLIGHT,)),
            pltpu.SemaphoreType.DMA((TAB_SLOTS,)),
        ],
    )
    return pl.pallas_call(
        functools.partial(_moe_kernel, t_pad=t_pad, n_tok=n_tok, lead_row=lead_row),
        out_shape=jax.ShapeDtypeStruct((TOP_K * t_pad * PSLAB, LANES), U32),
        grid_spec=grid_spec,
        compiler_params=_cparams(("arbitrary",)),
        name="routed_experts",
    )(first_blk, n_blk, total, table, xp4, wg, wu, wd)


def _combine_kernel(h1_ref, mod_ref, y8_ref, rw_ref, sg_ref, su_ref, sd_ref, g2_ref, b2_ref, o_ref,
                    *, alpha):
    h1 = h1_ref[...]
    m = mod_ref[0]
    u2 = (_layer_norm_rows(h1) * (1.0 + m[4:5, :]) + m[3:4, :]).astype(BF16)
    g = _dot(u2, sg_ref[...])
    act = (g * jax.nn.sigmoid(g)) * _dot(u2, su_ref[...])
    f = _dot(act.astype(BF16), sd_ref[...])
    tm = h1.shape[0]
    rw = rw_ref[...]
    wk = [jnp.broadcast_to(rw[:, kk:kk + 1], (tm, LANES)) for kk in range(TOP_K)]
    cols = [None] * (2 * PSLAB)
    for kk in range(TOP_K):
        for jb, pair in enumerate(_load_packed_slabs(y8_ref.at[kk], tm)):
            for half in range(2):
                t = wk[kk] * pair[half]
                c = 2 * jb + half
                cols[c] = t if cols[c] is None else cols[c] + t
    f = f + jnp.concatenate(cols, axis=1)
    o_ref[...] = _layer_norm_rows(alpha * h1 + m[5:6, :] * f) * g2_ref[...] + b2_ref[...]


def _combine_call(h1, mod, y8, rw, wts, rows_per_group, group_is_ctx_first, nb, alpha):
    rows = h1.shape[0]
    tm = COMBINE_TILE
    tpg = rows_per_group // tm
    if group_is_ctx_first:
        first = CTX_LEN // tm
        modi = lambda i: jnp.where(i % tpg < first, nb, i // tpg)
    else:
        modi = lambda i: i // tpg

    def full(a):
        return pl.BlockSpec(a.shape, lambda i: (0,) * a.ndim)

    return pl.pallas_call(
        functools.partial(_combine_kernel, alpha=alpha),
        out_shape=jax.ShapeDtypeStruct((rows, D_MODEL), F32),
        grid=(rows // tm,),
        in_specs=[pl.BlockSpec((tm, D_MODEL), lambda i: (i, 0)),
                  pl.BlockSpec((1, 8, D_MODEL), lambda i: (modi(i), 0, 0)),
                  pl.BlockSpec((TOP_K, tm * PSLAB, LANES), lambda i: (0, i, 0)),
                  pl.BlockSpec((tm, TOP_K), lambda i: (i, 0))]
                 + [full(w) for w in wts],
        out_specs=pl.BlockSpec((tm, D_MODEL), lambda i: (i, 0)),
        compiler_params=_cparams(("parallel",)),
        name="moe_combine",
    )(h1, mod, y8, rw, *wts)


_BIG_LANE = 1 << 30


def _route_kernel(lg_ref, b_ref, idx_o, w_o, rank_o, cnt_o, carry):
    i = pl.program_id(0)

    @pl.when(i == 0)
    def _():
        carry[...] = jnp.zeros_like(carry)

    tm = lg_ref.shape[1]
    gsize = N_EXPERTS // N_GROUPS
    scores = jax.nn.sigmoid(lg_ref[...])
    biased = scores + b_ref[...]
    eid = lax.broadcasted_iota(jnp.int32, (N_EXPERTS, tm), 0)

    b3 = biased.reshape(N_GROUPS, gsize, tm)
    in_g = lax.broadcasted_iota(jnp.int32, (N_GROUPS, gsize, tm), 1)
    m1 = jnp.max(b3, axis=1, keepdims=True)
    first = jnp.min(jnp.where(b3 == m1, in_g, _BIG_LANE), axis=1, keepdims=True)
    m2 = jnp.max(jnp.where(in_g == first, -jnp.inf, b3), axis=1, keepdims=True)
    gscore = (m1 + m2).reshape(N_GROUPS, tm)

    gid = lax.broadcasted_iota(jnp.int32, (N_GROUPS, tm), 0)
    beaten = jnp.zeros((N_GROUPS, tm), jnp.int32)
    for g in range(N_GROUPS):
        sg = gscore[g:g + 1, :]
        ahead = jnp.logical_or(sg > gscore, jnp.logical_and(sg == gscore, g < gid))
        beaten = beaten + ahead.astype(jnp.int32)
    keep = jnp.broadcast_to((beaten < TOPK_GROUPS).astype(jnp.int32).reshape(N_GROUPS, 1, tm),
                            (N_GROUPS, gsize, tm)).reshape(N_EXPERTS, tm)
    masked = jnp.where(keep > 0, biased, -jnp.inf)

    idxs, ws, hots = [], [], []
    for _ in range(TOP_K):
        m = jnp.max(masked, axis=0, keepdims=True)
        ix = jnp.min(jnp.where(masked == m, eid, _BIG_LANE), axis=0, keepdims=True)
        hot = eid == ix
        idxs.append(ix)
        ws.append(jnp.sum(jnp.where(hot, scores, 0.0), axis=0, keepdims=True))
        hots.append(hot)
        masked = jnp.where(hot, -jnp.inf, masked)
    wsum = ws[0]
    for r in range(1, TOP_K):
        wsum = wsum + ws[r]
    idx_o[...] = jnp.concatenate(idxs, axis=0)
    w_rows = jnp.concatenate([wr / wsum * ROUTED_SCALE for wr in ws], axis=0)

    eye = (lax.broadcasted_iota(jnp.int32, (tm, tm), 0)
           == lax.broadcasted_iota(jnp.int32, (tm, tm), 1)).astype(BF16)
    w_cols = jnp.zeros((tm, TOP_K), F32)
    rest = w_rows
    for _ in range(3):
        part = rest.astype(BF16)
        rest = rest - part.astype(F32)
        w_cols = w_cols + _dot_nt(eye, part)
    w_o[...] = w_cols

    sel = jnp.zeros((N_EXPERTS, tm), F32)
    for hot in hots:
        sel = sel + hot.astype(F32)
    sel = sel.astype(BF16)
    earlier = (lax.broadcasted_iota(jnp.int32, (tm, tm), 0)
               < lax.broadcasted_iota(jnp.int32, (tm, tm), 1)).astype(BF16)
    prefix = _dot(sel, earlier) + carry[:, 0:1]
    rank_o[...] = jnp.concatenate(
        [jnp.sum(jnp.where(hot, prefix, 0.0), axis=0, keepdims=True) for hot in hots],
        axis=0).astype(jnp.int32)
    carry[...] = carry[...] + _dot(sel, jnp.ones((tm, LANES), BF16))
    cnt_o[...] = carry[...]


def _route_call(logits_t, b_r):
    t = logits_t.shape[1]
    tm = ROW_TILE
    kt = pl.BlockSpec((TOP_K, tm), lambda i: (0, i))
    return pl.pallas_call(
        _route_kernel,
        out_shape=[jax.ShapeDtypeStruct((TOP_K, t), jnp.int32),
                   jax.ShapeDtypeStruct((t, TOP_K), F32),
                   jax.ShapeDtypeStruct((TOP_K, t), jnp.int32),
                   jax.ShapeDtypeStruct((N_EXPERTS, LANES), F32)],
        grid=(t // tm,),
        in_specs=[pl.BlockSpec((N_EXPERTS, tm), lambda i: (0, i)),
                  pl.BlockSpec((N_EXPERTS, 1), lambda i: (0, 0))],
        out_specs=[kt, pl.BlockSpec((tm, TOP_K), lambda i: (i, 0)), kt,
                   pl.BlockSpec((N_EXPERTS, LANES), lambda i: (0, 0))],
        scratch_shapes=[pltpu.VMEM((N_EXPERTS, LANES), F32)],
        compiler_params=_cparams(("arbitrary",)),
        name="route_topk",
    )(logits_t, b_r.astype(F32).reshape(N_EXPERTS, 1))


def _dest_kernel(idx_ref, rank_ref, start_ref, o_ref):
    tm = idx_ref.shape[1]
    eid = lax.broadcasted_iota(jnp.int32, (N_EXPERTS, tm), 0)
    idx = idx_ref[...]
    start = start_ref[...]
    rows = [jnp.sum(jnp.where(eid == idx[r:r + 1, :], start, 0), axis=0, keepdims=True)
            for r in range(TOP_K)]
    o_ref[...] = jnp.concatenate(rows, axis=0) + rank_ref[...]


def _dest_call(idx, rank, pad_start):
    t = idx.shape[1]
    tm = ROW_TILE
    blk = pl.BlockSpec((TOP_K, tm), lambda i: (0, i))
    return pl.pallas_call(
        _dest_kernel,
        out_shape=jax.ShapeDtypeStruct((TOP_K, t), jnp.int32),
        grid=(t // tm,),
        in_specs=[blk, blk, pl.BlockSpec((N_EXPERTS, 1), lambda i: (0, 0))],
        out_specs=blk,
        compiler_params=_cparams(("parallel",)),
        name="slot_of_assignment",
    )(idx, rank, pad_start.reshape(N_EXPERTS, 1))


def _dispatch(idx, rank, counts, n_tok):
    n_assign = n_tok * TOP_K
    used_max = (n_assign + N_EXPERTS * (MOE_BLOCK - 1) + MOE_BLOCK - 1) // MOE_BLOCK
    nblk = -(-(used_max + TAB_AHEAD + 1) // BLOCKS_IN_FLIGHT) * BLOCKS_IN_FLIGHT
    n_slots = nblk * MOE_BLOCK
    counts = counts[:, 0].astype(jnp.int32)
    padded = (counts + MOE_BLOCK - 1) // MOE_BLOCK * MOE_BLOCK
    pad_end = jnp.cumsum(padded)
    pad_start = pad_end - padded
    dest = _dest_call(idx, rank, pad_start)
    assign = (jnp.arange(n_tok, dtype=jnp.int32)[None, :] * TOP_K
              + jnp.arange(TOP_K, dtype=jnp.int32)[:, None])
    assert (nblk - 1) % BLOCKS_IN_FLIGHT == BLOCKS_IN_FLIGHT - 1
    pad_a = n_assign + jnp.arange(n_slots, dtype=jnp.int32) % (BLOCKS_IN_FLIGHT * MOE_BLOCK)
    slot_a = pad_a.at[dest.reshape(-1)].set(assign.reshape(-1), unique_indices=True,
                                            mode='promise_in_bounds')
    t_pad = n_tok + PAD_ROWS
    tok = lax.shift_right_logical(slot_a, K_SHIFT)
    tok4 = jnp.minimum(tok, n_tok - 1) * PSLAB
    dst4 = ((slot_a & (TOP_K - 1)) * t_pad + tok) * PSLAB
    table = jnp.stack([tok4.reshape(nblk, MOE_BLOCK), dst4.reshape(nblk, MOE_BLOCK)], axis=1)
    total = (pad_end[-1] // MOE_BLOCK).astype(jnp.int32).reshape(1)
    return pad_start // MOE_BLOCK, padded // MOE_BLOCK, total, table


def _rope_tables(s, ntok):
    rows_n = s // GRID_W
    row = jnp.repeat(jnp.arange(rows_n, dtype=F32), GRID_W)
    col = jnp.tile(jnp.arange(GRID_W, dtype=F32), rows_n)
    axis_dim = HEAD_DIM // 2
    inv = jnp.power(ROPE_THETA, -jnp.arange(0, axis_dim, 2, dtype=F32) / axis_dim)
    ar = row[:, None] * inv[None]
    ac = col[:, None] * inv[None]
    ang = jnp.concatenate([ar, ar, ac, ac], -1)
    cos, sin = jnp.cos(ang), jnp.sin(ang)
    quarter = (jnp.arange(HEAD_DIM) // 16) % 2
    s_up = jnp.where(quarter == 0, -sin, 0.0)
    s_dn = jnp.where(quarter == 1, sin, 0.0)
    nctx = ntok - s

    def expand(t, ctx_val):
        t = jnp.concatenate([jnp.full((nctx, HEAD_DIM), ctx_val, F32), t], axis=0)
        return jnp.tile(t, (1, LANES // HEAD_DIM))

    return expand(cos, 1.0), expand(s_up, 0.0), expand(s_dn, 0.0)


def _head_mean_matrix(width):
    hid = np.arange(width) // HEAD_DIM
    return jnp.asarray((hid[:, None] == hid[None, :]).astype(np.float32) / HEAD_DIM).astype(BF16)


def _dup_heads(a, n_heads):
    parts = []
    for hd in range(n_heads):
        p = a[..., hd * HEAD_DIM:(hd + 1) * HEAD_DIM]
        parts += [p, p]
    return jnp.concatenate(parts, axis=-1)


def kernel(x, c, ctx, c_ctx, w_mod, b_mod, w_in, qn_a, kn_a, lam_q1, lam_k1, lam_q2, lam_k2, subln_c, w_br_a, w_br_b, w_br_c, w_out, ln1_g, ln1_b, w_router, b_router, w_sh_gate, w_sh_up, w_sh_down, w_e_gate, w_e_up, w_e_down, ln2_g, ln2_b):
    nb, s, d = x.shape
    lc = ctx.shape[1]
    depth = w_mod.shape[0]
    assert d == D_MODEL and lc == CTX_LEN and s % ROW_TILE == 0 and s % GRID_W == 0
    ntok = lc + s
    alpha = (2 * depth) ** 0.25

    tabs = _rope_tables(s, ntok)
    pos_lat = _dft_tables(s)
    pos_ctx = _dft_tables(lc)
    chan = _channel_table()
    e_q = _head_mean_matrix(A_Q_W)
    e_k = _head_mean_matrix(2 * A_KV_W)

    cc = jnp.concatenate([c, c_ctx[None, :]], axis=0)
    cc = jnp.pad(cc, ((0, (-(nb + 1)) % 8), (0, 0)))
    h_all = jnp.concatenate([ctx, x], axis=1).reshape(nb * ntok, d)

    offs = np.cumsum([0, A_Q_W, A_KV_W, A_KV_W, B_W, C_QK_W, C_QK_W, C_V_W, GATE_W])
    out = None
    for l in range(depth):
        last = l == depth - 1
        mod = _mod_call(cc, w_mod[l], b_mod[l])[:nb + 1].reshape(nb + 1, 6, d)
        mod = jnp.pad(mod, ((0, 0), (0, 2), (0, 0)))
        lam_init = 0.8 - 0.6 * math.exp(-0.3 * l)
        lam = (jnp.exp(jnp.sum(lam_q1[l].astype(F32) * lam_k1[l].astype(F32)))
               - jnp.exp(jnp.sum(lam_q2[l].astype(F32) * lam_k2[l].astype(F32)))) + lam_init
        lam = lam.reshape(1, 1).astype(F32)

        wl = w_in[l]
        seg = [wl[:, offs[i]:offs[i + 1]] for i in range(8)]
        in_wts = (seg[0].astype(BF16), _dup_heads(seg[1], A_KV_HEADS).astype(BF16),
                  _dup_heads(seg[2], A_KV_HEADS).astype(BF16), seg[3].astype(BF16),
                  seg[4].astype(BF16), seg[5].astype(BF16), seg[6].astype(BF16),
                  jnp.tile(qn_a[l].astype(F32), A_Q_HEADS).reshape(1, A_Q_W),
                  jnp.tile(kn_a[l].astype(F32), 2 * A_KV_HEADS).reshape(1, 2 * A_KV_W),
                  e_q, e_k)
        qa, ka, va, fb, qc, kc, vc = _inproj_call(h_all, mod, tabs, in_wts, nb, ntok)

        first_tile = 1 if last else 0
        r3 = lambda a: a.reshape(nb, ntok, a.shape[-1])
        subln = subln_c[l].astype(F32).reshape(1, C_V_DIM)
        attn_a = functools.partial(_attn_a_call, r3(qa), ka, r3(va), nb, ntok)
        attn_c = functools.partial(_attn_c_call, lam, r3(qc), kc, r3(vc), subln, nb, ntok)
        oa, oc = attn_a(False), attn_c(False, 1.0 - lam_init)
        oa_ctx, oc_ctx = (oa, oc) if last else (attn_a(True), attn_c(True, 1.0 - lam_init))
        ob = _fourier_call(r3(fb), chan, pos_lat, pos_ctx, nb, ntok, first_tile)

        mix_wts = (seg[7].astype(BF16), w_br_a[l].astype(BF16), w_br_b[l].astype(BF16),
                   w_br_c[l].astype(BF16), w_out[l].astype(BF16),
                   ln1_g[l].astype(F32).reshape(1, d), ln1_b[l].astype(F32).reshape(1, d),
                   w_router[l].T.astype(BF16))
        flat = lambda a: a.reshape(-1, a.shape[-1])
        h1, xp, logits = _mixout_call(h_all, mod, flat(oa), flat(oa_ctx), flat(ob), flat(oc),
                                      flat(oc_ctx), mix_wts, nb, ntok, last, alpha)

        n_tok = h1.shape[0]
        idx, rw, rank, counts = _route_call(logits, b_router[l])
        first_blk, n_blk, total, table = _dispatch(idx, rank, counts, n_tok)
        y8 = _moe_call(first_blk, n_blk, total, table, xp,
                       w_e_gate, w_e_up, w_e_down, l, n_tok)
        y8 = y8.reshape(TOP_K, (n_tok + PAD_ROWS) * PSLAB, LANES)

        comb_wts = (w_sh_gate[l].astype(BF16), w_sh_up[l].astype(BF16), w_sh_down[l].astype(BF16),
                    ln2_g[l].astype(F32).reshape(1, d), ln2_b[l].astype(F32).reshape(1, d))
        h2 = _combine_call(h1, mod, y8, rw, comb_wts, s if last else ntok, not last, nb, alpha)
        if last:
            out = h2.reshape(nb, s, d)
        else:
            h_all = h2
    return out
```

```python
import functools
import math

import numpy as np
import jax
import jax.numpy as jnp
from jax import lax
from jax.experimental import pallas as pl
from jax.experimental.pallas import tpu as pltpu

F32 = jnp.float32
BF16 = jnp.bfloat16
U32 = jnp.uint32

D_MODEL = 1024
CTX_LEN = 256
GRID_W = 64
HEAD_DIM = 64
ROPE_THETA = 10000.0
A_Q_HEADS = 8
A_KV_HEADS = 2
A_Q_W = A_Q_HEADS * HEAD_DIM
A_KV_W = A_KV_HEADS * HEAD_DIM
F_GROUPS = 4
F_GROUP_W = 128
B_W = F_GROUPS * F_GROUP_W
C_HEADS = 4
C_V_DIM = 2 * HEAD_DIM
C_QK_W = C_HEADS * 2 * HEAD_DIM
C_V_W = C_HEADS * C_V_DIM
N_BRANCH = 3
GATE_W = N_BRANCH * D_MODEL
N_EXPERTS = 256
TOP_K = 8
N_GROUPS = 8
TOPK_GROUPS = 4
EXPERT_FF = 256
SHARED_FF = 256
ROUTED_SCALE = 2.5
LN_EPS = 1e-5
RMS_EPS = 1e-6
K_SHIFT = TOP_K.bit_length() - 1
assert 1 << K_SHIFT == TOP_K

LANES = 128
PSLAB = D_MODEL // (2 * LANES)
ROW_TILE = 256
ATTN_TILE = 512
MOE_BLOCK = 128
BLOCKS_IN_FLIGHT = 4
TAB_AHEAD = 4
TAB_SLOTS = 8
PAD_ROWS = BLOCKS_IN_FLIGHT * MOE_BLOCK // TOP_K
COMBINE_TILE = 256
VMEM_LIMIT = 56 * 1024 * 1024

_Q_SCALE = HEAD_DIM ** -0.5 * math.log2(math.e)


def _cparams(sem):
    return pltpu.CompilerParams(dimension_semantics=sem, vmem_limit_bytes=VMEM_LIMIT)


def _dot(a, b):
    return jnp.dot(a, b, preferred_element_type=F32)


def _dot_nt(a, b):
    return lax.dot_general(a, b, (((1,), (1,)), ((), ())), preferred_element_type=F32)


def _layer_norm_rows(x):
    mu = jnp.mean(x, axis=-1, keepdims=True)
    xc = x - mu
    var = jnp.mean(xc * xc, axis=-1, keepdims=True)
    return xc * lax.rsqrt(var + LN_EPS)


def _dot_split(x, e):
    hi = x.astype(BF16)
    lo = (x - hi.astype(F32)).astype(BF16)
    return _dot(hi, e) + _dot(lo, e)


def _mod_kernel(c_ref, w_ref, b_ref, o_ref):
    c = c_ref[...]
    sc = c * jax.nn.sigmoid(c)
    o_ref[...] = _dot(sc.astype(BF16), w_ref[...].astype(BF16)) + b_ref[...]


def _mod_call(cc, w_mod_l, b_mod_l):
    r = cc.shape[0]
    n = w_mod_l.shape[1]
    tn = D_MODEL
    return pl.pallas_call(
        _mod_kernel,
        out_shape=jax.ShapeDtypeStruct((r, n), F32),
        grid=(n // tn,),
        in_specs=[
            pl.BlockSpec((r, D_MODEL), lambda j: (0, 0)),
            pl.BlockSpec((D_MODEL, tn), lambda j: (0, j)),
            pl.BlockSpec((1, tn), lambda j: (0, j)),
        ],
        out_specs=pl.BlockSpec((r, tn), lambda j: (0, j)),
        compiler_params=_cparams(("arbitrary",)),
        name="mod_vectors",
    )(cc, w_mod_l, b_mod_l.reshape(1, n))


def _rope_cols(x, cos, sin_up, sin_dn):
    cols = []
    for c in range(x.shape[1] // LANES):
        xc = x[:, c * LANES:(c + 1) * LANES]
        up = pltpu.roll(xc, LANES - 16, axis=1)
        dn = pltpu.roll(xc, 16, axis=1)
        cols.append(xc * cos + up * sin_up + dn * sin_dn)
    return jnp.concatenate(cols, axis=1) if len(cols) > 1 else cols[0]


def _inproj_kernel(h_ref, mod_ref, cos_ref, su_ref, sd_ref,
                   wq_ref, wk_ref, wv_ref, wf_ref, wqc_ref, wkc_ref, wvc_ref,
                   qn_ref, kn_ref, eq_ref, ek_ref,
                   qa_o, ka_o, va_o, fb_o, qc_o, kc_o, vc_o):
    h = h_ref[...]
    shift = mod_ref[0, 0:1, :]
    scale = mod_ref[0, 1:2, :]
    u = (_layer_norm_rows(h) * (1.0 + scale) + shift).astype(BF16)
    cos = cos_ref[...]
    s_up = su_ref[...]
    s_dn = sd_ref[...]

    q = _dot(u, wq_ref[...])
    ms = _dot_split(q * q, eq_ref[...])
    q = q * lax.rsqrt(ms + RMS_EPS) * qn_ref[...]
    qa_o[...] = (_rope_cols(q, cos, s_up, s_dn) * _Q_SCALE).astype(BF16)

    k = _dot(u, wk_ref[...])
    ms = _dot_split(k * k, ek_ref[...])
    k = k * lax.rsqrt(ms + RMS_EPS) * kn_ref[...]
    ka_o[0] = jnp.transpose(_rope_cols(k, cos, s_up, s_dn)).astype(BF16)

    va_o[...] = _dot(u, wv_ref[...]).astype(BF16)
    fb_o[...] = _dot(u, wf_ref[...]).astype(BF16)
    qc = _dot(u, wqc_ref[...])
    qc_o[...] = (_rope_cols(qc, cos, s_up, s_dn) * _Q_SCALE).astype(BF16)
    kc = _dot(u, wkc_ref[...])
    kc_o[0] = jnp.transpose(_rope_cols(kc, cos, s_up, s_dn)).astype(BF16)
    vc_o[...] = _dot(u, wvc_ref[...]).astype(BF16)


def _inproj_call(h_all, mod, tabs, wts, nb, ntok):
    rows = h_all.shape[0]
    tpb = ntok // ROW_TILE
    cos, s_up, s_dn = tabs

    def full(a):
        return pl.BlockSpec(a.shape, lambda i: (0,) * a.ndim)

    def rowspec(w):
        return pl.BlockSpec((ROW_TILE, w), lambda i: (i, 0))

    tabspec = pl.BlockSpec((ROW_TILE, LANES), lambda i: (i % tpb, 0))
    modspec = pl.BlockSpec((1, 8, D_MODEL),
                           lambda i: (jnp.where(i % tpb == 0, nb, i // tpb), 0, 0))
    widths = (A_Q_W, 2 * A_KV_W, 2 * A_KV_W, B_W, C_QK_W, C_QK_W, C_V_W)
    transposed = (1, 5)

    def oshape(n, w):
        return (nb, w, ntok) if n in transposed else (rows, w)

    def ospec(n, w):
        if n in transposed:
            return pl.BlockSpec((1, w, ROW_TILE), lambda i: (i // tpb, 0, i % tpb))
        return rowspec(w)

    return pl.pallas_call(
        _inproj_kernel,
        out_shape=[jax.ShapeDtypeStruct(oshape(n, w), BF16) for n, w in enumerate(widths)],
        grid=(rows // ROW_TILE,),
        in_specs=[rowspec(D_MODEL), modspec, tabspec, tabspec, tabspec]
                 + [full(w) for w in wts],
        out_specs=[ospec(n, w) for n, w in enumerate(widths)],
        compiler_params=_cparams(("parallel",)),
        name="in_projection",
    )(h_all, mod, cos, s_up, s_dn, *wts)


def _softmax_parts(s):
    m = jnp.max(s, axis=-1, keepdims=True)
    e = jnp.exp2(s - m)
    return e, jnp.sum(e, axis=-1, keepdims=True)


def _query_rows(q_refs):
    rows = [r[0] for r in q_refs]
    return rows[0] if len(rows) == 1 else jnp.concatenate(rows, axis=0)


def _gqa_kernel(*refs):
    q_refs, (k_ref, v_ref, o_ref) = refs[:-3], refs[-3:]
    q = _query_rows(q_refs)
    lane = lax.broadcasted_iota(jnp.int32, (1, LANES), 1)
    low = lane < HEAD_DIM
    for c in range(A_Q_W // LANES):
        kvh = (2 * c) // (A_Q_HEADS // A_KV_HEADS)
        qc = q[:, c * LANES:(c + 1) * LANES]
        kk = k_ref[0, kvh * LANES:(kvh + 1) * LANES, :]
        vv = v_ref[0, :, kvh * LANES:(kvh + 1) * LANES]
        halves = []
        for keep in (low, jnp.logical_not(low)):
            qm = jnp.where(keep, qc, jnp.zeros_like(qc))
            e, l = _softmax_parts(_dot(qm, kk))
            halves.append(_dot(e.astype(BF16), vv) * (1.0 / l))
        o_ref[0, :, c * LANES:(c + 1) * LANES] = jnp.where(low, halves[0], halves[1]).astype(BF16)


def _diff_kernel(lam_ref, *refs, out_scale):
    q_refs, (k_ref, v_ref, g_ref, o_ref) = refs[:-4], refs[-4:]
    q = _query_rows(q_refs)
    lane = lax.broadcasted_iota(jnp.int32, (1, LANES), 1)
    low = lane < HEAD_DIM
    lam = lam_ref[0, 0]
    for hd in range(C_HEADS):
        sl = slice(hd * LANES, (hd + 1) * LANES)
        qc = q[:, sl]
        kk = k_ref[0, sl, :]
        vv = v_ref[0, :, sl]
        q1 = jnp.where(low, qc, jnp.zeros_like(qc))
        q2 = jnp.where(low, jnp.zeros_like(qc), qc)
        e1, l1 = _softmax_parts(_dot(q1, kk))
        e2, l2 = _softmax_parts(_dot(q2, kk))
        o = _dot(e1.astype(BF16), vv) * (1.0 / l1) - _dot(e2.astype(BF16), vv) * (lam / l2)
        ms = jnp.mean(o * o, axis=-1, keepdims=True)
        o = o * lax.rsqrt(ms + RMS_EPS) * g_ref[...] * out_scale
        o_ref[0, :, sl] = o.astype(BF16)


def _attn_call(kernel_fn, name, head, tail, q, kt, v, nb, ntok, wq, wkv, ctx_queries):
    cblk = CTX_LEN // ROW_TILE
    if ctx_queries:
        rows, nk, grid = CTX_LEN, CTX_LEN, (nb, 1)
        qspecs = [pl.BlockSpec((1, ROW_TILE, wq), lambda b, j: (b, 0, 0))]
    else:
        rows, nk, grid = ATTN_TILE, ntok, (nb, (ntok - CTX_LEN) // ATTN_TILE)
        per = ATTN_TILE // ROW_TILE
        qspecs = [pl.BlockSpec((1, ROW_TILE, wq), lambda b, j, r=r: (b, cblk + per * j + r, 0))
                  for r in range(per)]
    kspec = pl.BlockSpec((1, wkv, nk), lambda b, j: (b, 0, 0))
    vspec = pl.BlockSpec((1, nk, wkv), lambda b, j: (b, 0, 0))
    return pl.pallas_call(
        kernel_fn,
        out_shape=jax.ShapeDtypeStruct((nb, rows * grid[1], wq), BF16),
        grid=grid,
        in_specs=[sp for _, sp in head] + qspecs + [kspec, vspec] + [sp for _, sp in tail],
        out_specs=pl.BlockSpec((1, rows, wq), lambda b, j: (b, j, 0)),
        compiler_params=_cparams(("parallel", "arbitrary")),
        name=name,
    )(*[a for a, _ in head], *([q] * len(qspecs)), kt, v, *[a for a, _ in tail])


def _attn_a_call(qa, ka, va, nb, ntok, ctx_queries):
    return _attn_call(_gqa_kernel, "gqa_attention_ctx" if ctx_queries else "gqa_attention", [], [],
                      qa, ka, va, nb, ntok, A_Q_W, 2 * A_KV_W, ctx_queries)


def _attn_c_call(lam, qc, kc, vc, subln, nb, ntok, ctx_queries, out_scale):
    head = [(lam, pl.BlockSpec(memory_space=pltpu.SMEM))]
    tail = [(subln, pl.BlockSpec((1, C_V_DIM), lambda b, j: (0, 0)))]
    return _attn_call(functools.partial(_diff_kernel, out_scale=out_scale),
                      "diff_attention_ctx" if ctx_queries else "diff_attention", head, tail,
                      qc, kc, vc, nb, ntok, C_QK_W, C_QK_W, ctx_queries)


def _fourier_kernel(f_ref, chan_ref, pos_ref, posc_ref, o_ref, g_ref, *, ntok, first_tile):
    j = pl.program_id(1) + first_tile
    nlat = ntok - CTX_LEN

    def channel_stage(rows):
        g = _dot(rows, chan_ref[...])
        return jnp.concatenate([g[:, :B_W], g[:, B_W:]], axis=0).astype(BF16)

    if first_tile == 0:
        @pl.when(j == 0)
        def _():
            gc = channel_stage(f_ref[0, 0:CTX_LEN, :])
            y = _dot(posc_ref[...], gc) * (1.0 / math.sqrt(CTX_LEN * F_GROUP_W))
            o_ref[0] = y.astype(BF16)

    @pl.when(j == 1)
    def _():
        g_ref[...] = channel_stage(f_ref[0, CTX_LEN:ntok, :])

    @pl.when(j >= 1)
    def _():
        y = _dot(pos_ref[...], g_ref[...]) * (1.0 / math.sqrt(nlat * F_GROUP_W))
        o_ref[0] = y.astype(BF16)


def _fourier_call(fb, chan, pos, posc, nb, ntok, first_tile):
    tpb = ntok // ROW_TILE
    nlat = ntok - CTX_LEN
    return pl.pallas_call(
        functools.partial(_fourier_kernel, ntok=ntok, first_tile=first_tile),
        out_shape=jax.ShapeDtypeStruct((nb, ntok - first_tile * ROW_TILE, B_W), BF16),
        grid=(nb, tpb - first_tile),
        in_specs=[
            pl.BlockSpec((1, ntok, B_W), lambda b, j: (b, 0, 0)),
            pl.BlockSpec(chan.shape, lambda b, j: (0, 0)),
            pl.BlockSpec((ROW_TILE, 2 * nlat),
                         lambda b, j: (jnp.maximum(j + first_tile - 1, 0), 0)),
            pl.BlockSpec(posc.shape, lambda b, j: (0, 0)),
        ],
        out_specs=pl.BlockSpec((1, ROW_TILE, B_W), lambda b, j: (b, j, 0)),
        scratch_shapes=[pltpu.VMEM((2 * nlat, B_W), BF16)],
        compiler_params=_cparams(("parallel", "arbitrary")),
        name="fourier_mix",
    )(fb, chan, pos, posc)


def _dft_tables(n):
    n1 = 32
    n0 = n // n1
    k = np.arange(n, dtype=np.int64)
    a = 2.0 * np.pi * ((k[:, None] * np.arange(n1)[None, :] * n0) % n) / n
    b = 2.0 * np.pi * ((k[:, None] * np.arange(n0)[None, :]) % n) / n
    ca, sa = jnp.asarray(np.cos(a), F32)[:, :, None], jnp.asarray(np.sin(a), F32)[:, :, None]
    cb, sb = jnp.asarray(np.cos(b), F32)[:, None, :], jnp.asarray(np.sin(b), F32)[:, None, :]
    cos = (ca * cb - sa * sb).reshape(n, n)
    sin = (sa * cb + ca * sb).reshape(n, n)
    return jnp.concatenate([cos, -sin], axis=1).astype(BF16)


def _channel_table():
    c = np.arange(F_GROUP_W)
    ang = 2.0 * np.pi * ((c[:, None] * c[None, :]) % F_GROUP_W) / F_GROUP_W
    eye = np.eye(F_GROUPS)
    cos = np.kron(eye, np.cos(ang))
    sin = np.kron(eye, np.sin(ang))
    return jnp.asarray(np.concatenate([cos, sin], axis=1), F32).astype(BF16)


def _store_packed_slabs(dst_ref, u):
    rows = u.shape[0]
    for jb in range(PSLAB):
        lo = u[:, 2 * jb * LANES:(2 * jb + 1) * LANES]
        hi = u[:, (2 * jb + 1) * LANES:(2 * jb + 2) * LANES]
        dst_ref[pl.ds(jb, rows, stride=PSLAB), :] = pltpu.pack_elementwise([lo, hi], packed_dtype=BF16)


def _load_packed_slabs(src_ref, rows):
    out = []
    for jb in range(PSLAB):
        words = src_ref[pl.ds(jb, rows, stride=PSLAB), :]
        out.append((pltpu.unpack_elementwise(words, index=0, packed_dtype=BF16, unpacked_dtype=F32),
                    pltpu.unpack_elementwise(words, index=1, packed_dtype=BF16, unpacked_dtype=F32)))
    return out


def _mixout_kernel(h_ref, mod_ref, oa_ref, oa_ctx_ref, ob_ref, oc_ref, oc_ctx_ref,
                   wg_ref, wa_ref, wb_ref, wc_ref, wo_ref, g1_ref, b1_ref, wr_ref,
                   h1_o, xp_o, lg_o, *, alpha, ctx_every):
    h = h_ref[...]
    m = mod_ref[0]
    u = (_layer_norm_rows(h) * (1.0 + m[1:2, :]) + m[0:1, :]).astype(BF16)
    oa, oc = oa_ref[...], oc_ref[...]
    if ctx_every:
        is_ctx = pl.program_id(0) % ctx_every == 0
        oa = jnp.where(is_ctx, oa_ctx_ref[...], oa)
        oc = jnp.where(is_ctx, oc_ctx_ref[...], oc)
    y = None
    for n, (o, w_ref) in enumerate(((oa, wa_ref), (ob_ref[...], wb_ref), (oc, wc_ref))):
        gate = jax.nn.sigmoid(_dot(u, wg_ref[:, n * D_MODEL:(n + 1) * D_MODEL]))
        t = gate * _dot(o, w_ref[...])
        y = t if y is None else y + t
    z = _dot(y.astype(BF16), wo_ref[...])
    h1 = _layer_norm_rows(alpha * h + m[2:3, :] * z) * g1_ref[...] + b1_ref[...]
    h1_o[...] = h1
    u2 = _layer_norm_rows(h1) * (1.0 + m[4:5, :]) + m[3:4, :]
    _store_packed_slabs(xp_o, u2)
    lg_o[...] = _dot_nt(wr_ref[...], u2.astype(BF16))


def _tile_maps(nb, ntok, lat_only):
    tpb = ntok // ROW_TILE
    if lat_only:
        lpb = tpb - 1
        n_tiles = nb * lpb
        src = lambda i: (i // lpb) * tpb + 1 + i % lpb
        modi = lambda i: i // lpb
    else:
        n_tiles = nb * tpb
        src = lambda i: i
        modi = lambda i: jnp.where(i % tpb == 0, nb, i // tpb)
    return n_tiles, src, modi


def _mixout_call(h_all, mod, oa, oa_ctx, ob, oc, oc_ctx, wts, nb, ntok, lat_only, alpha):
    n_tiles, src, modi = _tile_maps(nb, ntok, lat_only)
    rows_out = n_tiles * ROW_TILE
    tpb = ntok // ROW_TILE
    lpb = tpb - CTX_LEN // ROW_TILE
    if lat_only:
        lat_blk = lambda i: i
        ctx_blk = lambda i: 0
    else:
        lat_blk = lambda i: (i // tpb) * lpb + jnp.maximum(i % tpb - 1, 0)
        ctx_blk = lambda i: i // tpb

    def full(a):
        return pl.BlockSpec(a.shape, lambda i: (0,) * a.ndim)

    def inrow(w):
        return pl.BlockSpec((ROW_TILE, w), lambda i: (src(i), 0))

    def outrow(w):
        return pl.BlockSpec((ROW_TILE, w), lambda i: (i, 0))

    modspec = pl.BlockSpec((1, 8, D_MODEL), lambda i: (modi(i), 0, 0))
    def latrow(w):
        return pl.BlockSpec((ROW_TILE, w), lambda i: (lat_blk(i), 0))

    def ctxrow(w):
        return pl.BlockSpec((ROW_TILE, w), lambda i: (ctx_blk(i), 0))

    return pl.pallas_call(
        functools.partial(_mixout_kernel, alpha=alpha, ctx_every=0 if lat_only else tpb),
        out_shape=[jax.ShapeDtypeStruct((rows_out, D_MODEL), F32),
                   jax.ShapeDtypeStruct((rows_out * PSLAB, LANES), U32),
                   jax.ShapeDtypeStruct((N_EXPERTS, rows_out), F32)],
        grid=(n_tiles,),
        in_specs=[inrow(D_MODEL), modspec, latrow(A_Q_W), ctxrow(A_Q_W), outrow(B_W),
                  latrow(C_V_W), ctxrow(C_V_W)]
                 + [full(w) for w in wts],
        out_specs=[outrow(D_MODEL), pl.BlockSpec((ROW_TILE * PSLAB, LANES), lambda i: (i, 0)),
                   pl.BlockSpec((N_EXPERTS, ROW_TILE), lambda i: (0, i))],
        compiler_params=_cparams(("parallel",)),
        name="mixer_output",
    )(h_all, mod, oa, oa_ctx, ob, oc, oc_ctx, *wts)


def _moe_kernel(first_ref, count_ref, total_ref,
                tab_hbm, xp_ref, wg_ref, wu_ref, wd_ref,
                out_hbm,
                tab, gath, wgu_s, wd_s, ybuf, sem, tsem, *, t_pad, n_tok, lead_row):
    nfl = BLOCKS_IN_FLIGHT
    tmask = TAB_SLOTS - 1
    e = pl.program_id(0)
    total = total_ref[0]

    def row_copy(s, m, dst4):
        return pltpu.make_async_copy(ybuf.at[s, pl.ds(PSLAB * m, PSLAB), :],
                                     out_hbm.at[pl.ds(pl.multiple_of(dst4, PSLAB), PSLAB), :],
                                     sem.at[s])

    def wait_rows(s):
        for m in range(MOE_BLOCK):
            row_copy(s, m, 0).wait()

    def send_rows(ts, ys):
        for m in range(MOE_BLOCK):
            row_copy(ys, m, tab[ts, 1, m]).start(priority=m % 2)

    def tab_copy(row, s):
        return pltpu.make_async_copy(tab_hbm.at[row], tab.at[s], tsem.at[s])

    def gather_rows(ts, gs):
        for m in range(MOE_BLOCK):
            t4 = pl.multiple_of(tab[ts, 0, m], PSLAB)
            gath[gs, PSLAB * m:PSLAB * (m + 1), :] = xp_ref[pl.ds(t4, PSLAB), :]

    @pl.when(e == 0)
    def _():
        ybuf[...] = jnp.zeros_like(ybuf)
        tab_copy(lead_row, tmask).start()
        for r in range(TAB_AHEAD):
            tab_copy(r, r).start()
        for b in range(nfl - 1):
            for m in range(MOE_BLOCK):
                q = b * MOE_BLOCK + m
                row_copy(b, m, ((q % TOP_K) * t_pad + n_tok + q // TOP_K) * PSLAB).start(priority=m % 2)
        tab_copy(lead_row, tmask).wait()
        tab_copy(0, 0).wait()
        gather_rows(0, 0)

    wgu_s[:, 0:EXPERT_FF] = wg_ref[0, 0].astype(BF16)
    wgu_s[:, EXPERT_FF:2 * EXPERT_FF] = wu_ref[0, 0].astype(BF16)
    wd_s[...] = wd_ref[0, 0].astype(BF16)

    def block(j, carry):
        g = first_ref[e] + j
        tab_copy(g + 1, (g + 1) & tmask).wait()
        tab_copy(g + TAB_AHEAD, (g + TAB_AHEAD) & tmask).start()
        ys = lax.rem(g, nfl)
        wait_rows(ys)

        send_rows((g - 1) & tmask, lax.rem(g + nfl - 1, nfl))
        gather_rows((g + 1) & tmask, (g + 1) & 1)

        acc = None
        for jb, (lo, hi) in enumerate(_load_packed_slabs(gath.at[g & 1], MOE_BLOCK)):
            a = jnp.concatenate([lo, hi], axis=1).astype(BF16)
            t = _dot(a, wgu_s[jb * 2 * LANES:(jb + 1) * 2 * LANES, :])
            acc = t if acc is None else acc + t
        gt = acc[:, 0:EXPERT_FF]
        act = (gt * jax.nn.sigmoid(gt)) * acc[:, EXPERT_FF:2 * EXPERT_FF]
        y = _dot(act.astype(BF16), wd_s[...])
        _store_packed_slabs(ybuf.at[ys], y)
        return carry

    lax.fori_loop(0, count_ref[e], block, 0)

    @pl.when(e == pl.num_programs(0) - 1)
    def _():
        send_rows((total - 1) & tmask, lax.rem(total + nfl - 1, nfl))
        for s in range(nfl):
            wait_rows(s)
        for r in range(1, TAB_AHEAD):
            tab_copy(0, (total + r) & tmask).wait()


def _moe_call(first_blk, n_blk, total, table, xp4, wg, wu, wd, layer, n_tok):
    t_pad = n_tok + PAD_ROWS
    lead_row = table.shape[0] - 1
    wspec = lambda r, c: pl.BlockSpec((1, 1, r, c), lambda e, fb, nb, tt: (layer, e, 0, 0))
    grid_spec = pltpu.PrefetchScalarGridSpec(
        num_scalar_prefetch=3,
        grid=(N_EXPERTS,),
        in_specs=[
            pl.BlockSpec(memory_space=pl.ANY),
            pl.BlockSpec(memory_space=pltpu.VMEM),
            wspec(D_MODEL, EXPERT_FF), wspec(D_MODEL, EXPERT_FF), wspec(EXPERT_FF, D_MODEL),
        ],
        out_specs=pl.BlockSpec(memory_space=pl.ANY),
        scratch_shapes=[
            pltpu.SMEM((TAB_SLOTS, 2, MOE_BLOCK), jnp.int32),
            pltpu.VMEM((2, MOE_BLOCK * PSLAB, LANES), U32),
            pltpu.VMEM((D_MODEL, 2 * EXPERT_FF), BF16),
            pltpu.VMEM((EXPERT_FF, D_MODEL), BF16),
            pltpu.VMEM((BLOCKS_IN_FLIGHT, MOE_BLOCK * PSLAB, LANES), U32),
            pltpu.SemaphoreType.DMA((BLOCKS_IN_FLIGHT,)),
            pltpu.SemaphoreType.DMA((TAB_SLOTS,)),
        ],
    )
    return pl.pallas_call(
        functools.partial(_moe_kernel, t_pad=t_pad, n_tok=n_tok, lead_row=lead_row),
        out_shape=jax.ShapeDtypeStruct((TOP_K * t_pad * PSLAB, LANES), U32),
        grid_spec=grid_spec,
        compiler_params=_cparams(("arbitrary",)),
        name="routed_experts",
    )(first_blk, n_blk, total, table, xp4, wg, wu, wd)


def _combine_kernel(h1_ref, mod_ref, y8_ref, rw_ref, sg_ref, su_ref, sd_ref, g2_ref, b2_ref, o_ref,
                    *, alpha):
    h1 = h1_ref[...]
    m = mod_ref[0]
    u2 = (_layer_norm_rows(h1) * (1.0 + m[4:5, :]) + m[3:4, :]).astype(BF16)
    g = _dot(u2, sg_ref[...])
    act = (g * jax.nn.sigmoid(g)) * _dot(u2, su_ref[...])
    f = _dot(act.astype(BF16), sd_ref[...])
    tm = h1.shape[0]
    rw = rw_ref[...]
    wk = [jnp.broadcast_to(rw[:, kk:kk + 1], (tm, LANES)) for kk in range(TOP_K)]
    cols = [None] * (2 * PSLAB)
    for kk in range(TOP_K):
        for jb, pair in enumerate(_load_packed_slabs(y8_ref.at[kk], tm)):
            for half in range(2):
                t = wk[kk] * pair[half]
                c = 2 * jb + half
                cols[c] = t if cols[c] is None else cols[c] + t
    f = f + jnp.concatenate(cols, axis=1)
    o_ref[...] = _layer_norm_rows(alpha * h1 + m[5:6, :] * f) * g2_ref[...] + b2_ref[...]


def _combine_call(h1, mod, y8, rw, wts, rows_per_group, group_is_ctx_first, nb, alpha):
    rows = h1.shape[0]
    tm = COMBINE_TILE
    tpg = rows_per_group // tm
    if group_is_ctx_first:
        first = CTX_LEN // tm
        modi = lambda i: jnp.where(i % tpg < first, nb, i // tpg)
    else:
        modi = lambda i: i // tpg

    def full(a):
        return pl.BlockSpec(a.shape, lambda i: (0,) * a.ndim)

    return pl.pallas_call(
        functools.partial(_combine_kernel, alpha=alpha),
        out_shape=jax.ShapeDtypeStruct((rows, D_MODEL), F32),
        grid=(rows // tm,),
        in_specs=[pl.BlockSpec((tm, D_MODEL), lambda i: (i, 0)),
                  pl.BlockSpec((1, 8, D_MODEL), lambda i: (modi(i), 0, 0)),
                  pl.BlockSpec((TOP_K, tm * PSLAB, LANES), lambda i: (0, i, 0)),
                  pl.BlockSpec((tm, TOP_K), lambda i: (i, 0))]
                 + [full(w) for w in wts],
        out_specs=pl.BlockSpec((tm, D_MODEL), lambda i: (i, 0)),
        compiler_params=_cparams(("parallel",)),
        name="moe_combine",
    )(h1, mod, y8, rw, *wts)


_BIG_LANE = 1 << 30


def _route_kernel(lg_ref, b_ref, idx_o, w_o, rank_o, cnt_o, carry):
    i = pl.program_id(0)

    @pl.when(i == 0)
    def _():
        carry[...] = jnp.zeros_like(carry)

    tm = lg_ref.shape[1]
    gsize = N_EXPERTS // N_GROUPS
    scores = jax.nn.sigmoid(lg_ref[...])
    biased = scores + b_ref[...]
    eid = lax.broadcasted_iota(jnp.int32, (N_EXPERTS, tm), 0)

    b3 = biased.reshape(N_GROUPS, gsize, tm)
    in_g = lax.broadcasted_iota(jnp.int32, (N_GROUPS, gsize, tm), 1)
    m1 = jnp.max(b3, axis=1, keepdims=True)
    first = jnp.min(jnp.where(b3 == m1, in_g, _BIG_LANE), axis=1, keepdims=True)
    m2 = jnp.max(jnp.where(in_g == first, -jnp.inf, b3), axis=1, keepdims=True)
    gscore = (m1 + m2).reshape(N_GROUPS, tm)

    gid = lax.broadcasted_iota(jnp.int32, (N_GROUPS, tm), 0)
    beaten = jnp.zeros((N_GROUPS, tm), jnp.int32)
    for g in range(N_GROUPS):
        sg = gscore[g:g + 1, :]
        ahead = jnp.logical_or(sg > gscore, jnp.logical_and(sg == gscore, g < gid))
        beaten = beaten + ahead.astype(jnp.int32)
    keep = jnp.broadcast_to((beaten < TOPK_GROUPS).astype(jnp.int32).reshape(N_GROUPS, 1, tm),
                            (N_GROUPS, gsize, tm)).reshape(N_EXPERTS, tm)
    masked = jnp.where(keep > 0, biased, -jnp.inf)

    idxs, ws, hots = [], [], []
    for _ in range(TOP_K):
        m = jnp.max(masked, axis=0, keepdims=True)
        ix = jnp.min(jnp.where(masked == m, eid, _BIG_LANE), axis=0, keepdims=True)
        hot = eid == ix
        idxs.append(ix)
        ws.append(jnp.sum(jnp.where(hot, scores, 0.0), axis=0, keepdims=True))
        hots.append(hot)
        masked = jnp.where(hot, -jnp.inf, masked)
    wsum = ws[0]
    for r in range(1, TOP_K):
        wsum = wsum + ws[r]
    idx_o[...] = jnp.concatenate(idxs, axis=0)
    w_rows = jnp.concatenate([wr / wsum * ROUTED_SCALE for wr in ws], axis=0)

    eye = (lax.broadcasted_iota(jnp.int32, (tm, tm), 0)
           == lax.broadcasted_iota(jnp.int32, (tm, tm), 1)).astype(BF16)
    w_cols = jnp.zeros((tm, TOP_K), F32)
    rest = w_rows
    for _ in range(3):
        part = rest.astype(BF16)
        rest = rest - part.astype(F32)
        w_cols = w_cols + _dot_nt(eye, part)
    w_o[...] = w_cols

    sel = jnp.zeros((N_EXPERTS, tm), F32)
    for hot in hots:
        sel = sel + hot.astype(F32)
    sel = sel.astype(BF16)
    earlier = (lax.broadcasted_iota(jnp.int32, (tm, tm), 0)
               < lax.broadcasted_iota(jnp.int32, (tm, tm), 1)).astype(BF16)
    prefix = _dot(sel, earlier) + carry[:, 0:1]
    rank_o[...] = jnp.concatenate(
        [jnp.sum(jnp.where(hot, prefix, 0.0), axis=0, keepdims=True) for hot in hots],
        axis=0).astype(jnp.int32)
    carry[...] = carry[...] + _dot(sel, jnp.ones((tm, LANES), BF16))
    cnt_o[...] = carry[...]


def _route_call(logits_t, b_r):
    t = logits_t.shape[1]
    tm = ROW_TILE
    kt = pl.BlockSpec((TOP_K, tm), lambda i: (0, i))
    return pl.pallas_call(
        _route_kernel,
        out_shape=[jax.ShapeDtypeStruct((TOP_K, t), jnp.int32),
                   jax.ShapeDtypeStruct((t, TOP_K), F32),
                   jax.ShapeDtypeStruct((TOP_K, t), jnp.int32),
                   jax.ShapeDtypeStruct((N_EXPERTS, LANES), F32)],
        grid=(t // tm,),
        in_specs=[pl.BlockSpec((N_EXPERTS, tm), lambda i: (0, i)),
                  pl.BlockSpec((N_EXPERTS, 1), lambda i: (0, 0))],
        out_specs=[kt, pl.BlockSpec((tm, TOP_K), lambda i: (i, 0)), kt,
                   pl.BlockSpec((N_EXPERTS, LANES), lambda i: (0, 0))],
        scratch_shapes=[pltpu.VMEM((N_EXPERTS, LANES), F32)],
        compiler_params=_cparams(("arbitrary",)),
        name="route_topk",
    )(logits_t, b_r.astype(F32).reshape(N_EXPERTS, 1))


def _dest_kernel(idx_ref, rank_ref, start_ref, o_ref):
    tm = idx_ref.shape[1]
    eid = lax.broadcasted_iota(jnp.int32, (N_EXPERTS, tm), 0)
    idx = idx_ref[...]
    start = start_ref[...]
    rows = [jnp.sum(jnp.where(eid == idx[r:r + 1, :], start, 0), axis=0, keepdims=True)
            for r in range(TOP_K)]
    o_ref[...] = jnp.concatenate(rows, axis=0) + rank_ref[...]


def _dest_call(idx, rank, pad_start):
    t = idx.shape[1]
    tm = ROW_TILE
    blk = pl.BlockSpec((TOP_K, tm), lambda i: (0, i))
    return pl.pallas_call(
        _dest_kernel,
        out_shape=jax.ShapeDtypeStruct((TOP_K, t), jnp.int32),
        grid=(t // tm,),
        in_specs=[blk, blk, pl.BlockSpec((N_EXPERTS, 1), lambda i: (0, 0))],
        out_specs=blk,
        compiler_params=_cparams(("parallel",)),
        name="slot_of_assignment",
    )(idx, rank, pad_start.reshape(N_EXPERTS, 1))


def _invert_kernel(dest_ref, pad_hbm, out_hbm, tab, sem, *, tokens_per_step):
    i = pl.program_id(0)

    @pl.when(i == 0)
    def _():
        cp = pltpu.make_async_copy(pad_hbm, tab, sem)
        cp.start()
        cp.wait()

    for k in range(TOP_K):
        def body(c, val, k=k):
            for u in range(LANES):
                tab[dest_ref[k, c, u]] = val + u * TOP_K
            return val + LANES * TOP_K

        lax.fori_loop(0, tokens_per_step // LANES, body, i * (tokens_per_step * TOP_K) + k)

    @pl.when(i == pl.num_programs(0) - 1)
    def _():
        cp = pltpu.make_async_copy(tab, out_hbm, sem)
        cp.start()
        cp.wait()


def _invert_call(dest, pad_a):
    t = dest.shape[1]
    tokens_per_step = 2048
    assert t % tokens_per_step == 0 and MOE_BLOCK == 128
    return pl.pallas_call(
        functools.partial(_invert_kernel, tokens_per_step=tokens_per_step),
        out_shape=jax.ShapeDtypeStruct(pad_a.shape, jnp.int32),
        grid=(t // tokens_per_step,),
        in_specs=[pl.BlockSpec((TOP_K, tokens_per_step // LANES, LANES), lambda i: (0, i, 0),
                               memory_space=pltpu.SMEM),
                  pl.BlockSpec(memory_space=pl.ANY)],
        out_specs=pl.BlockSpec(memory_space=pl.ANY),
        scratch_shapes=[pltpu.SMEM(pad_a.shape, jnp.int32), pltpu.SemaphoreType.DMA(())],
        compiler_params=_cparams(("arbitrary",)),
        name="slot_table",
    )(dest.reshape(TOP_K, t // LANES, LANES), pad_a)


def _dispatch(idx, rank, counts, n_tok):
    n_assign = n_tok * TOP_K
    used_max = (n_assign + N_EXPERTS * (MOE_BLOCK - 1) + MOE_BLOCK - 1) // MOE_BLOCK
    nblk = -(-(used_max + TAB_AHEAD + 1) // BLOCKS_IN_FLIGHT) * BLOCKS_IN_FLIGHT
    n_slots = nblk * MOE_BLOCK
    counts = counts[:, 0].astype(jnp.int32)
    padded = (counts + MOE_BLOCK - 1) // MOE_BLOCK * MOE_BLOCK
    pad_end = jnp.cumsum(padded)
    pad_start = pad_end - padded
    dest = _dest_call(idx, rank, pad_start)
    assert (nblk - 1) % BLOCKS_IN_FLIGHT == BLOCKS_IN_FLIGHT - 1
    pad_a = n_assign + jnp.arange(n_slots, dtype=jnp.int32) % (BLOCKS_IN_FLIGHT * MOE_BLOCK)
    slot_a = _invert_call(dest, pad_a)
    t_pad = n_tok + PAD_ROWS
    tok = lax.shift_right_logical(slot_a, K_SHIFT)
    tok4 = jnp.minimum(tok, n_tok - 1) * PSLAB
    dst4 = ((slot_a & (TOP_K - 1)) * t_pad + tok) * PSLAB
    table = jnp.stack([tok4.reshape(nblk, MOE_BLOCK), dst4.reshape(nblk, MOE_BLOCK)], axis=1)
    total = (pad_end[-1] // MOE_BLOCK).astype(jnp.int32).reshape(1)
    return pad_start // MOE_BLOCK, padded // MOE_BLOCK, total, table


def _rope_tables(s, ntok):
    rows_n = s // GRID_W
    row = jnp.repeat(jnp.arange(rows_n, dtype=F32), GRID_W)
    col = jnp.tile(jnp.arange(GRID_W, dtype=F32), rows_n)
    axis_dim = HEAD_DIM // 2
    inv = jnp.power(ROPE_THETA, -jnp.arange(0, axis_dim, 2, dtype=F32) / axis_dim)
    ar = row[:, None] * inv[None]
    ac = col[:, None] * inv[None]
    ang = jnp.concatenate([ar, ar, ac, ac], -1)
    cos, sin = jnp.cos(ang), jnp.sin(ang)
    quarter = (jnp.arange(HEAD_DIM) // 16) % 2
    s_up = jnp.where(quarter == 0, -sin, 0.0)
    s_dn = jnp.where(quarter == 1, sin, 0.0)
    nctx = ntok - s

    def expand(t, ctx_val):
        t = jnp.concatenate([jnp.full((nctx, HEAD_DIM), ctx_val, F32), t], axis=0)
        return jnp.tile(t, (1, LANES // HEAD_DIM))

    return expand(cos, 1.0), expand(s_up, 0.0), expand(s_dn, 0.0)


def _head_mean_matrix(width):
    hid = np.arange(width) // HEAD_DIM
    return jnp.asarray((hid[:, None] == hid[None, :]).astype(np.float32) / HEAD_DIM).astype(BF16)


def _dup_heads(a, n_heads):
    parts = []
    for hd in range(n_heads):
        p = a[..., hd * HEAD_DIM:(hd + 1) * HEAD_DIM]
        parts += [p, p]
    return jnp.concatenate(parts, axis=-1)


def kernel(x, c, ctx, c_ctx, w_mod, b_mod, w_in, qn_a, kn_a, lam_q1, lam_k1, lam_q2, lam_k2, subln_c, w_br_a, w_br_b, w_br_c, w_out, ln1_g, ln1_b, w_router, b_router, w_sh_gate, w_sh_up, w_sh_down, w_e_gate, w_e_up, w_e_down, ln2_g, ln2_b):
    nb, s, d = x.shape
    lc = ctx.shape[1]
    depth = w_mod.shape[0]
    assert d == D_MODEL and lc == CTX_LEN and s % ROW_TILE == 0 and s % GRID_W == 0
    ntok = lc + s
    alpha = (2 * depth) ** 0.25

    tabs = _rope_tables(s, ntok)
    pos_lat = _dft_tables(s)
    pos_ctx = _dft_tables(lc)
    chan = _channel_table()
    e_q = _head_mean_matrix(A_Q_W)
    e_k = _head_mean_matrix(2 * A_KV_W)

    cc = jnp.concatenate([c, c_ctx[None, :]], axis=0)
    cc = jnp.pad(cc, ((0, (-(nb + 1)) % 8), (0, 0)))
    h_all = jnp.concatenate([ctx, x], axis=1).reshape(nb * ntok, d)

    offs = np.cumsum([0, A_Q_W, A_KV_W, A_KV_W, B_W, C_QK_W, C_QK_W, C_V_W, GATE_W])
    out = None
    for l in range(depth):
        last = l == depth - 1
        mod = _mod_call(cc, w_mod[l], b_mod[l])[:nb + 1].reshape(nb + 1, 6, d)
        mod = jnp.pad(mod, ((0, 0), (0, 2), (0, 0)))
        lam_init = 0.8 - 0.6 * math.exp(-0.3 * l)
        lam = (jnp.exp(jnp.sum(lam_q1[l].astype(F32) * lam_k1[l].astype(F32)))
               - jnp.exp(jnp.sum(lam_q2[l].astype(F32) * lam_k2[l].astype(F32)))) + lam_init
        lam = lam.reshape(1, 1).astype(F32)

        wl = w_in[l]
        seg = [wl[:, offs[i]:offs[i + 1]] for i in range(8)]
        in_wts = (seg[0].astype(BF16), _dup_heads(seg[1], A_KV_HEADS).astype(BF16),
                  _dup_heads(seg[2], A_KV_HEADS).astype(BF16), seg[3].astype(BF16),
                  seg[4].astype(BF16), seg[5].astype(BF16), seg[6].astype(BF16),
                  jnp.tile(qn_a[l].astype(F32), A_Q_HEADS).reshape(1, A_Q_W),
                  jnp.tile(kn_a[l].astype(F32), 2 * A_KV_HEADS).reshape(1, 2 * A_KV_W),
                  e_q, e_k)
        qa, ka, va, fb, qc, kc, vc = _inproj_call(h_all, mod, tabs, in_wts, nb, ntok)

        first_tile = 1 if last else 0
        r3 = lambda a: a.reshape(nb, ntok, a.shape[-1])
        subln = subln_c[l].astype(F32).reshape(1, C_V_DIM)
        attn_a = functools.partial(_attn_a_call, r3(qa), ka, r3(va), nb, ntok)
        attn_c = functools.partial(_attn_c_call, lam, r3(qc), kc, r3(vc), subln, nb, ntok)
        oa, oc = attn_a(False), attn_c(False, 1.0 - lam_init)
        oa_ctx, oc_ctx = (oa, oc) if last else (attn_a(True), attn_c(True, 1.0 - lam_init))
        ob = _fourier_call(r3(fb), chan, pos_lat, pos_ctx, nb, ntok, first_tile)

        mix_wts = (seg[7].astype(BF16), w_br_a[l].astype(BF16), w_br_b[l].astype(BF16),
                   w_br_c[l].astype(BF16), w_out[l].astype(BF16),
                   ln1_g[l].astype(F32).reshape(1, d), ln1_b[l].astype(F32).reshape(1, d),
                   w_router[l].T.astype(BF16))
        flat = lambda a: a.reshape(-1, a.shape[-1])
        h1, xp, logits = _mixout_call(h_all, mod, flat(oa), flat(oa_ctx), flat(ob), flat(oc),
                                      flat(oc_ctx), mix_wts, nb, ntok, last, alpha)

        n_tok = h1.shape[0]
        idx, rw, rank, counts = _route_call(logits, b_router[l])
        first_blk, n_blk, total, table = _dispatch(idx, rank, counts, n_tok)
        y8 = _moe_call(first_blk, n_blk, total, table, xp,
                       w_e_gate, w_e_up, w_e_down, l, n_tok)
        y8 = y8.reshape(TOP_K, (n_tok + PAD_ROWS) * PSLAB, LANES)

        comb_wts = (w_sh_gate[l].astype(BF16), w_sh_up[l].astype(BF16), w_sh_down[l].astype(BF16),
                    ln2_g[l].astype(F32).reshape(1, d), ln2_b[l].astype(F32).reshape(1, d))
        h2 = _combine_call(h1, mod, y8, rw, comb_wts, s if last else ntok, not last, nb, alpha)
        if last:
            out = h2.reshape(nb, s, d)
        else:
            h_all = h2
    return out
```

```python
import functools
import math

import numpy as np
import jax
import jax.numpy as jnp
from jax import lax
from jax.experimental import pallas as pl
from jax.experimental.pallas import tpu as pltpu

F32 = jnp.float32
BF16 = jnp.bfloat16
U32 = jnp.uint32

D_MODEL = 1024
CTX_LEN = 256
GRID_W = 64
HEAD_DIM = 64
ROPE_THETA = 10000.0
A_Q_HEADS = 8
A_KV_HEADS = 2
A_Q_W = A_Q_HEADS * HEAD_DIM
A_KV_W = A_KV_HEADS * HEAD_DIM
F_GROUPS = 4
F_GROUP_W = 128
B_W = F_GROUPS * F_GROUP_W
C_HEADS = 4
C_V_DIM = 2 * HEAD_DIM
C_QK_W = C_HEADS * 2 * HEAD_DIM
C_V_W = C_HEADS * C_V_DIM
N_BRANCH = 3
GATE_W = N_BRANCH * D_MODEL
N_EXPERTS = 256
TOP_K = 8
N_GROUPS = 8
TOPK_GROUPS = 4
EXPERT_FF = 256
SHARED_FF = 256
ROUTED_SCALE = 2.5
LN_EPS = 1e-5
RMS_EPS = 1e-6
K_SHIFT = TOP_K.bit_length() - 1
assert 1 << K_SHIFT == TOP_K

LANES = 128
PSLAB = D_MODEL // (2 * LANES)
ROW_TILE = 256
ATTN_TILE = 512
MOE_BLOCK = 128
BLOCKS_IN_FLIGHT = 4
TAB_AHEAD = 4
TAB_SLOTS = 8
PAD_ROWS = BLOCKS_IN_FLIGHT * MOE_BLOCK // TOP_K
COMBINE_TILE = 256
VMEM_LIMIT = 56 * 1024 * 1024

_Q_SCALE = HEAD_DIM ** -0.5 * math.log2(math.e)


def _cparams(sem):
    return pltpu.CompilerParams(dimension_semantics=sem, vmem_limit_bytes=VMEM_LIMIT)


def _dot(a, b):
    return jnp.dot(a, b, preferred_element_type=F32)


def _dot_nt(a, b):
    return lax.dot_general(a, b, (((1,), (1,)), ((), ())), preferred_element_type=F32)


def _layer_norm_rows(x):
    mu = jnp.mean(x, axis=-1, keepdims=True)
    xc = x - mu
    var = jnp.mean(xc * xc, axis=-1, keepdims=True)
    return xc * lax.rsqrt(var + LN_EPS)


def _dot_split(x, e):
    hi = x.astype(BF16)
    lo = (x - hi.astype(F32)).astype(BF16)
    return _dot(hi, e) + _dot(lo, e)


def _mod_kernel(c_ref, w_ref, b_ref, o_ref):
    c = c_ref[...]
    sc = c * jax.nn.sigmoid(c)
    o_ref[...] = _dot(sc.astype(BF16), w_ref[...].astype(BF16)) + b_ref[...]


def _mod_call(cc, w_mod_l, b_mod_l):
    r = cc.shape[0]
    n = w_mod_l.shape[1]
    tn = D_MODEL
    return pl.pallas_call(
        _mod_kernel,
        out_shape=jax.ShapeDtypeStruct((r, n), F32),
        grid=(n // tn,),
        in_specs=[
            pl.BlockSpec((r, D_MODEL), lambda j: (0, 0)),
            pl.BlockSpec((D_MODEL, tn), lambda j: (0, j)),
            pl.BlockSpec((1, tn), lambda j: (0, j)),
        ],
        out_specs=pl.BlockSpec((r, tn), lambda j: (0, j)),
        compiler_params=_cparams(("arbitrary",)),
        name="mod_vectors",
    )(cc, w_mod_l, b_mod_l.reshape(1, n))


def _rope_cols(x, cos, sin_up, sin_dn):
    cols = []
    for c in range(x.shape[1] // LANES):
        xc = x[:, c * LANES:(c + 1) * LANES]
        up = pltpu.roll(xc, LANES - 16, axis=1)
        dn = pltpu.roll(xc, 16, axis=1)
        cols.append(xc * cos + up * sin_up + dn * sin_dn)
    return jnp.concatenate(cols, axis=1) if len(cols) > 1 else cols[0]


def _inproj_kernel(h_ref, mod_ref, cos_ref, su_ref, sd_ref,
                   wq_ref, wk_ref, wv_ref, wf_ref, wqc_ref, wkc_ref, wvc_ref,
                   qn_ref, kn_ref, eq_ref, ek_ref,
                   qa_o, ka_o, va_o, fb_o, qc_o, kc_o, vc_o):
    h = h_ref[...]
    shift = mod_ref[0, 0:1, :]
    scale = mod_ref[0, 1:2, :]
    u = (_layer_norm_rows(h) * (1.0 + scale) + shift).astype(BF16)
    cos = cos_ref[...]
    s_up = su_ref[...]
    s_dn = sd_ref[...]

    q = _dot(u, wq_ref[...])
    ms = _dot_split(q * q, eq_ref[...])
    q = q * lax.rsqrt(ms + RMS_EPS) * qn_ref[...]
    qa_o[...] = (_rope_cols(q, cos, s_up, s_dn) * _Q_SCALE).astype(BF16)

    k = _dot(u, wk_ref[...])
    ms = _dot_split(k * k, ek_ref[...])
    k = k * lax.rsqrt(ms + RMS_EPS) * kn_ref[...]
    ka_o[0] = jnp.transpose(_rope_cols(k, cos, s_up, s_dn)).astype(BF16)

    va_o[...] = _dot(u, wv_ref[...]).astype(BF16)
    fb_o[...] = _dot(u, wf_ref[...]).astype(BF16)
    qc = _dot(u, wqc_ref[...])
    qc_o[...] = (_rope_cols(qc, cos, s_up, s_dn) * _Q_SCALE).astype(BF16)
    kc = _dot(u, wkc_ref[...])
    kc_o[0] = jnp.transpose(_rope_cols(kc, cos, s_up, s_dn)).astype(BF16)
    vc_o[...] = _dot(u, wvc_ref[...]).astype(BF16)


def _inproj_call(h_all, mod, tabs, wts, nb, ntok):
    rows = h_all.shape[0]
    tpb = ntok // ROW_TILE
    cos, s_up, s_dn = tabs

    def full(a):
        return pl.BlockSpec(a.shape, lambda i: (0,) * a.ndim)

    def rowspec(w):
        return pl.BlockSpec((ROW_TILE, w), lambda i: (i, 0))

    tabspec = pl.BlockSpec((ROW_TILE, LANES), lambda i: (i % tpb, 0))
    modspec = pl.BlockSpec((1, 8, D_MODEL),
                           lambda i: (jnp.where(i % tpb == 0, nb, i // tpb), 0, 0))
    widths = (A_Q_W, 2 * A_KV_W, 2 * A_KV_W, B_W, C_QK_W, C_QK_W, C_V_W)
    transposed = (1, 5)

    def oshape(n, w):
        return (nb, w, ntok) if n in transposed else (rows, w)

    def ospec(n, w):
        if n in transposed:
            return pl.BlockSpec((1, w, ROW_TILE), lambda i: (i // tpb, 0, i % tpb))
        return rowspec(w)

    return pl.pallas_call(
        _inproj_kernel,
        out_shape=[jax.ShapeDtypeStruct(oshape(n, w), BF16) for n, w in enumerate(widths)],
        grid=(rows // ROW_TILE,),
        in_specs=[rowspec(D_MODEL), modspec, tabspec, tabspec, tabspec]
                 + [full(w) for w in wts],
        out_specs=[ospec(n, w) for n, w in enumerate(widths)],
        compiler_params=_cparams(("parallel",)),
        name="in_projection",
    )(h_all, mod, cos, s_up, s_dn, *wts)


def _softmax_parts(s):
    m = jnp.max(s, axis=-1, keepdims=True)
    e = jnp.exp2(s - m)
    return e, jnp.sum(e, axis=-1, keepdims=True)


def _query_rows(q_refs):
    rows = [r[0] for r in q_refs]
    return rows[0] if len(rows) == 1 else jnp.concatenate(rows, axis=0)


def _gqa_kernel(*refs):
    q_refs, (k_ref, v_ref, o_ref) = refs[:-3], refs[-3:]
    q = _query_rows(q_refs)
    lane = lax.broadcasted_iota(jnp.int32, (1, LANES), 1)
    low = lane < HEAD_DIM
    for c in range(A_Q_W // LANES):
        kvh = (2 * c) // (A_Q_HEADS // A_KV_HEADS)
        qc = q[:, c * LANES:(c + 1) * LANES]
        kk = k_ref[0, kvh * LANES:(kvh + 1) * LANES, :]
        vv = v_ref[0, :, kvh * LANES:(kvh + 1) * LANES]
        halves = []
        for keep in (low, jnp.logical_not(low)):
            qm = jnp.where(keep, qc, jnp.zeros_like(qc))
            e, l = _softmax_parts(_dot(qm, kk))
            halves.append(_dot(e.astype(BF16), vv) * (1.0 / l))
        o_ref[0, :, c * LANES:(c + 1) * LANES] = jnp.where(low, halves[0], halves[1]).astype(BF16)


def _diff_kernel(lam_ref, *refs, out_scale):
    q_refs, (k_ref, v_ref, g_ref, o_ref) = refs[:-4], refs[-4:]
    q = _query_rows(q_refs)
    lane = lax.broadcasted_iota(jnp.int32, (1, LANES), 1)
    low = lane < HEAD_DIM
    lam = lam_ref[0, 0]
    for hd in range(C_HEADS):
        sl = slice(hd * LANES, (hd + 1) * LANES)
        qc = q[:, sl]
        kk = k_ref[0, sl, :]
        vv = v_ref[0, :, sl]
        q1 = jnp.where(low, qc, jnp.zeros_like(qc))
        q2 = jnp.where(low, jnp.zeros_like(qc), qc)
        e1, l1 = _softmax_parts(_dot(q1, kk))
        e2, l2 = _softmax_parts(_dot(q2, kk))
        o = _dot(e1.astype(BF16), vv) * (1.0 / l1) - _dot(e2.astype(BF16), vv) * (lam / l2)
        ms = jnp.mean(o * o, axis=-1, keepdims=True)
        o = o * lax.rsqrt(ms + RMS_EPS) * g_ref[...] * out_scale
        o_ref[0, :, sl] = o.astype(BF16)


def _attn_call(kernel_fn, name, head, tail, q, kt, v, nb, ntok, wq, wkv, ctx_queries):
    cblk = CTX_LEN // ROW_TILE
    if ctx_queries:
        rows, nk, grid = CTX_LEN, CTX_LEN, (nb, 1)
        qspecs = [pl.BlockSpec((1, ROW_TILE, wq), lambda b, j: (b, 0, 0))]
    else:
        rows, nk, grid = ATTN_TILE, ntok, (nb, (ntok - CTX_LEN) // ATTN_TILE)
        per = ATTN_TILE // ROW_TILE
        qspecs = [pl.BlockSpec((1, ROW_TILE, wq), lambda b, j, r=r: (b, cblk + per * j + r, 0))
                  for r in range(per)]
    kspec = pl.BlockSpec((1, wkv, nk), lambda b, j: (b, 0, 0))
    vspec = pl.BlockSpec((1, nk, wkv), lambda b, j: (b, 0, 0))
    return pl.pallas_call(
        kernel_fn,
        out_shape=jax.ShapeDtypeStruct((nb, rows * grid[1], wq), BF16),
        grid=grid,
        in_specs=[sp for _, sp in head] + qspecs + [kspec, vspec] + [sp for _, sp in tail],
        out_specs=pl.BlockSpec((1, rows, wq), lambda b, j: (b, j, 0)),
        compiler_params=_cparams(("parallel", "arbitrary")),
        name=name,
    )(*[a for a, _ in head], *([q] * len(qspecs)), kt, v, *[a for a, _ in tail])


def _attn_a_call(qa, ka, va, nb, ntok, ctx_queries):
    return _attn_call(_gqa_kernel, "gqa_attention_ctx" if ctx_queries else "gqa_attention", [], [],
                      qa, ka, va, nb, ntok, A_Q_W, 2 * A_KV_W, ctx_queries)


def _attn_c_call(lam, qc, kc, vc, subln, nb, ntok, ctx_queries, out_scale):
    head = [(lam, pl.BlockSpec(memory_space=pltpu.SMEM))]
    tail = [(subln, pl.BlockSpec((1, C_V_DIM), lambda b, j: (0, 0)))]
    return _attn_call(functools.partial(_diff_kernel, out_scale=out_scale),
                      "diff_attention_ctx" if ctx_queries else "diff_attention", head, tail,
                      qc, kc, vc, nb, ntok, C_QK_W, C_QK_W, ctx_queries)


def _fourier_kernel(f_ref, chan_ref, pos_ref, posc_ref, o_ref, g_ref, *, ntok, first_tile):
    j = pl.program_id(1) + first_tile
    nlat = ntok - CTX_LEN

    def channel_stage(rows):
        g = _dot(rows, chan_ref[...])
        return jnp.concatenate([g[:, :B_W], g[:, B_W:]], axis=0).astype(BF16)

    if first_tile == 0:
        @pl.when(j == 0)
        def _():
            gc = channel_stage(f_ref[0, 0:CTX_LEN, :])
            y = _dot(posc_ref[...], gc) * (1.0 / math.sqrt(CTX_LEN * F_GROUP_W))
            o_ref[0] = y.astype(BF16)

    @pl.when(j == 1)
    def _():
        g_ref[...] = channel_stage(f_ref[0, CTX_LEN:ntok, :])

    @pl.when(j >= 1)
    def _():
        y = _dot(pos_ref[...], g_ref[...]) * (1.0 / math.sqrt(nlat * F_GROUP_W))
        o_ref[0] = y.astype(BF16)


def _fourier_call(fb, chan, pos, posc, nb, ntok, first_tile):
    tpb = ntok // ROW_TILE
    nlat = ntok - CTX_LEN
    return pl.pallas_call(
        functools.partial(_fourier_kernel, ntok=ntok, first_tile=first_tile),
        out_shape=jax.ShapeDtypeStruct((nb, ntok - first_tile * ROW_TILE, B_W), BF16),
        grid=(nb, tpb - first_tile),
        in_specs=[
            pl.BlockSpec((1, ntok, B_W), lambda b, j: (b, 0, 0)),
            pl.BlockSpec(chan.shape, lambda b, j: (0, 0)),
            pl.BlockSpec((ROW_TILE, 2 * nlat),
                         lambda b, j: (jnp.maximum(j + first_tile - 1, 0), 0)),
            pl.BlockSpec(posc.shape, lambda b, j: (0, 0)),
        ],
        out_specs=pl.BlockSpec((1, ROW_TILE, B_W), lambda b, j: (b, j, 0)),
        scratch_shapes=[pltpu.VMEM((2 * nlat, B_W), BF16)],
        compiler_params=_cparams(("parallel", "arbitrary")),
        name="fourier_mix",
    )(fb, chan, pos, posc)


def _dft_tables(n):
    n1 = 32
    n0 = n // n1
    k = np.arange(n, dtype=np.int64)
    a = 2.0 * np.pi * ((k[:, None] * np.arange(n1)[None, :] * n0) % n) / n
    b = 2.0 * np.pi * ((k[:, None] * np.arange(n0)[None, :]) % n) / n
    ca, sa = jnp.asarray(np.cos(a), F32)[:, :, None], jnp.asarray(np.sin(a), F32)[:, :, None]
    cb, sb = jnp.asarray(np.cos(b), F32)[:, None, :], jnp.asarray(np.sin(b), F32)[:, None, :]
    cos = (ca * cb - sa * sb).reshape(n, n)
    sin = (sa * cb + ca * sb).reshape(n, n)
    return jnp.concatenate([cos, -sin], axis=1).astype(BF16)


def _channel_table():
    c = np.arange(F_GROUP_W)
    ang = 2.0 * np.pi * ((c[:, None] * c[None, :]) % F_GROUP_W) / F_GROUP_W
    eye = np.eye(F_GROUPS)
    cos = np.kron(eye, np.cos(ang))
    sin = np.kron(eye, np.sin(ang))
    return jnp.asarray(np.concatenate([cos, sin], axis=1), F32).astype(BF16)


def _store_packed_slabs(dst_ref, u):
    rows = u.shape[0]
    for jb in range(PSLAB):
        lo = u[:, 2 * jb * LANES:(2 * jb + 1) * LANES]
        hi = u[:, (2 * jb + 1) * LANES:(2 * jb + 2) * LANES]
        dst_ref[pl.ds(jb, rows, stride=PSLAB), :] = pltpu.pack_elementwise([lo, hi], packed_dtype=BF16)


def _load_packed_slabs(src_ref, rows):
    out = []
    for jb in range(PSLAB):
        words = src_ref[pl.ds(jb, rows, stride=PSLAB), :]
        out.append((pltpu.unpack_elementwise(words, index=0, packed_dtype=BF16, unpacked_dtype=F32),
                    pltpu.unpack_elementwise(words, index=1, packed_dtype=BF16, unpacked_dtype=F32)))
    return out


def _mixout_kernel(h_ref, mod_ref, oa_ref, oa_ctx_ref, ob_ref, oc_ref, oc_ctx_ref,
                   wg_ref, wa_ref, wb_ref, wc_ref, wo_ref, g1_ref, b1_ref, wr_ref,
                   h1_o, xp_o, lg_o, *, alpha, ctx_every):
    h = h_ref[...]
    m = mod_ref[0]
    u = (_layer_norm_rows(h) * (1.0 + m[1:2, :]) + m[0:1, :]).astype(BF16)
    oa, oc = oa_ref[...], oc_ref[...]
    if ctx_every:
        is_ctx = pl.program_id(0) % ctx_every == 0
        oa = jnp.where(is_ctx, oa_ctx_ref[...], oa)
        oc = jnp.where(is_ctx, oc_ctx_ref[...], oc)
    y = None
    for n, (o, w_ref) in enumerate(((oa, wa_ref), (ob_ref[...], wb_ref), (oc, wc_ref))):
        gate = jax.nn.sigmoid(_dot(u, wg_ref[:, n * D_MODEL:(n + 1) * D_MODEL]))
        t = gate * _dot(o, w_ref[...])
        y = t if y is None else y + t
    z = _dot(y.astype(BF16), wo_ref[...])
    h1 = _layer_norm_rows(alpha * h + m[2:3, :] * z) * g1_ref[...] + b1_ref[...]
    h1_o[...] = h1
    u2 = _layer_norm_rows(h1) * (1.0 + m[4:5, :]) + m[3:4, :]
    _store_packed_slabs(xp_o, u2)
    lg_o[...] = _dot_nt(wr_ref[...], u2.astype(BF16))


def _tile_maps(nb, ntok, lat_only):
    tpb = ntok // ROW_TILE
    if lat_only:
        lpb = tpb - 1
        n_tiles = nb * lpb
        src = lambda i: (i // lpb) * tpb + 1 + i % lpb
        modi = lambda i: i // lpb
    else:
        n_tiles = nb * tpb
        src = lambda i: i
        modi = lambda i: jnp.where(i % tpb == 0, nb, i // tpb)
    return n_tiles, src, modi


def _mixout_call(h_all, mod, oa, oa_ctx, ob, oc, oc_ctx, wts, nb, ntok, lat_only, alpha):
    n_tiles, src, modi = _tile_maps(nb, ntok, lat_only)
    rows_out = n_tiles * ROW_TILE
    tpb = ntok // ROW_TILE
    lpb = tpb - CTX_LEN // ROW_TILE
    if lat_only:
        lat_blk = lambda i: i
        ctx_blk = lambda i: 0
    else:
        lat_blk = lambda i: (i // tpb) * lpb + jnp.maximum(i % tpb - 1, 0)
        ctx_blk = lambda i: i // tpb

    def full(a):
        return pl.BlockSpec(a.shape, lambda i: (0,) * a.ndim)

    def inrow(w):
        return pl.BlockSpec((ROW_TILE, w), lambda i: (src(i), 0))

    def outrow(w):
        return pl.BlockSpec((ROW_TILE, w), lambda i: (i, 0))

    modspec = pl.BlockSpec((1, 8, D_MODEL), lambda i: (modi(i), 0, 0))
    def latrow(w):
        return pl.BlockSpec((ROW_TILE, w), lambda i: (lat_blk(i), 0))

    def ctxrow(w):
        return pl.BlockSpec((ROW_TILE, w), lambda i: (ctx_blk(i), 0))

    return pl.pallas_call(
        functools.partial(_mixout_kernel, alpha=alpha, ctx_every=0 if lat_only else tpb),
        out_shape=[jax.ShapeDtypeStruct((rows_out, D_MODEL), F32),
                   jax.ShapeDtypeStruct((rows_out * PSLAB, LANES), U32),
                   jax.ShapeDtypeStruct((N_EXPERTS, rows_out), F32)],
        grid=(n_tiles,),
        in_specs=[inrow(D_MODEL), modspec, latrow(A_Q_W), ctxrow(A_Q_W), outrow(B_W),
                  latrow(C_V_W), ctxrow(C_V_W)]
                 + [full(w) for w in wts],
        out_specs=[outrow(D_MODEL), pl.BlockSpec((ROW_TILE * PSLAB, LANES), lambda i: (i, 0)),
                   pl.BlockSpec((N_EXPERTS, ROW_TILE), lambda i: (0, i))],
        compiler_params=_cparams(("parallel",)),
        name="mixer_output",
    )(h_all, mod, oa, oa_ctx, ob, oc, oc_ctx, *wts)


def _moe_kernel(first_ref, count_ref, total_ref,
                tab_hbm, xp_ref, wg_ref, wu_ref, wd_ref,
                out_hbm,
                tab, gath, wgu_s, wd_s, ybuf, sem, tsem, *, t_pad, n_tok, lead_row):
    nfl = BLOCKS_IN_FLIGHT
    tmask = TAB_SLOTS - 1
    e = pl.program_id(0)
    total = total_ref[0]

    def row_copy(s, m, dst4):
        return pltpu.make_async_copy(ybuf.at[s, pl.ds(PSLAB * m, PSLAB), :],
                                     out_hbm.at[pl.ds(pl.multiple_of(dst4, PSLAB), PSLAB), :],
                                     sem.at[s])

    def wait_rows(s):
        for m in range(MOE_BLOCK):
            row_copy(s, m, 0).wait()

    def send_rows(ts, ys):
        for m in range(MOE_BLOCK):
            row_copy(ys, m, tab[ts, 1, m]).start()

    def tab_copy(row, s):
        return pltpu.make_async_copy(tab_hbm.at[row], tab.at[s], tsem.at[s])

    def gather_rows(ts, gs):
        for m in range(MOE_BLOCK):
            t4 = pl.multiple_of(tab[ts, 0, m], PSLAB)
            gath[gs, PSLAB * m:PSLAB * (m + 1), :] = xp_ref[pl.ds(t4, PSLAB), :]

    @pl.when(e == 0)
    def _():
        ybuf[...] = jnp.zeros_like(ybuf)
        tab_copy(lead_row, tmask).start()
        for r in range(TAB_AHEAD):
            tab_copy(r, r).start()
        for b in range(nfl - 1):
            for m in range(MOE_BLOCK):
                q = b * MOE_BLOCK + m
                row_copy(b, m, ((q % TOP_K) * t_pad + n_tok + q // TOP_K) * PSLAB).start(priority=m % 2)
        tab_copy(lead_row, tmask).wait()
        tab_copy(0, 0).wait()
        gather_rows(0, 0)

    wgu_s[:, 0:EXPERT_FF] = wg_ref[0, 0].astype(BF16)
    wgu_s[:, EXPERT_FF:2 * EXPERT_FF] = wu_ref[0, 0].astype(BF16)
    wd_s[...] = wd_ref[0, 0].astype(BF16)

    def block(j, carry):
        g = first_ref[e] + j
        tab_copy(g + 1, (g + 1) & tmask).wait()
        tab_copy(g + TAB_AHEAD, (g + TAB_AHEAD) & tmask).start()
        ys = lax.rem(g, nfl)
        wait_rows(ys)

        send_rows((g - 1) & tmask, lax.rem(g + nfl - 1, nfl))
        gather_rows((g + 1) & tmask, (g + 1) & 1)

        acc = None
        for jb, (lo, hi) in enumerate(_load_packed_slabs(gath.at[g & 1], MOE_BLOCK)):
            a = jnp.concatenate([lo, hi], axis=1).astype(BF16)
            t = _dot(a, wgu_s[jb * 2 * LANES:(jb + 1) * 2 * LANES, :])
            acc = t if acc is None else acc + t
        gt = acc[:, 0:EXPERT_FF]
        act = (gt * jax.nn.sigmoid(gt)) * acc[:, EXPERT_FF:2 * EXPERT_FF]
        y = _dot(act.astype(BF16), wd_s[...])
        _store_packed_slabs(ybuf.at[ys], y)
        return carry

    lax.fori_loop(0, count_ref[e], block, 0)

    @pl.when(e == pl.num_programs(0) - 1)
    def _():
        send_rows((total - 1) & tmask, lax.rem(total + nfl - 1, nfl))
        for s in range(nfl):
            wait_rows(s)
        for r in range(1, TAB_AHEAD):
            tab_copy(0, (total + r) & tmask).wait()


def _moe_call(first_blk, n_blk, total, table, xp4, wg, wu, wd, layer, n_tok):
    t_pad = n_tok + PAD_ROWS
    lead_row = table.shape[0] - 1
    wspec = lambda r, c: pl.BlockSpec((1, 1, r, c), lambda e, fb, nb, tt: (layer, e, 0, 0))
    grid_spec = pltpu.PrefetchScalarGridSpec(
        num_scalar_prefetch=3,
        grid=(N_EXPERTS,),
        in_specs=[
            pl.BlockSpec(memory_space=pl.ANY),
            pl.BlockSpec(memory_space=pltpu.VMEM),
            wspec(D_MODEL, EXPERT_FF), wspec(D_MODEL, EXPERT_FF), wspec(EXPERT_FF, D_MODEL),
        ],
        out_specs=pl.BlockSpec(memory_space=pl.ANY),
        scratch_shapes=[
            pltpu.SMEM((TAB_SLOTS, 2, MOE_BLOCK), jnp.int32),
            pltpu.VMEM((2, MOE_BLOCK * PSLAB, LANES), U32),
            pltpu.VMEM((D_MODEL, 2 * EXPERT_FF), BF16),
            pltpu.VMEM((EXPERT_FF, D_MODEL), BF16),
            pltpu.VMEM((BLOCKS_IN_FLIGHT, MOE_BLOCK * PSLAB, LANES), U32),
            pltpu.SemaphoreType.DMA((BLOCKS_IN_FLIGHT,)),
            pltpu.SemaphoreType.DMA((TAB_SLOTS,)),
        ],
    )
    return pl.pallas_call(
        functools.partial(_moe_kernel, t_pad=t_pad, n_tok=n_tok, lead_row=lead_row),
        out_shape=jax.ShapeDtypeStruct((TOP_K * t_pad * PSLAB, LANES), U32),
        grid_spec=grid_spec,
        compiler_params=_cparams(("arbitrary",)),
        name="routed_experts",
    )(first_blk, n_blk, total, table, xp4, wg, wu, wd)


def _combine_kernel(h1_ref, mod_ref, y8_ref, rw_ref, sg_ref, su_ref, sd_ref, g2_ref, b2_ref, o_ref,
                    *, alpha):
    h1 = h1_ref[...]
    m = mod_ref[0]
    u2 = (_layer_norm_rows(h1) * (1.0 + m[4:5, :]) + m[3:4, :]).astype(BF16)
    g = _dot(u2, sg_ref[...])
    act = (g * jax.nn.sigmoid(g)) * _dot(u2, su_ref[...])
    f = _dot(act.astype(BF16), sd_ref[...])
    tm = h1.shape[0]
    rw = rw_ref[...]
    wk = [jnp.broadcast_to(rw[:, kk:kk + 1], (tm, LANES)) for kk in range(TOP_K)]
    cols = [None] * (2 * PSLAB)
    for kk in range(TOP_K):
        for jb, pair in enumerate(_load_packed_slabs(y8_ref.at[kk], tm)):
            for half in range(2):
                t = wk[kk] * pair[half]
                c = 2 * jb + half
                cols[c] = t if cols[c] is None else cols[c] + t
    f = f + jnp.concatenate(cols, axis=1)
    o_ref[...] = _layer_norm_rows(alpha * h1 + m[5:6, :] * f) * g2_ref[...] + b2_ref[...]


def _combine_call(h1, mod, y8, rw, wts, rows_per_group, group_is_ctx_first, nb, alpha):
    rows = h1.shape[0]
    tm = COMBINE_TILE
    tpg = rows_per_group // tm
    if group_is_ctx_first:
        first = CTX_LEN // tm
        modi = lambda i: jnp.where(i % tpg < first, nb, i // tpg)
    else:
        modi = lambda i: i // tpg

    def full(a):
        return pl.BlockSpec(a.shape, lambda i: (0,) * a.ndim)

    return pl.pallas_call(
        functools.partial(_combine_kernel, alpha=alpha),
        out_shape=jax.ShapeDtypeStruct((rows, D_MODEL), F32),
        grid=(rows // tm,),
        in_specs=[pl.BlockSpec((tm, D_MODEL), lambda i: (i, 0)),
                  pl.BlockSpec((1, 8, D_MODEL), lambda i: (modi(i), 0, 0)),
                  pl.BlockSpec((TOP_K, tm * PSLAB, LANES), lambda i: (0, i, 0)),
                  pl.BlockSpec((tm, TOP_K), lambda i: (i, 0))]
                 + [full(w) for w in wts],
        out_specs=pl.BlockSpec((tm, D_MODEL), lambda i: (i, 0)),
        compiler_params=_cparams(("parallel",)),
        name="moe_combine",
    )(h1, mod, y8, rw, *wts)


_BIG_LANE = 1 << 30


def _route_kernel(lg_ref, b_ref, idx_o, w_o, rank_o, cnt_o, carry):
    i = pl.program_id(0)

    @pl.when(i == 0)
    def _():
        carry[...] = jnp.zeros_like(carry)

    tm = lg_ref.shape[1]
    gsize = N_EXPERTS // N_GROUPS
    scores = jax.nn.sigmoid(lg_ref[...])
    biased = scores + b_ref[...]
    eid = lax.broadcasted_iota(jnp.int32, (N_EXPERTS, tm), 0)

    b3 = biased.reshape(N_GROUPS, gsize, tm)
    in_g = lax.broadcasted_iota(jnp.int32, (N_GROUPS, gsize, tm), 1)
    m1 = jnp.max(b3, axis=1, keepdims=True)
    first = jnp.min(jnp.where(b3 == m1, in_g, _BIG_LANE), axis=1, keepdims=True)
    m2 = jnp.max(jnp.where(in_g == first, -jnp.inf, b3), axis=1, keepdims=True)
    gscore = (m1 + m2).reshape(N_GROUPS, tm)

    gid = lax.broadcasted_iota(jnp.int32, (N_GROUPS, tm), 0)
    beaten = jnp.zeros((N_GROUPS, tm), jnp.int32)
    for g in range(N_GROUPS):
        sg = gscore[g:g + 1, :]
        ahead = jnp.logical_or(sg > gscore, jnp.logical_and(sg == gscore, g < gid))
        beaten = beaten + ahead.astype(jnp.int32)
    keep = jnp.broadcast_to((beaten < TOPK_GROUPS).astype(jnp.int32).reshape(N_GROUPS, 1, tm),
                            (N_GROUPS, gsize, tm)).reshape(N_EXPERTS, tm)
    masked = jnp.where(keep > 0, biased, -jnp.inf)

    idxs, ws, hots = [], [], []
    for _ in range(TOP_K):
        m = jnp.max(masked, axis=0, keepdims=True)
        ix = jnp.min(jnp.where(masked == m, eid, _BIG_LANE), axis=0, keepdims=True)
        hot = eid == ix
        idxs.append(ix)
        ws.append(jnp.sum(jnp.where(hot, scores, 0.0), axis=0, keepdims=True))
        hots.append(hot)
        masked = jnp.where(hot, -jnp.inf, masked)
    wsum = ws[0]
    for r in range(1, TOP_K):
        wsum = wsum + ws[r]
    idx_o[...] = jnp.concatenate(idxs, axis=0)
    w_rows = jnp.concatenate([wr / wsum * ROUTED_SCALE for wr in ws], axis=0)

    eye = (lax.broadcasted_iota(jnp.int32, (tm, tm), 0)
           == lax.broadcasted_iota(jnp.int32, (tm, tm), 1)).astype(BF16)
    w_cols = jnp.zeros((tm, TOP_K), F32)
    rest = w_rows
    for _ in range(3):
        part = rest.astype(BF16)
        rest = rest - part.astype(F32)
        w_cols = w_cols + _dot_nt(eye, part)
    w_o[...] = w_cols

    sel = jnp.zeros((N_EXPERTS, tm), F32)
    for hot in hots:
        sel = sel + hot.astype(F32)
    sel = sel.astype(BF16)
    earlier = (lax.broadcasted_iota(jnp.int32, (tm, tm), 0)
               < lax.broadcasted_iota(jnp.int32, (tm, tm), 1)).astype(BF16)
    prefix = _dot(sel, earlier) + carry[:, 0:1]
    rank_o[...] = jnp.concatenate(
        [jnp.sum(jnp.where(hot, prefix, 0.0), axis=0, keepdims=True) for hot in hots],
        axis=0).astype(jnp.int32)
    carry[...] = carry[...] + _dot(sel, jnp.ones((tm, LANES), BF16))
    cnt_o[...] = carry[...]


def _route_call(logits_t, b_r):
    t = logits_t.shape[1]
    tm = ROW_TILE
    kt = pl.BlockSpec((TOP_K, tm), lambda i: (0, i))
    return pl.pallas_call(
        _route_kernel,
        out_shape=[jax.ShapeDtypeStruct((TOP_K, t), jnp.int32),
                   jax.ShapeDtypeStruct((t, TOP_K), F32),
                   jax.ShapeDtypeStruct((TOP_K, t), jnp.int32),
                   jax.ShapeDtypeStruct((N_EXPERTS, LANES), F32)],
        grid=(t // tm,),
        in_specs=[pl.BlockSpec((N_EXPERTS, tm), lambda i: (0, i)),
                  pl.BlockSpec((N_EXPERTS, 1), lambda i: (0, 0))],
        out_specs=[kt, pl.BlockSpec((tm, TOP_K), lambda i: (i, 0)), kt,
                   pl.BlockSpec((N_EXPERTS, LANES), lambda i: (0, 0))],
        scratch_shapes=[pltpu.VMEM((N_EXPERTS, LANES), F32)],
        compiler_params=_cparams(("arbitrary",)),
        name="route_topk",
    )(logits_t, b_r.astype(F32).reshape(N_EXPERTS, 1))


def _dest_kernel(idx_ref, rank_ref, start_ref, o_ref):
    tm = idx_ref.shape[1]
    eid = lax.broadcasted_iota(jnp.int32, (N_EXPERTS, tm), 0)
    idx = idx_ref[...]
    start = start_ref[...]
    rows = [jnp.sum(jnp.where(eid == idx[r:r + 1, :], start, 0), axis=0, keepdims=True)
            for r in range(TOP_K)]
    o_ref[...] = jnp.concatenate(rows, axis=0) + rank_ref[...]


def _dest_call(idx, rank, pad_start):
    t = idx.shape[1]
    tm = ROW_TILE
    blk = pl.BlockSpec((TOP_K, tm), lambda i: (0, i))
    return pl.pallas_call(
        _dest_kernel,
        out_shape=jax.ShapeDtypeStruct((TOP_K, t), jnp.int32),
        grid=(t // tm,),
        in_specs=[blk, blk, pl.BlockSpec((N_EXPERTS, 1), lambda i: (0, 0))],
        out_specs=blk,
        compiler_params=_cparams(("parallel",)),
        name="slot_of_assignment",
    )(idx, rank, pad_start.reshape(N_EXPERTS, 1))


def _invert_kernel(dest_ref, pad_hbm, out_hbm, tab, sem, *, tokens_per_step):
    i = pl.program_id(0)

    @pl.when(i == 0)
    def _():
        cp = pltpu.make_async_copy(pad_hbm, tab, sem)
        cp.start()
        cp.wait()

    for k in range(TOP_K):
        def body(c, val, k=k):
            for u in range(LANES):
                tab[dest_ref[k, c, u]] = val + u * TOP_K
            return val + LANES * TOP_K

        lax.fori_loop(0, tokens_per_step // LANES, body, i * (tokens_per_step * TOP_K) + k)

    @pl.when(i == pl.num_programs(0) - 1)
    def _():
        cp = pltpu.make_async_copy(tab, out_hbm, sem)
        cp.start()
        cp.wait()


def _invert_call(dest, pad_a):
    t = dest.shape[1]
    tokens_per_step = 2048
    assert t % tokens_per_step == 0 and MOE_BLOCK == 128
    return pl.pallas_call(
        functools.partial(_invert_kernel, tokens_per_step=tokens_per_step),
        out_shape=jax.ShapeDtypeStruct(pad_a.shape, jnp.int32),
        grid=(t // tokens_per_step,),
        in_specs=[pl.BlockSpec((TOP_K, tokens_per_step // LANES, LANES), lambda i: (0, i, 0),
                               memory_space=pltpu.SMEM),
                  pl.BlockSpec(memory_space=pl.ANY)],
        out_specs=pl.BlockSpec(memory_space=pl.ANY),
        scratch_shapes=[pltpu.SMEM(pad_a.shape, jnp.int32), pltpu.SemaphoreType.DMA(())],
        compiler_params=_cparams(("arbitrary",)),
        name="slot_table",
    )(dest.reshape(TOP_K, t // LANES, LANES), pad_a)


def _dispatch(idx, rank, counts, n_tok):
    n_assign = n_tok * TOP_K
    used_max = (n_assign + N_EXPERTS * (MOE_BLOCK - 1) + MOE_BLOCK - 1) // MOE_BLOCK
    nblk = -(-(used_max + TAB_AHEAD + 1) // BLOCKS_IN_FLIGHT) * BLOCKS_IN_FLIGHT
    n_slots = nblk * MOE_BLOCK
    counts = counts[:, 0].astype(jnp.int32)
    padded = (counts + MOE_BLOCK - 1) // MOE_BLOCK * MOE_BLOCK
    pad_end = jnp.cumsum(padded)
    pad_start = pad_end - padded
    dest = _dest_call(idx, rank, pad_start)
    assert (nblk - 1) % BLOCKS_IN_FLIGHT == BLOCKS_IN_FLIGHT - 1
    pad_a = n_assign + jnp.arange(n_slots, dtype=jnp.int32) % (BLOCKS_IN_FLIGHT * MOE_BLOCK)
    slot_a = _invert_call(dest, pad_a)
    t_pad = n_tok + PAD_ROWS
    tok = lax.shift_right_logical(slot_a, K_SHIFT)
    tok4 = jnp.minimum(tok, n_tok - 1) * PSLAB
    dst4 = ((slot_a & (TOP_K - 1)) * t_pad + tok) * PSLAB
    table = jnp.stack([tok4.reshape(nblk, MOE_BLOCK), dst4.reshape(nblk, MOE_BLOCK)], axis=1)
    total = (pad_end[-1] // MOE_BLOCK).astype(jnp.int32).reshape(1)
    return pad_start // MOE_BLOCK, padded // MOE_BLOCK, total, table


def _rope_tables(s, ntok):
    rows_n = s // GRID_W
    row = jnp.repeat(jnp.arange(rows_n, dtype=F32), GRID_W)
    col = jnp.tile(jnp.arange(GRID_W, dtype=F32), rows_n)
    axis_dim = HEAD_DIM // 2
    inv = jnp.power(ROPE_THETA, -jnp.arange(0, axis_dim, 2, dtype=F32) / axis_dim)
    ar = row[:, None] * inv[None]
    ac = col[:, None] * inv[None]
    ang = jnp.concatenate([ar, ar, ac, ac], -1)
    cos, sin = jnp.cos(ang), jnp.sin(ang)
    quarter = (jnp.arange(HEAD_DIM) // 16) % 2
    s_up = jnp.where(quarter == 0, -sin, 0.0)
    s_dn = jnp.where(quarter == 1, sin, 0.0)
    nctx = ntok - s

    def expand(t, ctx_val):
        t = jnp.concatenate([jnp.full((nctx, HEAD_DIM), ctx_val, F32), t], axis=0)
        return jnp.tile(t, (1, LANES // HEAD_DIM))

    return expand(cos, 1.0), expand(s_up, 0.0), expand(s_dn, 0.0)


def _head_mean_matrix(width):
    hid = np.arange(width) // HEAD_DIM
    return jnp.asarray((hid[:, None] == hid[None, :]).astype(np.float32) / HEAD_DIM).astype(BF16)


def _dup_heads(a, n_heads):
    parts = []
    for hd in range(n_heads):
        p = a[..., hd * HEAD_DIM:(hd + 1) * HEAD_DIM]
        parts += [p, p]
    return jnp.concatenate(parts, axis=-1)


def kernel(x, c, ctx, c_ctx, w_mod, b_mod, w_in, qn_a, kn_a, lam_q1, lam_k1, lam_q2, lam_k2, subln_c, w_br_a, w_br_b, w_br_c, w_out, ln1_g, ln1_b, w_router, b_router, w_sh_gate, w_sh_up, w_sh_down, w_e_gate, w_e_up, w_e_down, ln2_g, ln2_b):
    nb, s, d = x.shape
    lc = ctx.shape[1]
    depth = w_mod.shape[0]
    assert d == D_MODEL and lc == CTX_LEN and s % ROW_TILE == 0 and s % GRID_W == 0
    ntok = lc + s
    alpha = (2 * depth) ** 0.25

    tabs = _rope_tables(s, ntok)
    pos_lat = _dft_tables(s)
    pos_ctx = _dft_tables(lc)
    chan = _channel_table()
    e_q = _head_mean_matrix(A_Q_W)
    e_k = _head_mean_matrix(2 * A_KV_W)

    cc = jnp.concatenate([c, c_ctx[None, :]], axis=0)
    cc = jnp.pad(cc, ((0, (-(nb + 1)) % 8), (0, 0)))
    h_all = jnp.concatenate([ctx, x], axis=1).reshape(nb * ntok, d)

    offs = np.cumsum([0, A_Q_W, A_KV_W, A_KV_W, B_W, C_QK_W, C_QK_W, C_V_W, GATE_W])
    out = None
    for l in range(depth):
        last = l == depth - 1
        mod = _mod_call(cc, w_mod[l], b_mod[l])[:nb + 1].reshape(nb + 1, 6, d)
        mod = jnp.pad(mod, ((0, 0), (0, 2), (0, 0)))
        lam_init = 0.8 - 0.6 * math.exp(-0.3 * l)
        lam = (jnp.exp(jnp.sum(lam_q1[l].astype(F32) * lam_k1[l].astype(F32)))
               - jnp.exp(jnp.sum(lam_q2[l].astype(F32) * lam_k2[l].astype(F32)))) + lam_init
        lam = lam.reshape(1, 1).astype(F32)

        wl = w_in[l]
        seg = [wl[:, offs[i]:offs[i + 1]] for i in range(8)]
        in_wts = (seg[0].astype(BF16), _dup_heads(seg[1], A_KV_HEADS).astype(BF16),
                  _dup_heads(seg[2], A_KV_HEADS).astype(BF16), seg[3].astype(BF16),
                  seg[4].astype(BF16), seg[5].astype(BF16), seg[6].astype(BF16),
                  jnp.tile(qn_a[l].astype(F32), A_Q_HEADS).reshape(1, A_Q_W),
                  jnp.tile(kn_a[l].astype(F32), 2 * A_KV_HEADS).reshape(1, 2 * A_KV_W),
                  e_q, e_k)
        qa, ka, va, fb, qc, kc, vc = _inproj_call(h_all, mod, tabs, in_wts, nb, ntok)

        first_tile = 1 if last else 0
        r3 = lambda a: a.reshape(nb, ntok, a.shape[-1])
        subln = subln_c[l].astype(F32).reshape(1, C_V_DIM)
        attn_a = functools.partial(_attn_a_call, r3(qa), ka, r3(va), nb, ntok)
        attn_c = functools.partial(_attn_c_call, lam, r3(qc), kc, r3(vc), subln, nb, ntok)
        oa, oc = attn_a(False), attn_c(False, 1.0 - lam_init)
        oa_ctx, oc_ctx = (oa, oc) if last else (attn_a(True), attn_c(True, 1.0 - lam_init))
        ob = _fourier_call(r3(fb), chan, pos_lat, pos_ctx, nb, ntok, first_tile)

        mix_wts = (seg[7].astype(BF16), w_br_a[l].astype(BF16), w_br_b[l].astype(BF16),
                   w_br_c[l].astype(BF16), w_out[l].astype(BF16),
                   ln1_g[l].astype(F32).reshape(1, d), ln1_b[l].astype(F32).reshape(1, d),
                   w_router[l].T.astype(BF16))
        flat = lambda a: a.reshape(-1, a.shape[-1])
        h1, xp, logits = _mixout_call(h_all, mod, flat(oa), flat(oa_ctx), flat(ob), flat(oc),
                                      flat(oc_ctx), mix_wts, nb, ntok, last, alpha)

        n_tok = h1.shape[0]
        idx, rw, rank, counts = _route_call(logits, b_router[l])
        first_blk, n_blk, total, table = _dispatch(idx, rank, counts, n_tok)
        y8 = _moe_call(first_blk, n_blk, total, table, xp,
                       w_e_gate, w_e_up, w_e_down, l, n_tok)
        y8 = y8.reshape(TOP_K, (n_tok + PAD_ROWS) * PSLAB, LANES)

        comb_wts = (w_sh_gate[l].astype(BF16), w_sh_up[l].astype(BF16), w_sh_down[l].astype(BF16),
                    ln2_g[l].astype(F32).reshape(1, d), ln2_b[l].astype(F32).reshape(1, d))
        h2 = _combine_call(h1, mod, y8, rw, comb_wts, s if last else ntok, not last, nb, alpha)
        if last:
            out = h2.reshape(nb, s, d)
        else:
            h_all = h2
    return out
```
